```python
import jax
import jax.numpy as jnp
from jax import lax
import numpy as np

D_MODEL = 4096
BATCH = 4
SEQ = 2048
DEPTH = 4

CHUNK = 64
HEAD_DIM = D_MODEL // 32
GROUP_HEADS = 16
GROUP_WIDTH = GROUP_HEADS * HEAD_DIM
MIX_WIDTH = 2 * GROUP_WIDTH
AB_IN = 7 * GROUP_WIDTH + 2 * GROUP_HEADS
CD_IN = 6 * GROUP_WIDTH
CONV_K = 4
F_BIAS_LO = 3.0
F_BIAS_HI = 6.0
PAST_CHUNKS = 8
BAND = (PAST_CHUNKS + 1) * CHUNK
REL_MAX = 2 * CHUNK
REL_SIZE = CHUNK + REL_MAX
SB_BLOCK = 128
ROPE_BASE = 10000.0
N_MEM = 256
XA_HEADS = 4
XA_HEAD_DIM = HEAD_DIM
XA_WIDTH = XA_HEADS * XA_HEAD_DIM
N_EXPERTS = 32
TOP_K = 4
EXPERT_FF = D_MODEL // 16
SWIGLU_LIMIT = 7.0
SWIGLU_ALPHA = 1.702
LN_EPS = 1e-5
HN_EPS = 1e-6
DN_ALPHA = (2.0 * DEPTH) ** 0.25
DN_BETA = (8.0 * DEPTH) ** -0.25
N_EVEN = (DEPTH + 1) // 2
N_ODD = DEPTH // 2

kernel_name = 'hybrid_streaming_retention_mlstm_band_stickbreak_moe'

F32 = jnp.float32


def layer_norm(x, g, b):
    xf = x.astype(F32)
    mu = jnp.mean(xf, axis=-1, keepdims=True)
    var = jnp.mean(jnp.square(xf - mu), axis=-1, keepdims=True)
    return ((xf - mu) * lax.rsqrt(var + LN_EPS) * g.astype(F32) + b.astype(F32)).astype(x.dtype)


def head_norm(h, g):
    mu = jnp.mean(h, axis=-1, keepdims=True)
    var = jnp.mean(jnp.square(h - mu), axis=-1, keepdims=True)
    hn = (h - mu) * lax.rsqrt(var + HN_EPS)
    return hn.reshape(h.shape[:-2] + (-1,)) * g.astype(F32)


def to_heads(a):
    return a.reshape(a.shape[:-1] + (GROUP_HEADS, HEAD_DIM)).astype(F32)


def rope_tables(seq_len):
    pos = jnp.arange(seq_len, dtype=F32)
    inv_freq = ROPE_BASE ** (-jnp.arange(0, HEAD_DIM, 2, dtype=F32) / HEAD_DIM)
    ang = pos[:, None] * inv_freq[None, :]
    return jnp.cos(ang), jnp.sin(ang)


def rope(a, cos, sin):
    half = a.shape[-1] // 2
    a1, a2 = a[..., :half], a[..., half:]
    c = cos[None, :, None, :]
    s = sin[None, :, None, :]
    return jnp.concatenate([a1 * c - a2 * s, a1 * s + a2 * c], axis=-1)


def seq_to_chunks(a):
    b_, s_, h_ = a.shape[:3]
    a = a.reshape((b_, s_ // CHUNK, CHUNK, h_) + a.shape[3:])
    return a.transpose((1, 0, 3, 2) + tuple(range(4, a.ndim)))


def chunks_to_seq(o):
    nc, b_, h_, l_, d_ = o.shape
    return o.transpose(1, 0, 3, 2, 4).reshape(b_, nc * l_, h_, d_)


def causal_depthwise_conv(u, w, b):
    y = lax.conv_general_dilated(
        u, w[:, None, :].astype(u.dtype), window_strides=(1,), padding=[(CONV_K - 1, 0)],
        dimension_numbers=('NWC', 'WIO', 'NWC'), feature_group_count=u.shape[-1])
    return y + b.astype(u.dtype)


def retention(q, k, v):
    b_, _, h_, d_ = q.shape
    log_g = jnp.log1p(-jnp.exp2(-(5.0 + jnp.arange(h_, dtype=F32))))
    j = jnp.arange(CHUNK, dtype=F32)
    diff = j[:, None] - j[None, :]
    intra = jnp.where(diff >= 0, jnp.exp(log_g[:, None, None] * jnp.maximum(diff, 0.0)), 0.0)
    q_dec = jnp.exp(log_g[:, None] * (j + 1.0))[..., None]
    k_dec = jnp.exp(log_g[:, None] * (CHUNK - 1.0 - j))[..., None]
    c_dec = jnp.exp(log_g * CHUNK)[:, None, None]

    def step(state, inp):
        qc, kc, vc = inp
        att = jnp.einsum('bhid,bhjd->bhij', qc, kc) * intra
        o = (jnp.einsum('bhij,bhje->bhie', att, vc)
             + jnp.einsum('bhid,bhde->bhie', qc * q_dec, state))
        state = state * c_dec + jnp.einsum('bhjd,bhje->bhde', kc * k_dec, vc)
        return state, o

    state0 = jnp.zeros((b_, h_, d_, v.shape[-1]), F32)
    _, o = lax.scan(step, state0, (seq_to_chunks(q), seq_to_chunks(k), seq_to_chunks(v)))
    return chunks_to_seq(o)


def mlstm(q, k, v, ig, lf):
    b_, _, h_, d_ = q.shape
    causal = jnp.tril(jnp.ones((CHUNK, CHUNK), dtype=bool))

    def step(carry, inp):
        c_st, n_st, m_st = carry
        qc, kc, vc, ic, fc = inp
        bcum = jnp.cumsum(fc, axis=-1)
        log_intra = jnp.where(causal, bcum[..., :, None] - bcum[..., None, :] + ic[..., None, :], -jnp.inf)
        log_cross = bcum + m_st[..., None]
        m_row = jnp.maximum(log_cross, jnp.max(log_intra, axis=-1))
        w_intra = jnp.exp(log_intra - m_row[..., None])
        w_cross = jnp.exp(log_cross - m_row)
        qk = jnp.einsum('bhid,bhjd->bhij', qc, kc) * w_intra
        num = (jnp.einsum('bhij,bhje->bhie', qk, vc)
               + w_cross[..., None] * jnp.einsum('bhid,bhde->bhie', qc, c_st))
        den = jnp.sum(qk, axis=-1) + w_cross * jnp.einsum('bhid,bhd->bhi', qc, n_st)
        h = num / jnp.maximum(jnp.abs(den), jnp.exp(-m_row))[..., None]
        b_last = bcum[..., -1]
        log_state = b_last[..., None] - bcum + ic
        m_new = jnp.maximum(b_last + m_st, jnp.max(log_state, axis=-1))
        decay = jnp.exp(b_last + m_st - m_new)
        kw = kc * jnp.exp(log_state - m_new[..., None])[..., None]
        c_st = decay[..., None, None] * c_st + jnp.einsum('bhjd,bhje->bhde', kw, vc)
        n_st = decay[..., None] * n_st + jnp.sum(kw, axis=2)
        return (c_st, n_st, m_new), h

    carry0 = (jnp.zeros((b_, h_, d_, v.shape[-1]), F32), jnp.zeros((b_, h_, d_), F32),
              jnp.zeros((b_, h_), F32))
    _, h = lax.scan(step, carry0, (seq_to_chunks(q), seq_to_chunks(k), seq_to_chunks(v),
                                   seq_to_chunks(ig), seq_to_chunks(lf)))
    return chunks_to_seq(h)


def chunk_band_attention(q, k, v, rel_bias):
    b_, s_, h_, d_ = q.shape
    past = PAST_CHUNKS * CHUNK
    kp = jnp.pad(k, ((0, 0), (past, 0), (0, 0), (0, 0)))
    vp = jnp.pad(v, ((0, 0), (past, 0), (0, 0), (0, 0)))
    r = jnp.arange(CHUNK)
    jk = jnp.arange(BAND)
    dist = r[:, None] + past - jk[None, :]
    bias = rel_bias.astype(F32)[:, jnp.clip(dist, -(CHUNK - 1), REL_MAX) + (CHUNK - 1)]
    scale = d_ ** -0.5

    def one_chunk(c):
        start = c * CHUNK
        qc = lax.dynamic_slice_in_dim(q, start, CHUNK, axis=1)
        kb = lax.dynamic_slice_in_dim(kp, start, BAND, axis=1)
        vb = lax.dynamic_slice_in_dim(vp, start, BAND, axis=1)
        s = jnp.einsum('bqhd,bkhd->bhqk', qc, kb) * scale + bias
        valid = (start - past + jk) >= 0
        p = jax.nn.softmax(jnp.where(valid, s, -jnp.inf), axis=-1)
        return jnp.einsum('bhqk,bkhd->bqhd', p, vb)

    out = lax.map(one_chunk, jnp.arange(s_ // CHUNK))
    return out.transpose(1, 0, 2, 3, 4).reshape(b_, s_, h_, d_)


def stick_breaking_attention(q, k, v):
    b_, s_, h_, d_ = q.shape
    scale = d_ ** -0.5
    key_pos = jnp.arange(s_)

    def one_block(blk):
        start = blk * SB_BLOCK
        qb = lax.dynamic_slice_in_dim(q, start, SB_BLOCK, axis=1)
        z = jnp.einsum('bqhd,bkhd->bhqk', qb, k) * scale
        strict = key_pos[None, :] < (start + jnp.arange(SB_BLOCK))[:, None]
        log_rest = jnp.where(strict, jax.nn.log_sigmoid(-z), 0.0)
        between = lax.cumsum(log_rest, axis=3, reverse=True) - log_rest
        log_a = jnp.where(strict, jax.nn.log_sigmoid(z) + between, -jnp.inf)
        return jnp.einsum('bhqk,bkhd->bqhd', jnp.exp(log_a), v)

    out = lax.map(one_block, jnp.arange(s_ // SB_BLOCK))
    return out.transpose(1, 0, 2, 3, 4).reshape(b_, s_, h_, d_)


def mixer_ab(x, w_in, gate_b, conv_w, conv_b, wq_m, wk_m, ret_g, mlstm_g, w_out, cos, sin):
    wd = GROUP_WIDTH
    scale = HEAD_DIM ** -0.5
    z = x @ w_in
    rq, rk, rv, rg, mu, mv, mo = [z[..., i * wd:(i + 1) * wd] for i in range(7)]
    gates = z[..., 7 * wd:].astype(F32) + gate_b.astype(F32)
    ret = retention(rope(to_heads(rq), cos, sin), rope(to_heads(rk), cos, sin) * scale, to_heads(rv))
    out_a = head_norm(ret, ret_g) * jax.nn.silu(rg.astype(F32))
    u = to_heads(jax.nn.silu(causal_depthwise_conv(mu, conv_w, conv_b)))
    qm = jnp.einsum('bshd,hde->bshe', u, wq_m.astype(F32))
    km = jnp.einsum('bshd,hde->bshe', u, wk_m.astype(F32)) * scale
    ig = gates[..., :GROUP_HEADS]
    lf = jax.nn.log_sigmoid(gates[..., GROUP_HEADS:])
    h = mlstm(qm, km, to_heads(mv), ig, lf) * jax.nn.sigmoid(to_heads(mo))
    out_b = head_norm(h, mlstm_g)
    y = jnp.concatenate([out_a, out_b], axis=-1).astype(x.dtype)
    return y @ w_out


def mixer_cd(x, w_in, rel_bias, w_out):
    z = x @ w_in
    cq, ck, cv, sq, sk, sv = jnp.split(z, 6, axis=-1)
    out_c = chunk_band_attention(to_heads(cq), to_heads(ck), to_heads(cv), rel_bias)
    out_d = stick_breaking_attention(to_heads(sq), to_heads(sk), to_heads(sv))
    y = jnp.concatenate([out_c, out_d], axis=-2)
    y = y.reshape(x.shape[:2] + (MIX_WIDTH,)).astype(x.dtype)
    return y @ w_out


def memory_cross_attention(x, mem, wq, wkv, wo):
    b_, s_, _ = x.shape
    q = (x @ wq).reshape(b_, s_, XA_HEADS, XA_HEAD_DIM).astype(F32)
    kv = (mem @ wkv).reshape(b_, mem.shape[1], 2, XA_HEADS, XA_HEAD_DIM).astype(F32)
    s = jnp.einsum('bqhd,bkhd->bhqk', q, kv[:, :, 0]) * (XA_HEAD_DIM ** -0.5)
    p = jax.nn.softmax(s, axis=-1)
    o = jnp.einsum('bhqk,bkhd->bqhd', p, kv[:, :, 1]).reshape(b_, s_, XA_WIDTH)
    return o.astype(x.dtype) @ wo


def moe(x, router_w, router_b, w_gate, b_gate, w_up, b_up, w_down, b_down):
    b_, s_, d_ = x.shape
    xt = x.reshape(-1, d_)
    logits = (xt @ router_w).astype(F32) + router_b.astype(F32)
    top_v, top_i = lax.top_k(logits, TOP_K)
    top_w = jax.nn.softmax(top_v, axis=-1)
    gate = jnp.einsum('tk,tke->te', top_w, jax.nn.one_hot(top_i, N_EXPERTS, dtype=F32))
    g = (jnp.einsum('td,edf->tef', xt, w_gate) + b_gate).astype(F32)
    u = (jnp.einsum('td,edf->tef', xt, w_up) + b_up).astype(F32)
    g = jnp.minimum(g, SWIGLU_LIMIT)
    u = jnp.clip(u, -SWIGLU_LIMIT, SWIGLU_LIMIT)
    act = g * jax.nn.sigmoid(SWIGLU_ALPHA * g) * (u + 1.0)
    h = (act * gate[..., None]).astype(x.dtype)
    y = jnp.einsum('tef,efd->td', h, w_down) + gate.astype(x.dtype) @ b_down
    return y.reshape(b_, s_, d_)


def setup_inputs(seed: int = 0) -> dict:
    key = jax.random.key(seed)
    ks = iter(jax.random.split(key, 48))
    d, w, h, hd = D_MODEL, GROUP_WIDTH, GROUP_HEADS, HEAD_DIM

    def nrm(shape, scale):
        return jax.random.normal(next(ks), shape, F32) * scale

    def gain(shape):
        return 1.0 + nrm(shape, 0.02)

    x = nrm((BATCH, SEQ, d), 1.0)
    mem = nrm((BATCH, N_MEM, d), 1.0)
    ab_w_in = nrm((N_EVEN, d, AB_IN), d ** -0.5)
    f_bias = jnp.linspace(F_BIAS_LO, F_BIAS_HI, h, dtype=F32)[None, :] + nrm((N_EVEN, h), 0.01)
    ab_gate_b = jnp.concatenate([nrm((N_EVEN, h), 0.1), f_bias], axis=-1)
    ab_conv_w = nrm((N_EVEN, CONV_K, w), CONV_K ** -0.5)
    ab_conv_b = nrm((N_EVEN, w), 0.02)
    ab_wq = nrm((N_EVEN, h, hd, hd), hd ** -0.5)
    ab_wk = nrm((N_EVEN, h, hd, hd), hd ** -0.5)
    ab_ret_norm_g = gain((N_EVEN, w))
    ab_mlstm_norm_g = gain((N_EVEN, w))
    ab_w_out = nrm((N_EVEN, MIX_WIDTH, d), MIX_WIDTH ** -0.5 * DN_BETA)
    cd_w_in = nrm((N_ODD, d, CD_IN), d ** -0.5)
    cd_rel_bias = nrm((N_ODD, h, REL_SIZE), 0.2)
    cd_w_out = nrm((N_ODD, MIX_WIDTH, d), MIX_WIDTH ** -0.5 * DN_BETA)
    mix_ln_g = gain((DEPTH, d))
    mix_ln_b = nrm((DEPTH, d), 0.02)
    xa_wq = nrm((DEPTH, d, XA_WIDTH), d ** -0.5)
    xa_wkv = nrm((DEPTH, d, 2 * XA_WIDTH), d ** -0.5)
    xa_wo = nrm((DEPTH, XA_WIDTH, d), XA_WIDTH ** -0.5 * DN_BETA)
    xa_ln_g = gain((DEPTH, d))
    xa_ln_b = nrm((DEPTH, d), 0.02)
    moe_router_w = nrm((DEPTH, d, N_EXPERTS), d ** -0.5)
    moe_router_b = nrm((DEPTH, N_EXPERTS), 0.01)
    moe_w_gate = nrm((DEPTH, N_EXPERTS, d, EXPERT_FF), d ** -0.5)
    moe_b_gate = nrm((DEPTH, N_EXPERTS, EXPERT_FF), 0.02)
    moe_w_up = nrm((DEPTH, N_EXPERTS, d, EXPERT_FF), d ** -0.5)
    moe_b_up = nrm((DEPTH, N_EXPERTS, EXPERT_FF), 0.02)
    moe_w_down = nrm((DEPTH, N_EXPERTS, EXPERT_FF, d), EXPERT_FF ** -0.5 * DN_BETA)
    moe_b_down = nrm((DEPTH, N_EXPERTS, d), 0.02)
    moe_ln_g = gain((DEPTH, d))
    moe_ln_b = nrm((DEPTH, d), 0.02)
    return {
        'x': x, 'mem': mem,
        'ab_w_in': ab_w_in, 'ab_gate_b': ab_gate_b, 'ab_conv_w': ab_conv_w, 'ab_conv_b': ab_conv_b,
        'ab_wq': ab_wq, 'ab_wk': ab_wk, 'ab_ret_norm_g': ab_ret_norm_g,
        'ab_mlstm_norm_g': ab_mlstm_norm_g, 'ab_w_out': ab_w_out,
        'cd_w_in': cd_w_in, 'cd_rel_bias': cd_rel_bias, 'cd_w_out': cd_w_out,
        'mix_ln_g': mix_ln_g, 'mix_ln_b': mix_ln_b,
        'xa_wq': xa_wq, 'xa_wkv': xa_wkv, 'xa_wo': xa_wo, 'xa_ln_g': xa_ln_g, 'xa_ln_b': xa_ln_b,
        'moe_router_w': moe_router_w, 'moe_router_b': moe_router_b,
        'moe_w_gate': moe_w_gate, 'moe_b_gate': moe_b_gate, 'moe_w_up': moe_w_up, 'moe_b_up': moe_b_up,
        'moe_w_down': moe_w_down, 'moe_b_down': moe_b_down, 'moe_ln_g': moe_ln_g, 'moe_ln_b': moe_ln_b,
    }


def reference(x, mem, ab_w_in, ab_gate_b, ab_conv_w, ab_conv_b, ab_wq, ab_wk, ab_ret_norm_g,
              ab_mlstm_norm_g, ab_w_out, cd_w_in, cd_rel_bias, cd_w_out, mix_ln_g, mix_ln_b,
              xa_wq, xa_wkv, xa_wo, xa_ln_g, xa_ln_b, moe_router_w, moe_router_b,
              moe_w_gate, moe_b_gate, moe_w_up, moe_b_up, moe_w_down, moe_b_down,
              moe_ln_g, moe_ln_b):
    cos, sin = rope_tables(x.shape[1])
    h = x
    for layer in range(DEPTH):
        i = layer // 2
        if layer % 2 == 0:
            y = mixer_ab(h, ab_w_in[i], ab_gate_b[i], ab_conv_w[i], ab_conv_b[i], ab_wq[i], ab_wk[i],
                         ab_ret_norm_g[i], ab_mlstm_norm_g[i], ab_w_out[i], cos, sin)
        else:
            y = mixer_cd(h, cd_w_in[i], cd_rel_bias[i], cd_w_out[i])
        h = layer_norm(DN_ALPHA * h + y, mix_ln_g[layer], mix_ln_b[layer])
        y = memory_cross_attention(h, mem, xa_wq[layer], xa_wkv[layer], xa_wo[layer])
        h = layer_norm(DN_ALPHA * h + y, xa_ln_g[layer], xa_ln_b[layer])
        y = moe(h, moe_router_w[layer], moe_router_b[layer], moe_w_gate[layer], moe_b_gate[layer],
                moe_w_up[layer], moe_b_up[layer], moe_w_down[layer], moe_b_down[layer])
        h = layer_norm(DN_ALPHA * h + y, moe_ln_g[layer], moe_ln_b[layer])
    return h.astype(x.dtype)
```

```python
import functools

import jax
import jax.numpy as jnp
from jax import lax
from jax.experimental import pallas as pl
from jax.experimental.pallas import tpu as pltpu

F32 = jnp.float32
BF16 = jnp.bfloat16

CHUNK = 64
HEAD_DIM = 128
GROUP_HEADS = 16
GROUP_WIDTH = GROUP_HEADS * HEAD_DIM
CONV_K = 4
PAST_CHUNKS = 8
BAND = (PAST_CHUNKS + 1) * CHUNK
REL_MAX = 2 * CHUNK
SB_BLOCK = 128
ROPE_BASE = 10000.0
XA_HEADS = 4
N_EXPERTS = 32
TOP_K = 4
SWIGLU_LIMIT = 7.0
SWIGLU_ALPHA = 1.702
LN_EPS = 1e-5
HN_EPS = 1e-6

LANES = 128
VMEM_LIMIT_BYTES = 58 * 1024 * 1024


def _params(*sem):
    return pltpu.CompilerParams(dimension_semantics=sem, vmem_limit_bytes=VMEM_LIMIT_BYTES)


def _mm_kernel(x_ref, w_ref, o_ref, wb_ref):
    @pl.when(pl.program_id(1) == 0)
    def _():
        wb_ref[...] = w_ref[...].astype(BF16)

    o_ref[...] = jnp.dot(x_ref[...], wb_ref[...], preferred_element_type=F32).astype(o_ref.dtype)


def matmul_stacked(x, w, layer, n_cols, *, tm, tn, out_dtype):
    m, k = x.shape
    return pl.pallas_call(
        _mm_kernel,
        grid=(n_cols // tn, m // tm),
        in_specs=[
            pl.BlockSpec((tm, k), lambda j, i: (i, 0)),
            pl.BlockSpec((None, k, tn), lambda j, i: (layer, 0, j)),
        ],
        out_specs=pl.BlockSpec((tm, tn), lambda j, i: (i, j)),
        out_shape=jax.ShapeDtypeStruct((m, n_cols), out_dtype),
        scratch_shapes=[pltpu.VMEM((k, tn), BF16)],
        compiler_params=_params("arbitrary", "arbitrary"),
        name="mm_in",
    )(x, w)


def _mm_small_kernel(x_ref, w_ref, b_ref, o_ref, *, precision):
    o_ref[...] = jnp.dot(x_ref[...], w_ref[...], preferred_element_type=F32,
                         precision=precision) + b_ref[...]


def matmul_small(x, w, b, *, tm, precision=None):
    m, k = x.shape
    n = w.shape[1]
    return pl.pallas_call(
        functools.partial(_mm_small_kernel, precision=precision),
        grid=(m // tm,),
        in_specs=[
            pl.BlockSpec((tm, k), lambda i: (i, 0)),
            pl.BlockSpec((k, n), lambda i: (0, 0)),
            pl.BlockSpec((1, n), lambda i: (0, 0)),
        ],
        out_specs=pl.BlockSpec((tm, n), lambda i: (i, 0)),
        out_shape=jax.ShapeDtypeStruct((m, n), F32),
        compiler_params=_params("arbitrary"),
        name="mm_small",
    )(x, w, b)


def _ln_rows(z_ref, g_ref, b_ref, of_ref, ob_ref, rows):
    tm = z_ref.shape[0]

    def body(r, carry):
        sl = pl.ds(pl.multiple_of(r * rows, rows), rows)
        z = z_ref[sl, :]
        mu = jnp.mean(z, axis=-1, keepdims=True)
        zc = z - mu
        var = jnp.mean(zc * zc, axis=-1, keepdims=True)
        y = zc * lax.rsqrt(var + LN_EPS) * g_ref[...] + b_ref[...]
        of_ref[sl, :] = y
        ob_ref[sl, :] = y.astype(BF16)
        return carry

    lax.fori_loop(0, tm // rows, body, 0)


def _mm_ln_kernel(*refs, nk, nj, tn, alpha, has_extra):
    if has_extra:
        x_ref, w_ref, h_ref, g_ref, b_ref, e_ref, we_ref, of_ref, ob_ref = refs
    else:
        x_ref, w_ref, h_ref, g_ref, b_ref, of_ref, ob_ref = refs
    k = pl.program_id(1)
    j = pl.program_id(2)
    part = jnp.dot(x_ref[...], w_ref[...].astype(BF16), preferred_element_type=F32)
    for jj in range(nj):
        sl = slice(jj * tn, (jj + 1) * tn)

        @pl.when((j == jj) & (k == 0))
        def _():
            of_ref[:, sl] = alpha * h_ref[...] + part

        @pl.when((j == jj) & (k > 0))
        def _():
            of_ref[:, sl] += part

    @pl.when((k == nk - 1) & (j == nj - 1))
    def _():
        if has_extra:
            of_ref[...] += jnp.dot(e_ref[...], we_ref[...], preferred_element_type=F32)
        _ln_rows(of_ref, g_ref, b_ref, of_ref, ob_ref, 32)


def matmul_ln(x, w, w_layer, h, g, b, alpha, *, extra=None, w_extra=None, tm=512, tn=512, tk=4096):
    m, kdim = x.shape
    n = h.shape[1]
    tk = min(tk, kdim)
    nk, nj = kdim // tk, n // tn
    if w.ndim == 3:
        w_spec = pl.BlockSpec((None, tk, tn), lambda i, k, j: (w_layer, k, j))
    else:
        w_spec = pl.BlockSpec((tk, tn), lambda i, k, j: (k, j))
    in_specs = [
        pl.BlockSpec((tm, tk), lambda i, k, j: (i, k)),
        w_spec,
        pl.BlockSpec((tm, tn), lambda i, k, j: (i, j)),
        pl.BlockSpec((1, n), lambda i, k, j: (0, 0)),
        pl.BlockSpec((1, n), lambda i, k, j: (0, 0)),
    ]
    args = [x, w, h, g, b]
    if extra is not None:
        in_specs += [
            pl.BlockSpec((tm, extra.shape[1]), lambda i, k, j: (i, 0)),
            pl.BlockSpec(w_extra.shape, lambda i, k, j: (0, 0)),
        ]
        args += [extra, w_extra]
    return pl.pallas_call(
        functools.partial(_mm_ln_kernel, nk=nk, nj=nj, tn=tn, alpha=alpha, has_extra=extra is not None),
        grid=(m // tm, nk, nj),
        in_specs=in_specs,
        out_specs=[
            pl.BlockSpec((tm, n), lambda i, k, j: (i, 0)),
            pl.BlockSpec((tm, n), lambda i, k, j: (i, 0)),
        ],
        out_shape=[jax.ShapeDtypeStruct((m, n), F32), jax.ShapeDtypeStruct((m, n), BF16)],
        compiler_params=_params("arbitrary", "arbitrary", "arbitrary"),
        name="mm_ln",
    )(*args)


def _xattn_kernel(hb_ref, h_ref, wq_ref, kv_ref, wo_ref, g_ref, b_ref, of_ref, ob_ref, z_ref, *, alpha, heads, hd):
    q = jnp.dot(hb_ref[...], wq_ref[...], preferred_element_type=F32)
    scale = hd ** -0.5
    outs = []
    for hh in range(heads):
        qh = (q[:, hh * hd:(hh + 1) * hd] * scale).astype(BF16)
        kh = kv_ref[:, hh * hd:(hh + 1) * hd]
        vh = kv_ref[:, (heads + hh) * hd:(heads + hh + 1) * hd]
        s = lax.dot_general(qh, kh, (((1,), (1,)), ((), ())), preferred_element_type=F32)
        s = s - jnp.max(s, axis=-1, keepdims=True)
        p = jnp.exp(s)
        l = jnp.sum(p, axis=-1, keepdims=True)
        o = jnp.dot(p.astype(BF16), vh, preferred_element_type=F32) / l
        outs.append(o.astype(BF16))
    o_all = jnp.concatenate(outs, axis=-1)
    z_ref[...] = alpha * h_ref[...] + jnp.dot(o_all, wo_ref[...], preferred_element_type=F32)
    _ln_rows(z_ref, g_ref, b_ref, of_ref, ob_ref, 32)


def xattn_ln(hb, h, wq_b, kv_b, wo_b, g, b, alpha, *, seq, tm=256):
    m, d = h.shape
    xw = wq_b.shape[1]
    n_mem = kv_b.shape[0] // (m // seq)
    per_b = seq // tm
    return pl.pallas_call(
        functools.partial(_xattn_kernel, alpha=alpha, heads=XA_HEADS, hd=xw // XA_HEADS),
        grid=(m // tm,),
        in_specs=[
            pl.BlockSpec((tm, d), lambda i: (i, 0)),
            pl.BlockSpec((tm, d), lambda i: (i, 0)),
            pl.BlockSpec((d, xw), lambda i: (0, 0)),
            pl.BlockSpec((n_mem, 2 * xw), lambda i: (i // per_b, 0)),
            pl.BlockSpec((xw, d), lambda i: (0, 0)),
            pl.BlockSpec((1, d), lambda i: (0, 0)),
            pl.BlockSpec((1, d), lambda i: (0, 0)),
        ],
        out_specs=[pl.BlockSpec((tm, d), lambda i: (i, 0)), pl.BlockSpec((tm, d), lambda i: (i, 0))],
        out_shape=[jax.ShapeDtypeStruct((m, d), F32), jax.ShapeDtypeStruct((m, d), BF16)],
        scratch_shapes=[pltpu.VMEM((tm, d), F32)],
        compiler_params=_params("arbitrary"),
        name="xattn_ln",
    )(hb, h, wq_b, kv_b, wo_b, g, b)


def _router_kernel(h_ref, w_ref, b_ref, gate_ref):
    logits = jnp.dot(h_ref[...], w_ref[...], preferred_element_type=F32,
                     precision=lax.Precision.HIGHEST) + b_ref[...]
    lane = lax.broadcasted_iota(jnp.int32, logits.shape, 1)
    neg = jnp.float32(-jnp.inf)
    masked = jnp.where(lane < N_EXPERTS, logits, neg)
    top_vals, top_hot = [], []
    for _ in range(TOP_K):
        mval = jnp.max(masked, axis=-1, keepdims=True)
        idx = jnp.min(jnp.where(masked == mval, lane, LANES), axis=-1, keepdims=True)
        hot = lane == idx
        top_vals.append(mval)
        top_hot.append(hot)
        masked = jnp.where(hot, neg, masked)
    exps = [jnp.exp(v - top_vals[0]) for v in top_vals]
    denom = exps[0]
    for e in exps[1:]:
        denom = denom + e
    gate = jnp.zeros_like(logits)
    for e, hot in zip(exps, top_hot):
        gate = jnp.where(hot, e / denom, gate)
    gate_ref[...] = gate


def moe_router(h, w_pad, b_pad, *, tm=512):
    m, d = h.shape
    return pl.pallas_call(
        _router_kernel,
        grid=(m // tm,),
        in_specs=[
            pl.BlockSpec((tm, d), lambda i: (i, 0)),
            pl.BlockSpec((d, LANES), lambda i: (0, 0)),
            pl.BlockSpec((1, LANES), lambda i: (0, 0)),
        ],
        out_specs=pl.BlockSpec((tm, LANES), lambda i: (i, 0)),
        out_shape=jax.ShapeDtypeStruct((m, LANES), F32),
        compiler_params=_params("arbitrary"),
        name="moe_router",
    )(h, w_pad, b_pad)


def _moe_up_kernel(x_ref, wg_ref, wu_ref, bg_ref, bu_ref, gate_ref, o_ref, wb_ref, *, ff):
    e = pl.program_id(0)

    @pl.when(pl.program_id(1) == 0)
    def _():
        wb_ref[:, :ff] = wg_ref[...].astype(BF16)
        wb_ref[:, ff:] = wu_ref[...].astype(BF16)

    gu = jnp.dot(x_ref[...], wb_ref[...], preferred_element_type=F32)
    g = jnp.minimum(gu[:, :ff] + bg_ref[...], SWIGLU_LIMIT)
    u = jnp.clip(gu[:, ff:] + bu_ref[...], -SWIGLU_LIMIT, SWIGLU_LIMIT)
    act = g * jax.nn.sigmoid(SWIGLU_ALPHA * g) * (u + 1.0)
    gate = gate_ref[...]
    lane = lax.broadcasted_iota(jnp.int32, gate.shape, 1)
    gcol = jnp.sum(jnp.where(lane == e, gate, 0.0), axis=-1, keepdims=True)
    o_ref[...] = (act * gcol).astype(o_ref.dtype)


def moe_up_dense(xb, w_gate, b_gate, w_up, b_up, gate, layer, *, tm=1024):
    m, d = xb.shape
    n_e, ff = w_gate.shape[1], w_gate.shape[3]
    bg = b_gate.reshape(b_gate.shape[0], n_e, 1, ff)
    bu = b_up.reshape(b_up.shape[0], n_e, 1, ff)
    w_spec = pl.BlockSpec((None, None, d, ff), lambda e, i: (layer, e, 0, 0))
    b_spec = pl.BlockSpec((None, None, 1, ff), lambda e, i: (layer, e, 0, 0))
    return pl.pallas_call(
        functools.partial(_moe_up_kernel, ff=ff),
        grid=(n_e, m // tm),
        in_specs=[
            pl.BlockSpec((tm, d), lambda e, i: (i, 0)),
            w_spec, w_spec, b_spec, b_spec,
            pl.BlockSpec((tm, LANES), lambda e, i: (i, 0)),
        ],
        out_specs=pl.BlockSpec((tm, ff), lambda e, i: (i, e)),
        out_shape=jax.ShapeDtypeStruct((m, n_e * ff), BF16),
        scratch_shapes=[pltpu.VMEM((d, 2 * ff), BF16)],
        compiler_params=_params("arbitrary", "arbitrary"),
        name="moe_up",
    )(xb, w_gate, w_up, bg, bu, gate)


def _head_norm(h, g):
    mu = jnp.mean(h, axis=-1, keepdims=True)
    var = jnp.mean(jnp.square(h - mu), axis=-1, keepdims=True)
    hn = (h - mu) * lax.rsqrt(var + HN_EPS)
    return hn.reshape(h.shape[:-2] + (-1,)) * g.astype(F32)


def _to_heads(a):
    return a.reshape(a.shape[:-1] + (GROUP_HEADS, HEAD_DIM)).astype(F32)


def _rope_tables(seq_len):
    pos = jnp.arange(seq_len, dtype=F32)
    inv_freq = ROPE_BASE ** (-jnp.arange(0, HEAD_DIM, 2, dtype=F32) / HEAD_DIM)
    ang = pos[:, None] * inv_freq[None, :]
    return jnp.cos(ang), jnp.sin(ang)


def _rope(a, cos, sin):
    half = a.shape[-1] // 2
    a1, a2 = a[..., :half], a[..., half:]
    c = cos[None, :, None, :]
    s = sin[None, :, None, :]
    return jnp.concatenate([a1 * c - a2 * s, a1 * s + a2 * c], axis=-1)


def _seq_to_chunks(a):
    b_, s_, h_ = a.shape[:3]
    a = a.reshape((b_, s_ // CHUNK, CHUNK, h_) + a.shape[3:])
    return a.transpose((1, 0, 3, 2) + tuple(range(4, a.ndim)))


def _chunks_to_seq(o):
    nc, b_, h_, l_, d_ = o.shape
    return o.transpose(1, 0, 3, 2, 4).reshape(b_, nc * l_, h_, d_)


def _causal_depthwise_conv(u, w, b):
    y = lax.conv_general_dilated(
        u, w[:, None, :].astype(u.dtype), window_strides=(1,), padding=[(CONV_K - 1, 0)],
        dimension_numbers=('NWC', 'WIO', 'NWC'), feature_group_count=u.shape[-1])
    return y + b.astype(u.dtype)


def _retention(q, k, v):
    b_, _, h_, d_ = q.shape
    log_g = jnp.log1p(-jnp.exp2(-(5.0 + jnp.arange(h_, dtype=F32))))
    j = jnp.arange(CHUNK, dtype=F32)
    diff = j[:, None] - j[None, :]
    intra = jnp.where(diff >= 0, jnp.exp(log_g[:, None, None] * jnp.maximum(diff, 0.0)), 0.0)
    q_dec = jnp.exp(log_g[:, None] * (j + 1.0))[..., None]
    k_dec = jnp.exp(log_g[:, None] * (CHUNK - 1.0 - j))[..., None]
    c_dec = jnp.exp(log_g * CHUNK)[:, None, None]

    def step(state, inp):
        qc, kc, vc = inp
        att = jnp.einsum('bhid,bhjd->bhij', qc, kc) * intra
        o = (jnp.einsum('bhij,bhje->bhie', att, vc)
             + jnp.einsum('bhid,bhde->bhie', qc * q_dec, state))
        state = state * c_dec + jnp.einsum('bhjd,bhje->bhde', kc * k_dec, vc)
        return state, o

    state0 = jnp.zeros((b_, h_, d_, v.shape[-1]), F32)
    _, o = lax.scan(step, state0, (_seq_to_chunks(q), _seq_to_chunks(k), _seq_to_chunks(v)))
    return _chunks_to_seq(o)


def _mlstm(q, k, v, ig, lf):
    b_, _, h_, d_ = q.shape
    causal = jnp.tril(jnp.ones((CHUNK, CHUNK), dtype=bool))

    def step(carry, inp):
        c_st, n_st, m_st = carry
        qc, kc, vc, ic, fc = inp
        bcum = jnp.cumsum(fc, axis=-1)
        log_intra = jnp.where(causal, bcum[..., :, None] - bcum[..., None, :] + ic[..., None, :], -jnp.inf)
        log_cross = bcum + m_st[..., None]
        m_row = jnp.maximum(log_cross, jnp.max(log_intra, axis=-1))
        w_intra = jnp.exp(log_intra - m_row[..., None])
        w_cross = jnp.exp(log_cross - m_row)
        qk = jnp.einsum('bhid,bhjd->bhij', qc, kc) * w_intra
        num = (jnp.einsum('bhij,bhje->bhie', qk, vc)
               + w_cross[..., None] * jnp.einsum('bhid,bhde->bhie', qc, c_st))
        den = jnp.sum(qk, axis=-1) + w_cross * jnp.einsum('bhid,bhd->bhi', qc, n_st)
        h = num / jnp.maximum(jnp.abs(den), jnp.exp(-m_row))[..., None]
        b_last = bcum[..., -1]
        log_state = b_last[..., None] - bcum + ic
        m_new = jnp.maximum(b_last + m_st, jnp.max(log_state, axis=-1))
        decay = jnp.exp(b_last + m_st - m_new)
        kw = kc * jnp.exp(log_state - m_new[..., None])[..., None]
        c_st = decay[..., None, None] * c_st + jnp.einsum('bhjd,bhje->bhde', kw, vc)
        n_st = decay[..., None] * n_st + jnp.sum(kw, axis=2)
        return (c_st, n_st, m_new), h

    carry0 = (jnp.zeros((b_, h_, d_, v.shape[-1]), F32), jnp.zeros((b_, h_, d_), F32),
              jnp.zeros((b_, h_), F32))
    _, h = lax.scan(step, carry0, (_seq_to_chunks(q), _seq_to_chunks(k), _seq_to_chunks(v),
                                   _seq_to_chunks(ig), _seq_to_chunks(lf)))
    return _chunks_to_seq(h)


def _chunk_band_attention(q, k, v, rel_bias):
    b_, s_, h_, d_ = q.shape
    past = PAST_CHUNKS * CHUNK
    kp = jnp.pad(k, ((0, 0), (past, 0), (0, 0), (0, 0)))
    vp = jnp.pad(v, ((0, 0), (past, 0), (0, 0), (0, 0)))
    r = jnp.arange(CHUNK)
    jk = jnp.arange(BAND)
    dist = r[:, None] + past - jk[None, :]
    bias = rel_bias.astype(F32)[:, jnp.clip(dist, -(CHUNK - 1), REL_MAX) + (CHUNK - 1)]
    scale = d_ ** -0.5

    def one_chunk(c):
        start = c * CHUNK
        qc = lax.dynamic_slice_in_dim(q, start, CHUNK, axis=1)
        kb = lax.dynamic_slice_in_dim(kp, start, BAND, axis=1)
        vb = lax.dynamic_slice_in_dim(vp, start, BAND, axis=1)
        s = jnp.einsum('bqhd,bkhd->bhqk', qc, kb) * scale + bias
        valid = (start - past + jk) >= 0
        p = jax.nn.softmax(jnp.where(valid, s, -jnp.inf), axis=-1)
        return jnp.einsum('bhqk,bkhd->bqhd', p, vb)

    out = lax.map(one_chunk, jnp.arange(s_ // CHUNK))
    return out.transpose(1, 0, 2, 3, 4).reshape(b_, s_, h_, d_)


def _stick_breaking_attention(q, k, v):
    b_, s_, h_, d_ = q.shape
    scale = d_ ** -0.5
    key_pos = jnp.arange(s_)

    def one_block(blk):
        start = blk * SB_BLOCK
        qb = lax.dynamic_slice_in_dim(q, start, SB_BLOCK, axis=1)
        z = jnp.einsum('bqhd,bkhd->bhqk', qb, k) * scale
        strict = key_pos[None, :] < (start + jnp.arange(SB_BLOCK))[:, None]
        log_rest = jnp.where(strict, jax.nn.log_sigmoid(-z), 0.0)
        between = lax.cumsum(log_rest, axis=3, reverse=True) - log_rest
        log_a = jnp.where(strict, jax.nn.log_sigmoid(z) + between, -jnp.inf)
        return jnp.einsum('bhqk,bkhd->bqhd', jnp.exp(log_a), v)

    out = lax.map(one_block, jnp.arange(s_ // SB_BLOCK))
    return out.transpose(1, 0, 2, 3, 4).reshape(b_, s_, h_, d_)


def _mixer_ab_core(z, gates, conv_w, conv_b, wq_m, wk_m, ret_g, mlstm_g, cos, sin):
    wd = GROUP_WIDTH
    scale = HEAD_DIM ** -0.5
    z = z.astype(F32)
    rq, rk, rv, rg, mu, mv, mo = [z[..., i * wd:(i + 1) * wd] for i in range(7)]
    ret = _retention(_rope(_to_heads(rq), cos, sin), _rope(_to_heads(rk), cos, sin) * scale, _to_heads(rv))
    out_a = _head_norm(ret, ret_g) * jax.nn.silu(rg)
    u = _to_heads(jax.nn.silu(_causal_depthwise_conv(mu, conv_w, conv_b)))
    qm = jnp.einsum('bshd,hde->bshe', u, wq_m.astype(F32))
    km = jnp.einsum('bshd,hde->bshe', u, wk_m.astype(F32)) * scale
    ig = gates[..., :GROUP_HEADS]
    lf = jax.nn.log_sigmoid(gates[..., GROUP_HEADS:])
    h = _mlstm(qm, km, _to_heads(mv), ig, lf) * jax.nn.sigmoid(_to_heads(mo))
    out_b = _head_norm(h, mlstm_g)
    return jnp.concatenate([out_a, out_b], axis=-1)


def _mixer_cd_core(z, rel_bias):
    z = z.astype(F32)
    cq, ck, cv, sq, sk, sv = jnp.split(z, 6, axis=-1)
    out_c = _chunk_band_attention(_to_heads(cq), _to_heads(ck), _to_heads(cv), rel_bias)
    out_d = _stick_breaking_attention(_to_heads(sq), _to_heads(sk), _to_heads(sv))
    y = jnp.concatenate([out_c, out_d], axis=-2)
    return y.reshape(z.shape[:2] + (2 * GROUP_WIDTH,))


def _pad_lanes(a, value=0.0):
    return jnp.pad(a, ((0, 0), (0, LANES - a.shape[1])), constant_values=value)


def kernel(x, mem, ab_w_in, ab_gate_b, ab_conv_w, ab_conv_b, ab_wq, ab_wk, ab_ret_norm_g, ab_mlstm_norm_g, ab_w_out, cd_w_in, cd_rel_bias, cd_w_out, mix_ln_g, mix_ln_b, xa_wq, xa_wkv, xa_wo, xa_ln_g, xa_ln_b, moe_router_w, moe_router_b, moe_w_gate, moe_b_gate, moe_w_up, moe_b_up, moe_w_down, moe_b_down, moe_ln_g, moe_ln_b):
    bsz, seq, d = x.shape
    depth = mix_ln_g.shape[0]
    alpha = (2.0 * depth) ** 0.25
    t = bsz * seq
    cos, sin = _rope_tables(seq)
    h = x.reshape(t, d)
    hb = h.astype(BF16)
    memb = mem.reshape(-1, d).astype(BF16)
    n_ab = 7 * GROUP_WIDTH
    n_cd = 6 * GROUP_WIDTH
    for layer in range(depth):
        i = layer // 2
        if layer % 2 == 0:
            z = matmul_stacked(hb, ab_w_in, i, n_ab, tm=1024, tn=512, out_dtype=BF16)
            wg = _pad_lanes(ab_w_in[i, :, n_ab:]).astype(BF16)
            gates = matmul_small(hb, wg, _pad_lanes(ab_gate_b[i][None, :]), tm=1024)[:, :2 * GROUP_HEADS]
            y = _mixer_ab_core(z.reshape(bsz, seq, n_ab), gates.reshape(bsz, seq, -1), ab_conv_w[i], ab_conv_b[i],
                               ab_wq[i], ab_wk[i], ab_ret_norm_g[i], ab_mlstm_norm_g[i], cos, sin)
            w_out = ab_w_out[i].astype(BF16)
        else:
            z = matmul_stacked(hb, cd_w_in, i, n_cd, tm=1024, tn=512, out_dtype=BF16)
            y = _mixer_cd_core(z.reshape(bsz, seq, n_cd), cd_rel_bias[i])
            w_out = cd_w_out[i].astype(BF16)
        yb = y.reshape(t, -1).astype(BF16)
        h, hb = matmul_ln(yb, w_out, 0, h, mix_ln_g[layer][None, :], mix_ln_b[layer][None, :], alpha)

        kvb = matmul_stacked(memb, xa_wkv, layer, xa_wkv.shape[2], tm=memb.shape[0], tn=512, out_dtype=BF16)
        h, hb = xattn_ln(hb, h, xa_wq[layer].astype(BF16), kvb, xa_wo[layer].astype(BF16),
                         xa_ln_g[layer][None, :], xa_ln_b[layer][None, :], alpha, seq=seq)

        gate = moe_router(h, _pad_lanes(moe_router_w[layer]), _pad_lanes(moe_router_b[layer][None, :]))
        hact = moe_up_dense(hb, moe_w_gate, moe_b_gate, moe_w_up, moe_b_up, gate, layer)
        w_down = moe_w_down[layer].reshape(-1, d).astype(BF16)
        b_down = jnp.pad(moe_b_down[layer], ((0, LANES - N_EXPERTS), (0, 0)))
        h, hb = matmul_ln(hact, w_down, 0, h, moe_ln_g[layer][None, :], moe_ln_b[layer][None, :], alpha,
                          extra=gate, w_extra=b_down)
    return h.reshape(bsz, seq, d)
```

```python
import functools

import jax
import jax.numpy as jnp
from jax import lax
from jax.experimental import pallas as pl
from jax.experimental.pallas import tpu as pltpu

F32 = jnp.float32
BF16 = jnp.bfloat16

CHUNK = 64
HEAD_DIM = 128
GROUP_HEADS = 16
GROUP_WIDTH = GROUP_HEADS * HEAD_DIM
CONV_K = 4
PAST_CHUNKS = 8
REL_MAX = 2 * CHUNK
ROPE_BASE = 10000.0
XA_HEADS = 4
N_EXPERTS = 32
TOP_K = 4
SWIGLU_LIMIT = 7.0
SWIGLU_ALPHA = 1.702
LN_EPS = 1e-5
HN_EPS = 1e-6

LANES = 128
VMEM_LIMIT_BYTES = 58 * 1024 * 1024


def _params(*sem):
    return pltpu.CompilerParams(dimension_semantics=sem, vmem_limit_bytes=VMEM_LIMIT_BYTES)


def _mm_kernel(x_ref, w_ref, o_ref, wb_ref):
    @pl.when(pl.program_id(1) == 0)
    def _():
        wb_ref[...] = w_ref[...].astype(BF16)

    o_ref[...] = jnp.dot(x_ref[...], wb_ref[...], preferred_element_type=F32).astype(o_ref.dtype)


def matmul_stacked(x, w, layer, n_cols, *, tm, tn, out_dtype):
    m, k = x.shape
    return pl.pallas_call(
        _mm_kernel,
        grid=(n_cols // tn, m // tm),
        in_specs=[
            pl.BlockSpec((tm, k), lambda j, i: (i, 0)),
            pl.BlockSpec((None, k, tn), lambda j, i: (layer, 0, j)),
        ],
        out_specs=pl.BlockSpec((tm, tn), lambda j, i: (i, j)),
        out_shape=jax.ShapeDtypeStruct((m, n_cols), out_dtype),
        scratch_shapes=[pltpu.VMEM((k, tn), BF16)],
        compiler_params=_params("arbitrary", "arbitrary"),
        name="mm_in",
    )(x, w)


def _mm_small_kernel(x_ref, w_ref, b_ref, o_ref, *, precision):
    o_ref[...] = jnp.dot(x_ref[...], w_ref[...], preferred_element_type=F32,
                         precision=precision) + b_ref[...]


def matmul_small(x, w, b, *, tm, precision=None):
    m, k = x.shape
    n = w.shape[1]
    return pl.pallas_call(
        functools.partial(_mm_small_kernel, precision=precision),
        grid=(m // tm,),
        in_specs=[
            pl.BlockSpec((tm, k), lambda i: (i, 0)),
            pl.BlockSpec((k, n), lambda i: (0, 0)),
            pl.BlockSpec((1, n), lambda i: (0, 0)),
        ],
        out_specs=pl.BlockSpec((tm, n), lambda i: (i, 0)),
        out_shape=jax.ShapeDtypeStruct((m, n), F32),
        compiler_params=_params("arbitrary"),
        name="mm_small",
    )(x, w, b)


def _ln_rows(z_ref, g_ref, b_ref, of_ref, ob_ref, rows):
    tm = z_ref.shape[0]

    def body(r, carry):
        sl = pl.ds(pl.multiple_of(r * rows, rows), rows)
        z = z_ref[sl, :]
        mu = jnp.mean(z, axis=-1, keepdims=True)
        zc = z - mu
        var = jnp.mean(zc * zc, axis=-1, keepdims=True)
        y = zc * lax.rsqrt(var + LN_EPS) * g_ref[...] + b_ref[...]
        of_ref[sl, :] = y
        ob_ref[sl, :] = y.astype(BF16)
        return carry

    lax.fori_loop(0, tm // rows, body, 0)


def _mm_ln_kernel(*refs, nk, nj, tn, alpha, has_extra):
    if has_extra:
        x_ref, w_ref, h_ref, g_ref, b_ref, e_ref, we_ref, of_ref, ob_ref = refs
    else:
        x_ref, w_ref, h_ref, g_ref, b_ref, of_ref, ob_ref = refs
    k = pl.program_id(1)
    j = pl.program_id(2)
    part = jnp.dot(x_ref[...], w_ref[...].astype(BF16), preferred_element_type=F32)
    for jj in range(nj):
        sl = slice(jj * tn, (jj + 1) * tn)

        @pl.when((j == jj) & (k == 0))
        def _():
            of_ref[:, sl] = alpha * h_ref[...] + part

        @pl.when((j == jj) & (k > 0))
        def _():
            of_ref[:, sl] += part

    @pl.when((k == nk - 1) & (j == nj - 1))
    def _():
        if has_extra:
            of_ref[...] += jnp.dot(e_ref[...], we_ref[...], preferred_element_type=F32)
        _ln_rows(of_ref, g_ref, b_ref, of_ref, ob_ref, 32)


def matmul_ln(x, w, w_layer, h, g, b, alpha, *, extra=None, w_extra=None, tm=512, tn=512, tk=4096):
    m, kdim = x.shape
    n = h.shape[1]
    tk = min(tk, kdim)
    nk, nj = kdim // tk, n // tn
    if w.ndim == 3:
        w_spec = pl.BlockSpec((None, tk, tn), lambda i, k, j: (w_layer, k, j))
    else:
        w_spec = pl.BlockSpec((tk, tn), lambda i, k, j: (k, j))
    in_specs = [
        pl.BlockSpec((tm, tk), lambda i, k, j: (i, k)),
        w_spec,
        pl.BlockSpec((tm, tn), lambda i, k, j: (i, j)),
        pl.BlockSpec((1, n), lambda i, k, j: (0, 0)),
        pl.BlockSpec((1, n), lambda i, k, j: (0, 0)),
    ]
    args = [x, w, h, g, b]
    if extra is not None:
        in_specs += [
            pl.BlockSpec((tm, extra.shape[1]), lambda i, k, j: (i, 0)),
            pl.BlockSpec(w_extra.shape, lambda i, k, j: (0, 0)),
        ]
        args += [extra, w_extra]
    return pl.pallas_call(
        functools.partial(_mm_ln_kernel, nk=nk, nj=nj, tn=tn, alpha=alpha, has_extra=extra is not None),
        grid=(m // tm, nk, nj),
        in_specs=in_specs,
        out_specs=[
            pl.BlockSpec((tm, n), lambda i, k, j: (i, 0)),
            pl.BlockSpec((tm, n), lambda i, k, j: (i, 0)),
        ],
        out_shape=[jax.ShapeDtypeStruct((m, n), F32), jax.ShapeDtypeStruct((m, n), BF16)],
        compiler_params=_params("arbitrary", "arbitrary", "arbitrary"),
        name="mm_ln",
    )(*args)


def _xattn_kernel(hb_ref, h_ref, wq_ref, kv_ref, wo_ref, g_ref, b_ref, of_ref, ob_ref, z_ref, *, alpha, heads, hd):
    q = jnp.dot(hb_ref[...], wq_ref[...], preferred_element_type=F32)
    scale = hd ** -0.5
    outs = []
    for hh in range(heads):
        qh = (q[:, hh * hd:(hh + 1) * hd] * scale).astype(BF16)
        kh = kv_ref[:, hh * hd:(hh + 1) * hd]
        vh = kv_ref[:, (heads + hh) * hd:(heads + hh + 1) * hd]
        s = lax.dot_general(qh, kh, (((1,), (1,)), ((), ())), preferred_element_type=F32)
        s = s - jnp.max(s, axis=-1, keepdims=True)
        p = jnp.exp(s)
        l = jnp.sum(p, axis=-1, keepdims=True)
        o = jnp.dot(p.astype(BF16), vh, preferred_element_type=F32) / l
        outs.append(o.astype(BF16))
    o_all = jnp.concatenate(outs, axis=-1)
    z_ref[...] = alpha * h_ref[...] + jnp.dot(o_all, wo_ref[...], preferred_element_type=F32)
    _ln_rows(z_ref, g_ref, b_ref, of_ref, ob_ref, 32)


def xattn_ln(hb, h, wq_b, kv_b, wo_b, g, b, alpha, *, seq, tm=256):
    m, d = h.shape
    xw = wq_b.shape[1]
    n_mem = kv_b.shape[0] // (m // seq)
    per_b = seq // tm
    return pl.pallas_call(
        functools.partial(_xattn_kernel, alpha=alpha, heads=XA_HEADS, hd=xw // XA_HEADS),
        grid=(m // tm,),
        in_specs=[
            pl.BlockSpec((tm, d), lambda i: (i, 0)),
            pl.BlockSpec((tm, d), lambda i: (i, 0)),
            pl.BlockSpec((d, xw), lambda i: (0, 0)),
            pl.BlockSpec((n_mem, 2 * xw), lambda i: (i // per_b, 0)),
            pl.BlockSpec((xw, d), lambda i: (0, 0)),
            pl.BlockSpec((1, d), lambda i: (0, 0)),
            pl.BlockSpec((1, d), lambda i: (0, 0)),
        ],
        out_specs=[pl.BlockSpec((tm, d), lambda i: (i, 0)), pl.BlockSpec((tm, d), lambda i: (i, 0))],
        out_shape=[jax.ShapeDtypeStruct((m, d), F32), jax.ShapeDtypeStruct((m, d), BF16)],
        scratch_shapes=[pltpu.VMEM((tm, d), F32)],
        compiler_params=_params("arbitrary"),
        name="xattn_ln",
    )(hb, h, wq_b, kv_b, wo_b, g, b)


def _router_kernel(h_ref, w_ref, b_ref, gate_ref):
    logits = jnp.dot(h_ref[...], w_ref[...], preferred_element_type=F32,
                     precision=lax.Precision.HIGHEST) + b_ref[...]
    lane = lax.broadcasted_iota(jnp.int32, logits.shape, 1)
    neg = jnp.float32(-jnp.inf)
    masked = jnp.where(lane < N_EXPERTS, logits, neg)
    top_vals, top_hot = [], []
    for _ in range(TOP_K):
        mval = jnp.max(masked, axis=-1, keepdims=True)
        idx = jnp.min(jnp.where(masked == mval, lane, LANES), axis=-1, keepdims=True)
        hot = lane == idx
        top_vals.append(mval)
        top_hot.append(hot)
        masked = jnp.where(hot, neg, masked)
    exps = [jnp.exp(v - top_vals[0]) for v in top_vals]
    denom = exps[0]
    for e in exps[1:]:
        denom = denom + e
    gate = jnp.zeros_like(logits)
    for e, hot in zip(exps, top_hot):
        gate = jnp.where(hot, e / denom, gate)
    gate_ref[...] = gate


def moe_router(h, w_pad, b_pad, *, tm=512):
    m, d = h.shape
    return pl.pallas_call(
        _router_kernel,
        grid=(m // tm,),
        in_specs=[
            pl.BlockSpec((tm, d), lambda i: (i, 0)),
            pl.BlockSpec((d, LANES), lambda i: (0, 0)),
            pl.BlockSpec((1, LANES), lambda i: (0, 0)),
        ],
        out_specs=pl.BlockSpec((tm, LANES), lambda i: (i, 0)),
        out_shape=jax.ShapeDtypeStruct((m, LANES), F32),
        compiler_params=_params("arbitrary"),
        name="moe_router",
    )(h, w_pad, b_pad)


def _moe_up_kernel(x_ref, wg_ref, wu_ref, bg_ref, bu_ref, gate_ref, o_ref, wb_ref, *, ff):
    e = pl.program_id(0)

    @pl.when(pl.program_id(1) == 0)
    def _():
        wb_ref[:, :ff] = wg_ref[...].astype(BF16)
        wb_ref[:, ff:] = wu_ref[...].astype(BF16)

    gu = jnp.dot(x_ref[...], wb_ref[...], preferred_element_type=F32)
    g = jnp.minimum(gu[:, :ff] + bg_ref[...], SWIGLU_LIMIT)
    u = jnp.clip(gu[:, ff:] + bu_ref[...], -SWIGLU_LIMIT, SWIGLU_LIMIT)
    act = g * jax.nn.sigmoid(SWIGLU_ALPHA * g) * (u + 1.0)
    gate = gate_ref[...]
    lane = lax.broadcasted_iota(jnp.int32, gate.shape, 1)
    gcol = jnp.sum(jnp.where(lane == e, gate, 0.0), axis=-1, keepdims=True)
    o_ref[...] = (act * gcol).astype(o_ref.dtype)


def moe_up_dense(xb, w_gate, b_gate, w_up, b_up, gate, layer, *, tm=1024):
    m, d = xb.shape
    n_e, ff = w_gate.shape[1], w_gate.shape[3]
    bg = b_gate.reshape(b_gate.shape[0], n_e, 1, ff)
    bu = b_up.reshape(b_up.shape[0], n_e, 1, ff)
    w_spec = pl.BlockSpec((None, None, d, ff), lambda e, i: (layer, e, 0, 0))
    b_spec = pl.BlockSpec((None, None, 1, ff), lambda e, i: (layer, e, 0, 0))
    return pl.pallas_call(
        functools.partial(_moe_up_kernel, ff=ff),
        grid=(n_e, m // tm),
        in_specs=[
            pl.BlockSpec((tm, d), lambda e, i: (i, 0)),
            w_spec, w_spec, b_spec, b_spec,
            pl.BlockSpec((tm, LANES), lambda e, i: (i, 0)),
        ],
        out_specs=pl.BlockSpec((tm, ff), lambda e, i: (i, e)),
        out_shape=jax.ShapeDtypeStruct((m, n_e * ff), BF16),
        scratch_shapes=[pltpu.VMEM((d, 2 * ff), BF16)],
        compiler_params=_params("arbitrary", "arbitrary"),
        name="moe_up",
    )(xb, w_gate, w_up, bg, bu, gate)


NEG = -1e30
MIX_CHUNK = 256
BAND_BLOCK = 256
SB_TQ = 512
SB_TK = 256

_NT = (((1,), (1,)), ((), ()))
_TN = (((0,), (0,)), ((), ()))


def _head_norm_rows(x, gain):
    mu = jnp.mean(x, axis=-1, keepdims=True)
    xc = x - mu
    var = jnp.mean(xc * xc, axis=-1, keepdims=True)
    return xc * lax.rsqrt(var + HN_EPS) * gain


def _log_sigmoid(x):
    return jnp.minimum(x, 0.0) - jnp.log(1.0 + jnp.exp(-jnp.abs(x)))


def _split_bf16(x):
    hi = x.astype(BF16)
    lo = (x - hi.astype(F32)).astype(BF16)
    return hi, lo


def _lane_select(x, lane_idx):
    lane = lax.broadcasted_iota(jnp.int32, x.shape, 1)
    col = jnp.sum(jnp.where(lane == lane_idx, x, 0.0), axis=-1, keepdims=True)
    return jnp.broadcast_to(col, x.shape)


def _retention_body(q_ref, k_ref, v_ref, g_ref, cos_ref, sin_ref, lg_ref, gain_ref, o_ref,
                    qs_ref, ks_ref, st_ref, *, scale):
    seq = q_ref.shape[0]
    L = MIX_CHUNK
    half = HEAD_DIM // 2
    cos = cos_ref[...]
    sin = sin_ref[...]
    q = q_ref[...].astype(F32)
    k = k_ref[...].astype(F32)
    qs_ref[...] = (q * cos + pltpu.roll(q, half, 1) * sin).astype(BF16)
    ks_ref[...] = ((k * cos + pltpu.roll(k, half, 1) * sin) * scale).astype(BF16)

    lg = lg_ref[...]
    ri = lax.broadcasted_iota(jnp.int32, (L, L), 0)
    ci = lax.broadcasted_iota(jnp.int32, (L, L), 1)
    intra = jnp.where(ri >= ci, jnp.exp(lg * jnp.maximum(ri - ci, 0).astype(F32)), 0.0)
    rr = lax.broadcasted_iota(jnp.int32, (L, HEAD_DIM), 0).astype(F32)
    lg_d = lg[:, :HEAD_DIM]
    q_dec = jnp.exp(lg_d * (rr + 1.0))
    k_dec = jnp.exp(lg_d * (L - 1.0 - rr))
    c_dec = jnp.exp(lg_d * float(L))
    gain = gain_ref[...]
    st_ref[...] = jnp.zeros_like(st_ref)

    def chunk(c, carry):
        sl = pl.ds(pl.multiple_of(c * L, L), L)
        qc = qs_ref[sl, :]
        kc = ks_ref[sl, :]
        vc = v_ref[sl, :]
        state = st_ref[:, :HEAD_DIM]
        att = lax.dot_general(qc, kc, _NT, preferred_element_type=F32) * intra
        o = (jnp.dot(att.astype(BF16), vc, preferred_element_type=F32)
             + jnp.dot((qc.astype(F32) * q_dec).astype(BF16), state.astype(BF16), preferred_element_type=F32))
        st_ref[:, :HEAD_DIM] = state * c_dec + lax.dot_general(
            (kc.astype(F32) * k_dec).astype(BF16), vc, _TN, preferred_element_type=F32)
        gv = g_ref[sl, :].astype(F32)
        o_ref[sl, :] = (_head_norm_rows(o, gain) * (gv * jax.nn.sigmoid(gv))).astype(o_ref.dtype)
        return carry

    lax.fori_loop(0, seq // L, chunk, 0)


def _mlstm_body(u_ref, v_ref, og_ref, gates_ref, cw_ref, cb_ref, wq_ref, wk_ref, gain_ref, o_ref,
                qs_ref, ks_ref, st_ref, m_ref, *, head, heads, scale):
    seq = u_ref.shape[0]
    L = MIX_CHUNK
    d = HEAD_DIM
    x = u_ref[...].astype(F32)
    row = lax.broadcasted_iota(jnp.int32, x.shape, 0)
    cw = cw_ref[...]
    y = x * cw[CONV_K - 1:CONV_K, :] + cb_ref[...]
    for sh in range(1, CONV_K):
        xs = jnp.where(row >= sh, pltpu.roll(x, sh, 0), 0.0)
        y = y + xs * cw[CONV_K - 1 - sh:CONV_K - sh, :]
    ub = (y * jax.nn.sigmoid(y)).astype(BF16)
    qs_ref[...] = jnp.dot(ub, wq_ref[...].astype(BF16), preferred_element_type=F32).astype(BF16)
    ks_ref[...] = (jnp.dot(ub, wk_ref[...].astype(BF16), preferred_element_type=F32) * scale).astype(BF16)

    ri = lax.broadcasted_iota(jnp.int32, (L, L), 0)
    ci = lax.broadcasted_iota(jnp.int32, (L, L), 1)
    causal = ri >= ci
    tri = jnp.where(causal, 1.0, 0.0).astype(BF16)
    ones_v = jnp.ones((L, d), BF16)
    gain = gain_ref[...]
    st_ref[...] = jnp.zeros_like(st_ref)
    m_ref[...] = jnp.zeros_like(m_ref)

    def chunk(c, carry):
        sl = pl.ds(pl.multiple_of(c * L, L), L)
        qc = qs_ref[sl, :]
        kc = ks_ref[sl, :]
        v_ext = jnp.concatenate([v_ref[sl, :], ones_v], axis=1)
        gts = gates_ref[sl, :]
        ic = _lane_select(gts, head)
        lf = _log_sigmoid(_lane_select(gts, heads + head))
        lf_hi, lf_lo = _split_bf16(lf)
        bcum = (jnp.dot(tri, lf_hi, preferred_element_type=F32)
                + jnp.dot(tri, lf_lo, preferred_element_type=F32))
        m_st = m_ref[...]
        src = jnp.transpose(ic - bcum)[:1, :]
        bcum2 = jnp.concatenate([bcum, bcum], axis=1)
        log_intra = jnp.where(causal, bcum2 + src, NEG)
        m_intra = jnp.max(log_intra, axis=-1, keepdims=True)
        log_cross = bcum + m_st
        m_row = jnp.maximum(log_cross, m_intra)
        m_row2 = jnp.concatenate([m_row, m_row], axis=1)
        w_intra = jnp.exp(log_intra - m_row2)
        w_cross = jnp.exp(log_cross - m_row)
        w_cross2 = jnp.concatenate([w_cross, w_cross], axis=1)
        qk = lax.dot_general(qc, kc, _NT, preferred_element_type=F32) * w_intra
        state = st_ref[...]
        res = (jnp.dot(qk.astype(BF16), v_ext, preferred_element_type=F32)
               + w_cross2 * jnp.dot(qc, state.astype(BF16), preferred_element_type=F32))
        num = res[:, :d]
        den = res[:, d:]
        hh = num / jnp.maximum(jnp.abs(den), jnp.exp(-m_row))
        og = og_ref[sl, :].astype(F32)
        o_ref[sl, :] = _head_norm_rows(hh * jax.nn.sigmoid(og), gain).astype(o_ref.dtype)
        b_last = bcum[L - 1:L, :]
        log_state = b_last - bcum + ic
        m_new = jnp.maximum(b_last + m_st, jnp.max(log_state, axis=0, keepdims=True))
        decay = jnp.exp(b_last + m_st - m_new)
        kw = (kc.astype(F32) * jnp.exp(log_state - m_new)).astype(BF16)
        decay2 = jnp.concatenate([decay, decay], axis=1)
        st_ref[...] = decay2 * state + lax.dot_general(kw, v_ext, _TN, preferred_element_type=F32)
        m_ref[...] = m_new
        return carry

    lax.fori_loop(0, seq // L, chunk, 0)


def _mixer_ab_kernel(a0_ref, a1_ref, a2_ref, a3_ref, gates_ref, cos_ref, sin_ref, lg_ref, cw_ref, cb_ref,
                     wq_ref, wk_ref, gain_ref, o_ref, qs_ref, ks_ref, st_ref, m_ref, *, heads, scale):
    g = pl.program_id(1)

    @pl.when(g < heads)
    def _():
        _retention_body(a0_ref, a1_ref, a2_ref, a3_ref, cos_ref, sin_ref, lg_ref, gain_ref, o_ref,
                        qs_ref, ks_ref, st_ref, scale=scale)

    @pl.when(g >= heads)
    def _():
        _mlstm_body(a0_ref, a1_ref, a2_ref, gates_ref, cw_ref, cb_ref, wq_ref, wk_ref, gain_ref, o_ref,
                    qs_ref, ks_ref, st_ref, m_ref, head=g - heads, heads=heads, scale=scale)


def mixer_ab(z, gates, cos2, sin2, log_g, conv_w, conv_b, wq_m, wk_m, gain, layer, *, bsz, seq, heads):
    t = z.shape[0]
    d = HEAD_DIM
    L = MIX_CHUNK

    def col(base_ret, base_ml):
        def index(b, g):
            is_ml = g // heads
            return (b, (1 - is_ml) * (base_ret * heads + g) + is_ml * (base_ml * heads + g - heads))
        return pl.BlockSpec((seq, d), index)

    def ml_head(g):
        return jnp.maximum(g - heads, 0)

    conv_w4 = conv_w.reshape(conv_w.shape[0], CONV_K, heads, d).transpose(0, 2, 1, 3)
    conv_b4 = conv_b.reshape(conv_b.shape[0], heads, 1, d)
    gain4 = gain.reshape(2 * heads, 1, d)
    return pl.pallas_call(
        functools.partial(_mixer_ab_kernel, heads=heads, scale=d ** -0.5),
        grid=(bsz, 2 * heads),
        in_specs=[
            col(0, 4), col(1, 5), col(2, 6), col(3, 6),
            pl.BlockSpec((seq, LANES), lambda b, g: (b, 0)),
            pl.BlockSpec((seq, d), lambda b, g: (0, 0)),
            pl.BlockSpec((seq, d), lambda b, g: (0, 0)),
            pl.BlockSpec((None, 1, L), lambda b, g: (jnp.minimum(g, heads - 1), 0, 0)),
            pl.BlockSpec((None, None, CONV_K, d), lambda b, g: (layer, ml_head(g), 0, 0)),
            pl.BlockSpec((None, None, 1, d), lambda b, g: (layer, ml_head(g), 0, 0)),
            pl.BlockSpec((None, None, d, d), lambda b, g: (layer, ml_head(g), 0, 0)),
            pl.BlockSpec((None, None, d, d), lambda b, g: (layer, ml_head(g), 0, 0)),
            pl.BlockSpec((None, 1, d), lambda b, g: (g, 0, 0)),
        ],
        out_specs=pl.BlockSpec((seq, d), lambda b, g: (b, g)),
        out_shape=jax.ShapeDtypeStruct((t, 2 * heads * d), BF16),
        scratch_shapes=[
            pltpu.VMEM((seq, d), BF16),
            pltpu.VMEM((seq, d), BF16),
            pltpu.VMEM((d, 2 * d), F32),
            pltpu.VMEM((1, d), F32),
        ],
        compiler_params=_params("arbitrary", "arbitrary"),
        name="mixer_ab",
    )(z, z, z, z, gates, cos2, sin2, log_g, conv_w4, conv_b4, wq_m, wk_m, gain4)


def _band_body(q_ref, k_ref, v_ref, bias_ref, o_ref, *, scale):
    seq = q_ref.shape[0]
    bq = BAND_BLOCK
    n_back = bias_ref.shape[0]

    def block(i, carry):
        sl = pl.ds(pl.multiple_of(i * bq, bq), bq)
        q = q_ref[sl, :]
        scores, vals = [], []
        for dlt in range(n_back):
            ks = pl.ds(pl.multiple_of(jnp.maximum(i - dlt, 0) * bq, bq), bq)
            s = lax.dot_general(q, k_ref[ks, :], _NT, preferred_element_type=F32) * scale + bias_ref[dlt]
            scores.append(jnp.where(i - dlt >= 0, s, NEG))
            vals.append(v_ref[ks, :])
        m = jnp.max(scores[0], axis=-1, keepdims=True)
        for s in scores[1:]:
            m = jnp.maximum(m, jnp.max(s, axis=-1, keepdims=True))
        acc = jnp.zeros((bq, HEAD_DIM), F32)
        l = jnp.zeros((bq, 1), F32)
        for s, vv in zip(scores, vals):
            p = jnp.exp(s - m)
            l = l + jnp.sum(p, axis=-1, keepdims=True)
            acc = acc + jnp.dot(p.astype(BF16), vv, preferred_element_type=F32)
        o_ref[sl, :] = (acc / l).astype(o_ref.dtype)
        return carry

    lax.fori_loop(0, seq // bq, block, 0)


def _stick_breaking_body(q_ref, k_ref, v_ref, o_ref, acc_ref, r_ref, *, scale):
    seq = q_ref.shape[0]
    tq, tk = SB_TQ, SB_TK
    per_q = tq // tk
    ri = lax.broadcasted_iota(jnp.int32, (tk, tk), 0)
    ci = lax.broadcasted_iota(jnp.int32, (tk, tk), 1)
    suffix = jnp.where(ri > ci, 1.0, 0.0).astype(BF16)
    qrow = lax.broadcasted_iota(jnp.int32, (tq, tk), 0)
    kcol = lax.broadcasted_iota(jnp.int32, (tq, tk), 1)

    def one_pair(q, q0, j, masked):
        k0 = pl.multiple_of(j * tk, tk)
        kb = k_ref[pl.ds(k0, tk), :]
        vb = v_ref[pl.ds(k0, tk), :]
        z = lax.dot_general(q, kb, _NT, preferred_element_type=F32) * scale
        sp = jnp.maximum(z, 0.0) + jnp.log(1.0 + jnp.exp(-jnp.abs(z)))
        ls_pos = z - sp
        lr = -sp
        if masked:
            strict = (k0 + kcol) < (q0 + qrow)
            lr = jnp.where(strict, lr, 0.0)
        hi, lo = _split_bf16(lr)
        between = (jnp.dot(hi, suffix, preferred_element_type=F32)
                   + jnp.dot(lo, suffix, preferred_element_type=F32))
        r = r_ref[...]
        log_a = ls_pos + between + jnp.concatenate([r] * (tk // LANES), axis=1)
        p = jnp.exp(log_a)
        if masked:
            p = jnp.where(strict, p, 0.0)
        acc_ref[...] += jnp.dot(p.astype(BF16), vb, preferred_element_type=F32)
        r_ref[...] = r + jnp.sum(lr, axis=-1, keepdims=True)

    def q_block(i, carry):
        q0 = pl.multiple_of(i * tq, tq)
        q = q_ref[pl.ds(q0, tq), :]
        acc_ref[...] = jnp.zeros_like(acc_ref)
        r_ref[...] = jnp.zeros_like(r_ref)
        last = (i + 1) * per_q - 1
        for dd in range(per_q):
            one_pair(q, q0, last - dd, True)

        def below(jj, c):
            one_pair(q, q0, i * per_q - 1 - jj, False)
            return c

        lax.fori_loop(0, i * per_q, below, 0)
        o_ref[pl.ds(q0, tq), :] = acc_ref[...].astype(o_ref.dtype)
        return carry

    lax.fori_loop(0, seq // tq, q_block, 0)


def _mixer_cd_kernel(q_ref, k_ref, v_ref, bias_ref, o_ref, acc_ref, r_ref, *, heads, scale):
    g = pl.program_id(1)

    @pl.when(g < heads)
    def _():
        _band_body(q_ref, k_ref, v_ref, bias_ref, o_ref, scale=scale)

    @pl.when(g >= heads)
    def _():
        _stick_breaking_body(q_ref, k_ref, v_ref, o_ref, acc_ref, r_ref, scale=scale)


def band_bias_table(rel_bias):
    bq = BAND_BLOCK
    n_back = PAST_CHUNKS * CHUNK // bq + 1
    qp = jnp.arange(bq)[None, :, None] + bq * jnp.arange(n_back)[:, None, None]
    kp = jnp.arange(bq)[None, None, :]
    allowed = (kp // CHUNK <= qp // CHUNK) & (kp // CHUNK >= qp // CHUNK - PAST_CHUNKS)
    idx = jnp.clip(qp - kp, -(CHUNK - 1), REL_MAX) + (CHUNK - 1)
    return jnp.where(allowed[None], rel_bias.astype(F32)[:, idx], NEG)


def mixer_cd(z, bias_tab, *, bsz, seq, heads):
    t = z.shape[0]
    d = HEAD_DIM

    def col(which):
        return pl.BlockSpec((seq, d), lambda b, g: (b, (3 * (g // heads) + which) * heads + g % heads))

    return pl.pallas_call(
        functools.partial(_mixer_cd_kernel, heads=heads, scale=d ** -0.5),
        grid=(bsz, 2 * heads),
        in_specs=[
            col(0), col(1), col(2),
            pl.BlockSpec((None,) + bias_tab.shape[1:], lambda b, g: (jnp.minimum(g, heads - 1), 0, 0, 0)),
        ],
        out_specs=pl.BlockSpec((seq, d), lambda b, g: (b, g)),
        out_shape=jax.ShapeDtypeStruct((t, 2 * heads * d), BF16),
        scratch_shapes=[pltpu.VMEM((SB_TQ, d), F32), pltpu.VMEM((SB_TQ, LANES), F32)],
        compiler_params=_params("arbitrary", "arbitrary"),
        name="mixer_cd",
    )(z, z, z, bias_tab)


def rope_tables(seq_len):
    pos = jnp.arange(seq_len, dtype=F32)
    inv_freq = ROPE_BASE ** (-jnp.arange(0, HEAD_DIM, 2, dtype=F32) / HEAD_DIM)
    ang = pos[:, None] * inv_freq[None, :]
    cos, sin = jnp.cos(ang), jnp.sin(ang)
    return jnp.concatenate([cos, cos], axis=1), jnp.concatenate([-sin, sin], axis=1)


def retention_log_decay(heads):
    lg = jnp.log1p(-jnp.exp2(-(5.0 + jnp.arange(heads, dtype=F32))))
    return jnp.broadcast_to(lg[:, None, None], (heads, 1, MIX_CHUNK))


def _pad_lanes(a, value=0.0):
    return jnp.pad(a, ((0, 0), (0, LANES - a.shape[1])), constant_values=value)


def kernel(x, mem, ab_w_in, ab_gate_b, ab_conv_w, ab_conv_b, ab_wq, ab_wk, ab_ret_norm_g, ab_mlstm_norm_g, ab_w_out, cd_w_in, cd_rel_bias, cd_w_out, mix_ln_g, mix_ln_b, xa_wq, xa_wkv, xa_wo, xa_ln_g, xa_ln_b, moe_router_w, moe_router_b, moe_w_gate, moe_b_gate, moe_w_up, moe_b_up, moe_w_down, moe_b_down, moe_ln_g, moe_ln_b):
    bsz, seq, d = x.shape
    depth = mix_ln_g.shape[0]
    heads = GROUP_HEADS
    alpha = (2.0 * depth) ** 0.25
    t = bsz * seq
    cos2, sin2 = rope_tables(seq)
    log_g = retention_log_decay(heads)
    h = x.reshape(t, d)
    hb = h.astype(BF16)
    memb = mem.reshape(-1, d).astype(BF16)
    n_ab = 7 * GROUP_WIDTH
    n_cd = 6 * GROUP_WIDTH
    for layer in range(depth):
        i = layer // 2
        if layer % 2 == 0:
            z = matmul_stacked(hb, ab_w_in, i, n_ab, tm=1024, tn=512, out_dtype=BF16)
            wg = _pad_lanes(ab_w_in[i, :, n_ab:]).astype(BF16)
            gates = matmul_small(hb, wg, _pad_lanes(ab_gate_b[i][None, :]), tm=1024)
            gain = jnp.concatenate([ab_ret_norm_g[i], ab_mlstm_norm_g[i]])
            yb = mixer_ab(z, gates, cos2, sin2, log_g, ab_conv_w, ab_conv_b, ab_wq, ab_wk, gain, i,
                          bsz=bsz, seq=seq, heads=heads)
            w_out = ab_w_out[i].astype(BF16)
        else:
            z = matmul_stacked(hb, cd_w_in, i, n_cd, tm=1024, tn=512, out_dtype=BF16)
            yb = mixer_cd(z, band_bias_table(cd_rel_bias[i]), bsz=bsz, seq=seq, heads=heads)
            w_out = cd_w_out[i].astype(BF16)
        h, hb = matmul_ln(yb, w_out, 0, h, mix_ln_g[layer][None, :], mix_ln_b[layer][None, :], alpha)

        kvb = matmul_stacked(memb, xa_wkv, layer, xa_wkv.shape[2], tm=memb.shape[0], tn=512, out_dtype=BF16)
        h, hb = xattn_ln(hb, h, xa_wq[layer].astype(BF16), kvb, xa_wo[layer].astype(BF16),
                         xa_ln_g[layer][None, :], xa_ln_b[layer][None, :], alpha, seq=seq)

        gate = moe_router(h, _pad_lanes(moe_router_w[layer]), _pad_lanes(moe_router_b[layer][None, :]))
        hact = moe_up_dense(hb, moe_w_gate, moe_b_gate, moe_w_up, moe_b_up, gate, layer)
        w_down = moe_w_down[layer].reshape(-1, d).astype(BF16)
        b_down = jnp.pad(moe_b_down[layer], ((0, LANES - N_EXPERTS), (0, 0)))
        h, hb = matmul_ln(hact, w_down, 0, h, moe_ln_g[layer][None, :], moe_ln_b[layer][None, :], alpha,
                          extra=gate, w_extra=b_down)
    return h.reshape(bsz, seq, d)
```

```python
import functools

import jax
import jax.numpy as jnp
from jax import lax
from jax.experimental import pallas as pl
from jax.experimental.pallas import tpu as pltpu

F32 = jnp.float32
BF16 = jnp.bfloat16

CHUNK = 64
HEAD_DIM = 128
GROUP_HEADS = 16
GROUP_WIDTH = GROUP_HEADS * HEAD_DIM
CONV_K = 4
PAST_CHUNKS = 8
REL_MAX = 2 * CHUNK
ROPE_BASE = 10000.0
XA_HEADS = 4
N_EXPERTS = 32
TOP_K = 4
SWIGLU_LIMIT = 7.0
SWIGLU_ALPHA = 1.702
LN_EPS = 1e-5
HN_EPS = 1e-6

LANES = 128
VMEM_LIMIT_BYTES = 58 * 1024 * 1024


def _params(*sem):
    return pltpu.CompilerParams(dimension_semantics=sem, vmem_limit_bytes=VMEM_LIMIT_BYTES)


def _mm_kernel(x_ref, w_ref, o_ref, wb_ref):
    @pl.when(pl.program_id(1) == 0)
    def _():
        wb_ref[...] = w_ref[...].astype(BF16)

    o_ref[...] = jnp.dot(x_ref[...], wb_ref[...], preferred_element_type=F32).astype(o_ref.dtype)


def matmul_stacked(x, w, layer, n_cols, *, tm, tn, out_dtype):
    m, k = x.shape
    return pl.pallas_call(
        _mm_kernel,
        grid=(n_cols // tn, m // tm),
        in_specs=[
            pl.BlockSpec((tm, k), lambda j, i: (i, 0)),
            pl.BlockSpec((None, k, tn), lambda j, i: (layer, 0, j)),
        ],
        out_specs=pl.BlockSpec((tm, tn), lambda j, i: (i, j)),
        out_shape=jax.ShapeDtypeStruct((m, n_cols), out_dtype),
        scratch_shapes=[pltpu.VMEM((k, tn), BF16)],
        compiler_params=_params("arbitrary", "arbitrary"),
        name="mm_in",
    )(x, w)


_NT = (((1,), (1,)), ((), ()))
_TN = (((0,), (0,)), ((), ()))


def _mm_nt_kernel(x_ref, w_ref, o_ref, wb_ref):
    @pl.when(pl.program_id(1) == 0)
    def _():
        wb_ref[...] = w_ref[...].astype(BF16)

    o_ref[...] = lax.dot_general(x_ref[...], wb_ref[...], _NT, preferred_element_type=F32).astype(o_ref.dtype)


def matmul_nt_stacked(x, wt, layer, n_cols, *, tm, tn, out_dtype):
    m, k = x.shape
    return pl.pallas_call(
        _mm_nt_kernel,
        grid=(n_cols // tn, m // tm),
        in_specs=[
            pl.BlockSpec((tm, k), lambda j, i: (i, 0)),
            pl.BlockSpec((None, tn, k), lambda j, i: (layer, j, 0)),
        ],
        out_specs=pl.BlockSpec((tm, tn), lambda j, i: (i, j)),
        out_shape=jax.ShapeDtypeStruct((m, n_cols), out_dtype),
        scratch_shapes=[pltpu.VMEM((tn, k), BF16)],
        compiler_params=_params("arbitrary", "arbitrary"),
        name="mm_in_nt",
    )(x, wt)


def _gates_kernel(x_ref, w_ref, b_ref, o_ref, wb_ref):
    @pl.when(pl.program_id(0) == 0)
    def _():
        wb_ref[...] = jnp.zeros_like(wb_ref)
        wb_ref[:w_ref.shape[0], :] = w_ref[...].astype(BF16)

    o_ref[...] = lax.dot_general(x_ref[...], wb_ref[...], _NT, preferred_element_type=F32) + b_ref[...]


def gates_nt(x, wt, layer, row0, n_rows, b_pad, *, tm):
    m, k = x.shape
    return pl.pallas_call(
        _gates_kernel,
        grid=(m // tm,),
        in_specs=[
            pl.BlockSpec((tm, k), lambda i: (i, 0)),
            pl.BlockSpec((None, n_rows, k), lambda i: (layer, row0 // n_rows, 0)),
            pl.BlockSpec((1, LANES), lambda i: (0, 0)),
        ],
        out_specs=pl.BlockSpec((tm, LANES), lambda i: (i, 0)),
        out_shape=jax.ShapeDtypeStruct((m, LANES), F32),
        scratch_shapes=[pltpu.VMEM((LANES, k), BF16)],
        compiler_params=_params("arbitrary"),
        name="gates",
    )(x, wt, b_pad)


def _ln_rows(z_ref, g_ref, b_ref, of_ref, ob_ref, rows):
    tm = z_ref.shape[0]

    def body(r, carry):
        sl = pl.ds(pl.multiple_of(r * rows, rows), rows)
        z = z_ref[sl, :]
        mu = jnp.mean(z, axis=-1, keepdims=True)
        zc = z - mu
        var = jnp.mean(zc * zc, axis=-1, keepdims=True)
        y = zc * lax.rsqrt(var + LN_EPS) * g_ref[...] + b_ref[...]
        of_ref[sl, :] = y
        ob_ref[sl, :] = y.astype(BF16)
        return carry

    lax.fori_loop(0, tm // rows, body, 0)


def _mm_ln_kernel(x_ref, w_ref, h_ref, g_ref, b_ref, of_ref, ob_ref, *, nk, nj, tn, alpha):
    k = pl.program_id(1)
    j = pl.program_id(2)
    part = jnp.dot(x_ref[...], w_ref[...].astype(BF16), preferred_element_type=F32)
    for jj in range(nj):
        sl = slice(jj * tn, (jj + 1) * tn)

        @pl.when((j == jj) & (k == 0))
        def _():
            of_ref[:, sl] = alpha * h_ref[...] + part

        @pl.when((j == jj) & (k > 0))
        def _():
            of_ref[:, sl] += part

    @pl.when((k == nk - 1) & (j == nj - 1))
    def _():
        _ln_rows(of_ref, g_ref, b_ref, of_ref, ob_ref, 32)


def matmul_ln(x, w, h, g, b, alpha, *, tm=512, tn=512, tk=4096):
    m, kdim = x.shape
    n = h.shape[1]
    tk = min(tk, kdim)
    nk, nj = kdim // tk, n // tn
    in_specs = [
        pl.BlockSpec((tm, tk), lambda i, k, j: (i, k)),
        pl.BlockSpec((tk, tn), lambda i, k, j: (k, j)),
        pl.BlockSpec((tm, tn), lambda i, k, j: (i, j)),
        pl.BlockSpec((1, n), lambda i, k, j: (0, 0)),
        pl.BlockSpec((1, n), lambda i, k, j: (0, 0)),
    ]
    args = [x, w, h, g, b]
    return pl.pallas_call(
        functools.partial(_mm_ln_kernel, nk=nk, nj=nj, tn=tn, alpha=alpha),
        grid=(m // tm, nk, nj),
        in_specs=in_specs,
        out_specs=[
            pl.BlockSpec((tm, n), lambda i, k, j: (i, 0)),
            pl.BlockSpec((tm, n), lambda i, k, j: (i, 0)),
        ],
        out_shape=[jax.ShapeDtypeStruct((m, n), F32), jax.ShapeDtypeStruct((m, n), BF16)],
        compiler_params=_params("arbitrary", "arbitrary", "arbitrary"),
        name="mm_ln",
    )(*args)


def _xattn_kernel(hb_ref, h_ref, wq_ref, kv_ref, wo_ref, g_ref, b_ref, of_ref, ob_ref, z_ref, *, alpha, heads, hd):
    q = jnp.dot(hb_ref[...], wq_ref[...], preferred_element_type=F32)
    scale = hd ** -0.5
    outs = []
    for hh in range(heads):
        qh = (q[:, hh * hd:(hh + 1) * hd] * scale).astype(BF16)
        kh = kv_ref[:, hh * hd:(hh + 1) * hd]
        vh = kv_ref[:, (heads + hh) * hd:(heads + hh + 1) * hd]
        s = lax.dot_general(qh, kh, (((1,), (1,)), ((), ())), preferred_element_type=F32)
        s = s - jnp.max(s, axis=-1, keepdims=True)
        p = jnp.exp(s)
        l = jnp.sum(p, axis=-1, keepdims=True)
        o = jnp.dot(p.astype(BF16), vh, preferred_element_type=F32) / l
        outs.append(o.astype(BF16))
    o_all = jnp.concatenate(outs, axis=-1)
    z_ref[...] = alpha * h_ref[...] + jnp.dot(o_all, wo_ref[...], preferred_element_type=F32)
    _ln_rows(z_ref, g_ref, b_ref, of_ref, ob_ref, 32)


def xattn_ln(hb, h, wq_b, kv_b, wo_b, g, b, alpha, *, seq, tm=256):
    m, d = h.shape
    xw = wq_b.shape[1]
    n_mem = kv_b.shape[0] // (m // seq)
    per_b = seq // tm
    return pl.pallas_call(
        functools.partial(_xattn_kernel, alpha=alpha, heads=XA_HEADS, hd=xw // XA_HEADS),
        grid=(m // tm,),
        in_specs=[
            pl.BlockSpec((tm, d), lambda i: (i, 0)),
            pl.BlockSpec((tm, d), lambda i: (i, 0)),
            pl.BlockSpec((d, xw), lambda i: (0, 0)),
            pl.BlockSpec((n_mem, 2 * xw), lambda i: (i // per_b, 0)),
            pl.BlockSpec((xw, d), lambda i: (0, 0)),
            pl.BlockSpec((1, d), lambda i: (0, 0)),
            pl.BlockSpec((1, d), lambda i: (0, 0)),
        ],
        out_specs=[pl.BlockSpec((tm, d), lambda i: (i, 0)), pl.BlockSpec((tm, d), lambda i: (i, 0))],
        out_shape=[jax.ShapeDtypeStruct((m, d), F32), jax.ShapeDtypeStruct((m, d), BF16)],
        scratch_shapes=[pltpu.VMEM((tm, d), F32)],
        compiler_params=_params("arbitrary"),
        name="xattn_ln",
    )(hb, h, wq_b, kv_b, wo_b, g, b)


def _router_kernel(h_ref, w_ref, b_ref, topw_ref, topi_ref, sel_ref):
    logits = jnp.dot(h_ref[...], w_ref[...], preferred_element_type=F32,
                     precision=lax.Precision.HIGHEST) + b_ref[...]
    lane = lax.broadcasted_iota(jnp.int32, logits.shape, 1)
    neg = jnp.float32(-jnp.inf)
    masked = jnp.where(lane < N_EXPERTS, logits, neg)
    top_vals, top_idx = [], []
    sel = jnp.zeros_like(logits)
    for _ in range(TOP_K):
        mval = jnp.max(masked, axis=-1, keepdims=True)
        idx = jnp.min(jnp.where(masked == mval, lane, LANES), axis=-1, keepdims=True)
        hot = lane == idx
        top_vals.append(mval)
        top_idx.append(idx)
        sel = jnp.where(hot, 1.0, sel)
        masked = jnp.where(hot, neg, masked)
    exps = [jnp.exp(v - top_vals[0]) for v in top_vals]
    denom = exps[0]
    for e in exps[1:]:
        denom = denom + e
    topw = jnp.zeros_like(logits)
    topi = jnp.zeros(logits.shape, jnp.int32)
    for k in range(TOP_K):
        topw = jnp.where(lane == k, exps[k] / denom, topw)
        topi = jnp.where(lane == k, top_idx[k], topi)
    topw_ref[...] = topw
    topi_ref[...] = topi
    sel_ref[...] = sel


def moe_router(h, w_pad, b_pad, *, tm=512):
    m, d = h.shape
    out_spec = pl.BlockSpec((tm, LANES), lambda i: (i, 0))
    return pl.pallas_call(
        _router_kernel,
        grid=(m // tm,),
        in_specs=[
            pl.BlockSpec((tm, d), lambda i: (i, 0)),
            pl.BlockSpec((d, LANES), lambda i: (0, 0)),
            pl.BlockSpec((1, LANES), lambda i: (0, 0)),
        ],
        out_specs=[out_spec, out_spec, out_spec],
        out_shape=[jax.ShapeDtypeStruct((m, LANES), F32), jax.ShapeDtypeStruct((m, LANES), jnp.int32),
                   jax.ShapeDtypeStruct((m, LANES), F32)],
        compiler_params=_params("arbitrary"),
        name="moe_router",
    )(h, w_pad, b_pad)


MOE_TM = 256


def moe_route_plan(topi, sel, *, tm):
    t = topi.shape[0]
    n_tiles = t * TOP_K // tm + N_EXPERTS
    rows = n_tiles * tm
    top4 = topi[:, :TOP_K]
    seli = sel[:, :N_EXPERTS].astype(jnp.int32)
    pos_incl = jnp.cumsum(seli, axis=0)
    counts = pos_incl[-1]
    pos = pos_incl - seli
    tiles_e = (counts + tm - 1) // tm
    tile_end = jnp.cumsum(tiles_e)
    row_off = (tile_end - tiles_e) * tm
    dest = (row_off[top4] + jnp.take_along_axis(pos, top4, axis=1)).reshape(-1)
    tok_ids = jnp.repeat(jnp.arange(t, dtype=jnp.int32), TOP_K)
    slot_rows = (jnp.arange(t, dtype=jnp.int32)[:, None] + t * jnp.arange(TOP_K, dtype=jnp.int32)[None, :]).reshape(-1)
    tok = jnp.zeros((rows,), jnp.int32).at[dest].set(tok_ids, unique_indices=True)
    ydst = (TOP_K * t + jnp.arange(rows, dtype=jnp.int32) % (2 * tm)).at[dest].set(slot_rows, unique_indices=True)
    tile_ids = jnp.arange(n_tiles, dtype=jnp.int32)
    tile_expert = jnp.minimum(jnp.sum(tile_end[None, :] <= tile_ids[:, None], axis=1), N_EXPERTS - 1)
    n_valid = tile_end[-1:].astype(jnp.int32)
    return (tok.reshape(n_tiles, 1, tm), ydst.reshape(n_tiles, 1, tm), tile_expert.astype(jnp.int32), n_valid)


def _moe_ffn_kernel(te_ref, nv_ref, tok_ref, tokn_ref, ydst_ref, h_hbm, wg_ref, wu_ref, wd_ref,
                    bg_ref, bu_ref, bd_ref, y_hbm, xg, og, wgu_b, wd_b, gsem, ssem, *, tm, ff):
    r = pl.program_id(0)
    nv = nv_ref[0]
    slot = lax.rem(r, 2)

    def row_gather(idx_ref, s):
        def body(i, c):
            pltpu.make_async_copy(h_hbm.at[pl.ds(idx_ref[0, i], 1), :], xg.at[s, pl.ds(i, 1), :], gsem.at[s]).start()
            return c
        lax.fori_loop(0, tm, body, 0, unroll=8)

    def gather_wait(s):
        pltpu.make_async_copy(h_hbm.at[pl.ds(0, tm), :], xg.at[s], gsem.at[s]).wait()

    def scatter_wait(s):
        pltpu.make_async_copy(og.at[s], y_hbm.at[pl.ds(0, tm), :], ssem.at[s]).wait()

    @pl.when(r == 0)
    def _():
        row_gather(tok_ref, 0)
        og[1] = jnp.zeros(og.shape[1:], og.dtype)
        base = y_hbm.shape[0] - 2 * tm
        for part in range(2):
            fill = pltpu.make_async_copy(og.at[1], y_hbm.at[pl.ds(base + part * tm, tm), :], ssem.at[1])
            fill.start()
            fill.wait()

    @pl.when(r + 1 < nv)
    def _():
        row_gather(tokn_ref, 1 - slot)

    @pl.when(r < nv)
    def _():
        gather_wait(slot)

        @pl.when(r >= 2)
        def _():
            scatter_wait(slot)

        @pl.when((r == 0) | (te_ref[r] != te_ref[jnp.maximum(r - 1, 0)]))
        def _():
            wgu_b[:, :ff] = wg_ref[...].astype(BF16)
            wgu_b[:, ff:] = wu_ref[...].astype(BF16)
            wd_b[...] = wd_ref[...].astype(BF16)

        x = xg[slot].astype(BF16)
        gu = jnp.dot(x, wgu_b[...], preferred_element_type=F32)
        g = jnp.minimum(gu[:, :ff] + bg_ref[...], SWIGLU_LIMIT)
        u = jnp.clip(gu[:, ff:] + bu_ref[...], -SWIGLU_LIMIT, SWIGLU_LIMIT)
        act = (g * jax.nn.sigmoid(SWIGLU_ALPHA * g) * (u + 1.0)).astype(BF16)
        og[slot] = jnp.dot(act, wd_b[...], preferred_element_type=F32) + bd_ref[...]

        def body(i, c):
            pltpu.make_async_copy(og.at[slot, pl.ds(i, 1), :], y_hbm.at[pl.ds(ydst_ref[0, i], 1), :],
                                  ssem.at[slot]).start()
            return c
        lax.fori_loop(0, tm, body, 0, unroll=8)

        @pl.when(r == nv - 1)
        def _():
            scatter_wait(slot)

            @pl.when(r >= 1)
            def _():
                scatter_wait(1 - slot)


def moe_experts(h, plan, w_gate, b_gate, w_up, b_up, w_down, b_down, layer, *, tm):
    tok, ydst, tile_expert, n_valid = plan
    t, d = h.shape
    n_tiles = tok.shape[0]
    n_e, ff = w_gate.shape[1], w_gate.shape[3]
    bg = b_gate.reshape(b_gate.shape[0], n_e, 1, ff)
    bu = b_up.reshape(b_up.shape[0], n_e, 1, ff)
    bd = b_down.reshape(b_down.shape[0], n_e, 1, d)

    def expert(r, te, nv):
        return te[jnp.minimum(r, nv[0] - 1)]

    smem_blk = functools.partial(pl.BlockSpec, (None, 1, tm), memory_space=pltpu.SMEM)
    grid_spec = pltpu.PrefetchScalarGridSpec(
        num_scalar_prefetch=2,
        grid=(n_tiles,),
        in_specs=[
            smem_blk(lambda r, te, nv: (r, 0, 0)),
            smem_blk(lambda r, te, nv: (jnp.minimum(r + 1, n_tiles - 1), 0, 0)),
            smem_blk(lambda r, te, nv: (r, 0, 0)),
            pl.BlockSpec(memory_space=pl.ANY),
            pl.BlockSpec((None, None, d, ff), lambda r, te, nv: (layer, expert(r, te, nv), 0, 0)),
            pl.BlockSpec((None, None, d, ff), lambda r, te, nv: (layer, expert(r, te, nv), 0, 0)),
            pl.BlockSpec((None, None, ff, d), lambda r, te, nv: (layer, expert(r, te, nv), 0, 0)),
            pl.BlockSpec((None, None, 1, ff), lambda r, te, nv: (layer, expert(r, te, nv), 0, 0)),
            pl.BlockSpec((None, None, 1, ff), lambda r, te, nv: (layer, expert(r, te, nv), 0, 0)),
            pl.BlockSpec((None, None, 1, d), lambda r, te, nv: (layer, expert(r, te, nv), 0, 0)),
        ],
        out_specs=pl.BlockSpec(memory_space=pl.ANY),
        scratch_shapes=[
            pltpu.VMEM((2, tm, d), F32),
            pltpu.VMEM((2, tm, d), F32),
            pltpu.VMEM((d, 2 * ff), BF16),
            pltpu.VMEM((ff, d), BF16),
            pltpu.SemaphoreType.DMA((2,)),
            pltpu.SemaphoreType.DMA((2,)),
        ],
    )
    return pl.pallas_call(
        functools.partial(_moe_ffn_kernel, tm=tm, ff=ff),
        grid_spec=grid_spec,
        out_shape=jax.ShapeDtypeStruct((TOP_K * t + 2 * tm, d), F32),
        compiler_params=_params("arbitrary"),
        name="moe_experts",
    )(tile_expert, n_valid, tok, tok, ydst, h, w_gate, w_up, w_down, bg, bu, bd)


def _moe_combine_kernel(y0_ref, y1_ref, y2_ref, y3_ref, w_ref, h_ref, g_ref, b_ref, of_ref, ob_ref, *, alpha, rows):
    tm = h_ref.shape[0]
    y_refs = (y0_ref, y1_ref, y2_ref, y3_ref)

    def body(rr, carry):
        sl = pl.ds(pl.multiple_of(rr * rows, rows), rows)
        w = w_ref[sl, :]
        z = alpha * h_ref[sl, :]
        for k, y_ref in enumerate(y_refs):
            z = z + w[:, k:k + 1] * y_ref[sl, :]
        mu = jnp.mean(z, axis=-1, keepdims=True)
        zc = z - mu
        var = jnp.mean(zc * zc, axis=-1, keepdims=True)
        y = zc * lax.rsqrt(var + LN_EPS) * g_ref[...] + b_ref[...]
        of_ref[sl, :] = y
        ob_ref[sl, :] = y.astype(BF16)
        return carry

    lax.fori_loop(0, tm // rows, body, 0)


def moe_combine_ln(y, topw, h, g, b, alpha, *, tm=128):
    t, d = h.shape
    nb = t // tm

    def y_spec(k):
        return pl.BlockSpec((tm, d), lambda i: (k * nb + i, 0))

    row_spec = pl.BlockSpec((tm, d), lambda i: (i, 0))
    return pl.pallas_call(
        functools.partial(_moe_combine_kernel, alpha=alpha, rows=32),
        grid=(nb,),
        in_specs=[y_spec(0), y_spec(1), y_spec(2), y_spec(3),
                  pl.BlockSpec((tm, LANES), lambda i: (i, 0)), row_spec,
                  pl.BlockSpec((1, d), lambda i: (0, 0)), pl.BlockSpec((1, d), lambda i: (0, 0))],
        out_specs=[row_spec, row_spec],
        out_shape=[jax.ShapeDtypeStruct((t, d), F32), jax.ShapeDtypeStruct((t, d), BF16)],
        compiler_params=_params("arbitrary"),
        name="moe_combine_ln",
    )(y, y, y, y, topw, h, g, b)


NEG = -1e30
MIX_CHUNK = 256
BAND_BLOCK = 256
SB_TQ = 512
SB_TK = 256

def _head_norm_rows(x, gain):
    mu = jnp.mean(x, axis=-1, keepdims=True)
    xc = x - mu
    var = jnp.mean(xc * xc, axis=-1, keepdims=True)
    return xc * lax.rsqrt(var + HN_EPS) * gain


def _log_sigmoid(x):
    return jnp.minimum(x, 0.0) - jnp.log(1.0 + jnp.exp(-jnp.abs(x)))


def _split_bf16(x):
    hi = x.astype(BF16)
    lo = (x - hi.astype(F32)).astype(BF16)
    return hi, lo


def _lane_select(x, lane_idx):
    lane = lax.broadcasted_iota(jnp.int32, x.shape, 1)
    col = jnp.sum(jnp.where(lane == lane_idx, x, 0.0), axis=-1, keepdims=True)
    return jnp.broadcast_to(col, x.shape)


def _retention_body(q_ref, k_ref, v_ref, g_ref, cos_ref, sin_ref, lg_ref, gain_ref, o_ref,
                    qs_ref, ks_ref, st_ref, *, scale):
    seq = q_ref.shape[0]
    L = MIX_CHUNK
    half = HEAD_DIM // 2
    cos = cos_ref[...]
    sin = sin_ref[...]
    q = q_ref[...].astype(F32)
    k = k_ref[...].astype(F32)
    qs_ref[...] = (q * cos + pltpu.roll(q, half, 1) * sin).astype(BF16)
    ks_ref[...] = ((k * cos + pltpu.roll(k, half, 1) * sin) * scale).astype(BF16)

    lg = lg_ref[...]
    ri = lax.broadcasted_iota(jnp.int32, (L, L), 0)
    ci = lax.broadcasted_iota(jnp.int32, (L, L), 1)
    intra = jnp.where(ri >= ci, jnp.exp(lg * jnp.maximum(ri - ci, 0).astype(F32)), 0.0)
    rr = lax.broadcasted_iota(jnp.int32, (L, HEAD_DIM), 0).astype(F32)
    lg_d = lg[:, :HEAD_DIM]
    q_dec = jnp.exp(lg_d * (rr + 1.0))
    k_dec = jnp.exp(lg_d * (L - 1.0 - rr))
    c_dec = jnp.exp(lg_d * float(L))
    gain = gain_ref[...]
    st_ref[...] = jnp.zeros_like(st_ref)

    def chunk(c, carry):
        sl = pl.ds(pl.multiple_of(c * L, L), L)
        qc = qs_ref[sl, :]
        kc = ks_ref[sl, :]
        vc = v_ref[sl, :]
        state = st_ref[:, :HEAD_DIM]
        att = lax.dot_general(qc, kc, _NT, preferred_element_type=F32) * intra
        o = (jnp.dot(att.astype(BF16), vc, preferred_element_type=F32)
             + jnp.dot((qc.astype(F32) * q_dec).astype(BF16), state.astype(BF16), preferred_element_type=F32))
        st_ref[:, :HEAD_DIM] = state * c_dec + lax.dot_general(
            (kc.astype(F32) * k_dec).astype(BF16), vc, _TN, preferred_element_type=F32)
        gv = g_ref[sl, :].astype(F32)
        o_ref[sl, :] = (_head_norm_rows(o, gain) * (gv * jax.nn.sigmoid(gv))).astype(o_ref.dtype)
        return carry

    lax.fori_loop(0, seq // L, chunk, 0)


def _mlstm_body(u_ref, v_ref, og_ref, gates_ref, cw_ref, cb_ref, wq_ref, wk_ref, gain_ref, o_ref,
                qs_ref, ks_ref, st_ref, m_ref, *, head, heads, scale):
    seq = u_ref.shape[0]
    L = MIX_CHUNK
    d = HEAD_DIM
    x = u_ref[...].astype(F32)
    row = lax.broadcasted_iota(jnp.int32, x.shape, 0)
    cw = cw_ref[...]
    y = x * cw[CONV_K - 1:CONV_K, :] + cb_ref[...]
    for sh in range(1, CONV_K):
        xs = jnp.where(row >= sh, pltpu.roll(x, sh, 0), 0.0)
        y = y + xs * cw[CONV_K - 1 - sh:CONV_K - sh, :]
    ub = (y * jax.nn.sigmoid(y)).astype(BF16)
    qs_ref[...] = jnp.dot(ub, wq_ref[...].astype(BF16), preferred_element_type=F32).astype(BF16)
    ks_ref[...] = (jnp.dot(ub, wk_ref[...].astype(BF16), preferred_element_type=F32) * scale).astype(BF16)

    ri = lax.broadcasted_iota(jnp.int32, (L, L), 0)
    ci = lax.broadcasted_iota(jnp.int32, (L, L), 1)
    causal = ri >= ci
    tri = jnp.where(causal, 1.0, 0.0).astype(BF16)
    ones_v = jnp.ones((L, d), BF16)
    gain = gain_ref[...]
    st_ref[...] = jnp.zeros_like(st_ref)
    m_ref[...] = jnp.zeros_like(m_ref)

    def chunk(c, carry):
        sl = pl.ds(pl.multiple_of(c * L, L), L)
        qc = qs_ref[sl, :]
        kc = ks_ref[sl, :]
        v_ext = jnp.concatenate([v_ref[sl, :], ones_v], axis=1)
        gts = gates_ref[sl, :]
        ic = _lane_select(gts, head)
        lf = _log_sigmoid(_lane_select(gts, heads + head))
        lf_hi, lf_lo = _split_bf16(lf)
        bcum = (jnp.dot(tri, lf_hi, preferred_element_type=F32)
                + jnp.dot(tri, lf_lo, preferred_element_type=F32))
        m_st = m_ref[...]
        src = jnp.transpose(ic - bcum)[:1, :]
        bcum2 = jnp.concatenate([bcum, bcum], axis=1)
        log_intra = jnp.where(causal, bcum2 + src, NEG)
        m_intra = jnp.max(log_intra, axis=-1, keepdims=True)
        log_cross = bcum + m_st
        m_row = jnp.maximum(log_cross, m_intra)
        m_row2 = jnp.concatenate([m_row, m_row], axis=1)
        w_intra = jnp.exp(log_intra - m_row2)
        w_cross = jnp.exp(log_cross - m_row)
        w_cross2 = jnp.concatenate([w_cross, w_cross], axis=1)
        qk = lax.dot_general(qc, kc, _NT, preferred_element_type=F32) * w_intra
        state = st_ref[...]
        res = (jnp.dot(qk.astype(BF16), v_ext, preferred_element_type=F32)
               + w_cross2 * jnp.dot(qc, state.astype(BF16), preferred_element_type=F32))
        num = res[:, :d]
        den = res[:, d:]
        hh = num / jnp.maximum(jnp.abs(den), jnp.exp(-m_row))
        og = og_ref[sl, :].astype(F32)
        o_ref[sl, :] = _head_norm_rows(hh * jax.nn.sigmoid(og), gain).astype(o_ref.dtype)
        b_last = bcum[L - 1:L, :]
        log_state = b_last - bcum + ic
        m_new = jnp.maximum(b_last + m_st, jnp.max(log_state, axis=0, keepdims=True))
        decay = jnp.exp(b_last + m_st - m_new)
        kw = (kc.astype(F32) * jnp.exp(log_state - m_new)).astype(BF16)
        decay2 = jnp.concatenate([decay, decay], axis=1)
        st_ref[...] = decay2 * state + lax.dot_general(kw, v_ext, _TN, preferred_element_type=F32)
        m_ref[...] = m_new
        return carry

    lax.fori_loop(0, seq // L, chunk, 0)


def _mixer_ab_kernel(a0_ref, a1_ref, a2_ref, a3_ref, gates_ref, cos_ref, sin_ref, lg_ref, cw_ref, cb_ref,
                     wq_ref, wk_ref, gain_ref, o_ref, qs_ref, ks_ref, st_ref, m_ref, *, heads, scale):
    g = pl.program_id(1)

    @pl.when(g < heads)
    def _():
        _retention_body(a0_ref, a1_ref, a2_ref, a3_ref, cos_ref, sin_ref, lg_ref, gain_ref, o_ref,
                        qs_ref, ks_ref, st_ref, scale=scale)

    @pl.when(g >= heads)
    def _():
        _mlstm_body(a0_ref, a1_ref, a2_ref, gates_ref, cw_ref, cb_ref, wq_ref, wk_ref, gain_ref, o_ref,
                    qs_ref, ks_ref, st_ref, m_ref, head=g - heads, heads=heads, scale=scale)


def mixer_ab(z, gates, cos2, sin2, log_g, conv_w, conv_b, wq_m, wk_m, gain, layer, *, bsz, seq, heads):
    t = z.shape[0]
    d = HEAD_DIM
    L = MIX_CHUNK

    def col(base_ret, base_ml):
        def index(b, g):
            is_ml = g // heads
            return (b, (1 - is_ml) * (base_ret * heads + g) + is_ml * (base_ml * heads + g - heads))
        return pl.BlockSpec((seq, d), index)

    def ml_head(g):
        return jnp.maximum(g - heads, 0)

    conv_w4 = conv_w.reshape(conv_w.shape[0], CONV_K, heads, d).transpose(0, 2, 1, 3)
    conv_b4 = conv_b.reshape(conv_b.shape[0], heads, 1, d)
    gain4 = gain.reshape(2 * heads, 1, d)
    return pl.pallas_call(
        functools.partial(_mixer_ab_kernel, heads=heads, scale=d ** -0.5),
        grid=(bsz, 2 * heads),
        in_specs=[
            col(0, 4), col(1, 5), col(2, 6), col(3, 6),
            pl.BlockSpec((seq, LANES), lambda b, g: (b, 0)),
            pl.BlockSpec((seq, d), lambda b, g: (0, 0)),
            pl.BlockSpec((seq, d), lambda b, g: (0, 0)),
            pl.BlockSpec((None, 1, L), lambda b, g: (jnp.minimum(g, heads - 1), 0, 0)),
            pl.BlockSpec((None, None, CONV_K, d), lambda b, g: (layer, ml_head(g), 0, 0)),
            pl.BlockSpec((None, None, 1, d), lambda b, g: (layer, ml_head(g), 0, 0)),
            pl.BlockSpec((None, None, d, d), lambda b, g: (layer, ml_head(g), 0, 0)),
            pl.BlockSpec((None, None, d, d), lambda b, g: (layer, ml_head(g), 0, 0)),
            pl.BlockSpec((None, 1, d), lambda b, g: (g, 0, 0)),
        ],
        out_specs=pl.BlockSpec((seq, d), lambda b, g: (b, g)),
        out_shape=jax.ShapeDtypeStruct((t, 2 * heads * d), BF16),
        scratch_shapes=[
            pltpu.VMEM((seq, d), BF16),
            pltpu.VMEM((seq, d), BF16),
            pltpu.VMEM((d, 2 * d), F32),
            pltpu.VMEM((1, d), F32),
        ],
        compiler_params=_params("arbitrary", "arbitrary"),
        name="mixer_ab",
    )(z, z, z, z, gates, cos2, sin2, log_g, conv_w4, conv_b4, wq_m, wk_m, gain4)


def _band_body(q_ref, k_ref, v_ref, bias_ref, o_ref, *, scale):
    seq = q_ref.shape[0]
    bq = BAND_BLOCK
    n_back = bias_ref.shape[0]

    def block(i, carry):
        sl = pl.ds(pl.multiple_of(i * bq, bq), bq)
        q = q_ref[sl, :]
        scores, vals = [], []
        for dlt in range(n_back):
            ks = pl.ds(pl.multiple_of(jnp.maximum(i - dlt, 0) * bq, bq), bq)
            s = lax.dot_general(q, k_ref[ks, :], _NT, preferred_element_type=F32) * scale + bias_ref[dlt]
            scores.append(jnp.where(i - dlt >= 0, s, NEG))
            vals.append(v_ref[ks, :])
        m = jnp.max(scores[0], axis=-1, keepdims=True)
        for s in scores[1:]:
            m = jnp.maximum(m, jnp.max(s, axis=-1, keepdims=True))
        acc = jnp.zeros((bq, HEAD_DIM), F32)
        l = jnp.zeros((bq, 1), F32)
        for s, vv in zip(scores, vals):
            p = jnp.exp(s - m)
            l = l + jnp.sum(p, axis=-1, keepdims=True)
            acc = acc + jnp.dot(p.astype(BF16), vv, preferred_element_type=F32)
        o_ref[sl, :] = (acc / l).astype(o_ref.dtype)
        return carry

    lax.fori_loop(0, seq // bq, block, 0)


def _stick_breaking_body(q_ref, k_ref, v_ref, o_ref, acc_ref, r_ref, *, scale):
    seq = q_ref.shape[0]
    tq, tk = SB_TQ, SB_TK
    per_q = tq // tk
    ri = lax.broadcasted_iota(jnp.int32, (tk, tk), 0)
    ci = lax.broadcasted_iota(jnp.int32, (tk, tk), 1)
    suffix = jnp.where(ri > ci, 1.0, 0.0).astype(BF16)
    qrow = lax.broadcasted_iota(jnp.int32, (tq, tk), 0)
    kcol = lax.broadcasted_iota(jnp.int32, (tq, tk), 1)

    def one_pair(q, q0, j, masked):
        k0 = pl.multiple_of(j * tk, tk)
        kb = k_ref[pl.ds(k0, tk), :]
        vb = v_ref[pl.ds(k0, tk), :]
        z = lax.dot_general(q, kb, _NT, preferred_element_type=F32) * scale
        sp = jnp.maximum(z, 0.0) + jnp.log(1.0 + jnp.exp(-jnp.abs(z)))
        ls_pos = z - sp
        lr = -sp
        if masked:
            strict = (k0 + kcol) < (q0 + qrow)
            lr = jnp.where(strict, lr, 0.0)
        hi, lo = _split_bf16(lr)
        between = (jnp.dot(hi, suffix, preferred_element_type=F32)
                   + jnp.dot(lo, suffix, preferred_element_type=F32))
        r = r_ref[...]
        log_a = ls_pos + between + jnp.concatenate([r] * (tk // LANES), axis=1)
        p = jnp.exp(log_a)
        if masked:
            p = jnp.where(strict, p, 0.0)
        acc_ref[...] += jnp.dot(p.astype(BF16), vb, preferred_element_type=F32)
        r_ref[...] = r + jnp.sum(lr, axis=-1, keepdims=True)

    def q_block(i, carry):
        q0 = pl.multiple_of(i * tq, tq)
        q = q_ref[pl.ds(q0, tq), :]
        acc_ref[...] = jnp.zeros_like(acc_ref)
        r_ref[...] = jnp.zeros_like(r_ref)
        last = (i + 1) * per_q - 1
        for dd in range(per_q):
            one_pair(q, q0, last - dd, True)

        def below(jj, c):
            one_pair(q, q0, i * per_q - 1 - jj, False)
            return c

        lax.fori_loop(0, i * per_q, below, 0)
        o_ref[pl.ds(q0, tq), :] = acc_ref[...].astype(o_ref.dtype)
        return carry

    lax.fori_loop(0, seq // tq, q_block, 0)


def _mixer_cd_kernel(q_ref, k_ref, v_ref, bias_ref, o_ref, acc_ref, r_ref, *, heads, scale):
    g = pl.program_id(1)

    @pl.when(g < heads)
    def _():
        _band_body(q_ref, k_ref, v_ref, bias_ref, o_ref, scale=scale)

    @pl.when(g >= heads)
    def _():
        _stick_breaking_body(q_ref, k_ref, v_ref, o_ref, acc_ref, r_ref, scale=scale)


def _band_bias_kernel(rrow_ref, o_ref):
    bq = BAND_BLOCK
    shift = CHUNK.bit_length() - 1
    qo = lax.broadcasted_iota(jnp.int32, (bq, bq), 0)
    ck = jnp.right_shift(lax.broadcasted_iota(jnp.int32, (bq, bq), 1), shift)
    for dlt in range(o_ref.shape[0]):
        x = jnp.broadcast_to(rrow_ref[dlt], (bq, 2 * bq))
        toep = pltpu.roll(x, 0, 1, stride=1, stride_axis=0)[:, :bq]
        cq = jnp.right_shift(qo + bq * dlt, shift)
        allowed = (ck <= cq) & (ck >= cq - PAST_CHUNKS)
        o_ref[dlt] = jnp.where(allowed, toep, NEG)


def band_bias_table(rel_bias):
    bq = BAND_BLOCK
    heads = rel_bias.shape[0]
    n_back = PAST_CHUNKS * CHUNK // bq + 1
    m = jnp.arange(2 * bq)
    key_minus_query = jnp.where(m < bq, m, m - 2 * bq)
    dist = bq * jnp.arange(n_back)[:, None] - key_minus_query[None, :]
    idx = jnp.clip(dist, -(CHUNK - 1), REL_MAX) + (CHUNK - 1)
    rrow = rel_bias.astype(F32)[:, idx].reshape(heads, n_back, 1, 2 * bq)
    return pl.pallas_call(
        _band_bias_kernel,
        grid=(heads,),
        in_specs=[pl.BlockSpec((None, n_back, 1, 2 * bq), lambda hh: (hh, 0, 0, 0))],
        out_specs=pl.BlockSpec((None, n_back, bq, bq), lambda hh: (hh, 0, 0, 0)),
        out_shape=jax.ShapeDtypeStruct((heads, n_back, bq, bq), F32),
        compiler_params=_params("arbitrary"),
        name="band_bias",
    )(rrow)


def mixer_cd(z, bias_tab, *, bsz, seq, heads):
    t = z.shape[0]
    d = HEAD_DIM

    def col(which):
        return pl.BlockSpec((seq, d), lambda b, g: (b, (3 * (g // heads) + which) * heads + g % heads))

    return pl.pallas_call(
        functools.partial(_mixer_cd_kernel, heads=heads, scale=d ** -0.5),
        grid=(bsz, 2 * heads),
        in_specs=[
            col(0), col(1), col(2),
            pl.BlockSpec((None,) + bias_tab.shape[1:], lambda b, g: (jnp.minimum(g, heads - 1), 0, 0, 0)),
        ],
        out_specs=pl.BlockSpec((seq, d), lambda b, g: (b, g)),
        out_shape=jax.ShapeDtypeStruct((t, 2 * heads * d), BF16),
        scratch_shapes=[pltpu.VMEM((SB_TQ, d), F32), pltpu.VMEM((SB_TQ, LANES), F32)],
        compiler_params=_params("arbitrary", "arbitrary"),
        name="mixer_cd",
    )(z, z, z, bias_tab)


def rope_tables(seq_len):
    pos = jnp.arange(seq_len, dtype=F32)
    inv_freq = ROPE_BASE ** (-jnp.arange(0, HEAD_DIM, 2, dtype=F32) / HEAD_DIM)
    ang = pos[:, None] * inv_freq[None, :]
    cos, sin = jnp.cos(ang), jnp.sin(ang)
    return jnp.concatenate([cos, cos], axis=1), jnp.concatenate([-sin, sin], axis=1)


def retention_log_decay(heads):
    lg = jnp.log1p(-jnp.exp2(-(5.0 + jnp.arange(heads, dtype=F32))))
    return jnp.broadcast_to(lg[:, None, None], (heads, 1, MIX_CHUNK))


def _pad_lanes(a, value=0.0):
    return jnp.pad(a, ((0, 0), (0, LANES - a.shape[1])), constant_values=value)


def kernel(x, mem, ab_w_in, ab_gate_b, ab_conv_w, ab_conv_b, ab_wq, ab_wk, ab_ret_norm_g, ab_mlstm_norm_g, ab_w_out, cd_w_in, cd_rel_bias, cd_w_out, mix_ln_g, mix_ln_b, xa_wq, xa_wkv, xa_wo, xa_ln_g, xa_ln_b, moe_router_w, moe_router_b, moe_w_gate, moe_b_gate, moe_w_up, moe_b_up, moe_w_down, moe_b_down, moe_ln_g, moe_ln_b):
    bsz, seq, d = x.shape
    depth = mix_ln_g.shape[0]
    heads = GROUP_HEADS
    alpha = (2.0 * depth) ** 0.25
    t = bsz * seq
    cos2, sin2 = rope_tables(seq)
    log_g = retention_log_decay(heads)
    h = x.reshape(t, d)
    hb = h.astype(BF16)
    memb = mem.reshape(-1, d).astype(BF16)
    n_ab = 7 * GROUP_WIDTH
    n_cd = 6 * GROUP_WIDTH
    ab_w_in_t = jnp.swapaxes(ab_w_in, 1, 2)
    for layer in range(depth):
        i = layer // 2
        if layer % 2 == 0:
            z = matmul_nt_stacked(hb, ab_w_in_t, i, n_ab, tm=1024, tn=512, out_dtype=BF16)
            gates = gates_nt(hb, ab_w_in_t, i, n_ab, 2 * heads, _pad_lanes(ab_gate_b[i][None, :]), tm=1024)
            gain = jnp.concatenate([ab_ret_norm_g[i], ab_mlstm_norm_g[i]])
            yb = mixer_ab(z, gates, cos2, sin2, log_g, ab_conv_w, ab_conv_b, ab_wq, ab_wk, gain, i,
                          bsz=bsz, seq=seq, heads=heads)
            w_out = ab_w_out[i].astype(BF16)
        else:
            z = matmul_stacked(hb, cd_w_in, i, n_cd, tm=1024, tn=512, out_dtype=BF16)
            yb = mixer_cd(z, band_bias_table(cd_rel_bias[i]), bsz=bsz, seq=seq, heads=heads)
            w_out = cd_w_out[i].astype(BF16)
        h, hb = matmul_ln(yb, w_out, h, mix_ln_g[layer][None, :], mix_ln_b[layer][None, :], alpha)

        kvb = matmul_stacked(memb, xa_wkv, layer, xa_wkv.shape[2], tm=memb.shape[0], tn=512, out_dtype=BF16)
        h, hb = xattn_ln(hb, h, xa_wq[layer].astype(BF16), kvb, xa_wo[layer].astype(BF16),
                         xa_ln_g[layer][None, :], xa_ln_b[layer][None, :], alpha, seq=seq)

        topw, topi, sel = moe_router(h, _pad_lanes(moe_router_w[layer]), _pad_lanes(moe_router_b[layer][None, :]))
        plan = moe_route_plan(topi, sel, tm=MOE_TM)
        y = moe_experts(h, plan, moe_w_gate, moe_b_gate, moe_w_up, moe_b_up, moe_w_down, moe_b_down, layer, tm=MOE_TM)
        h, hb = moe_combine_ln(y, topw, h, moe_ln_g[layer][None, :], moe_ln_b[layer][None, :], alpha)
    return h.reshape(bsz, seq, d)
```

```python
import functools

import jax
import jax.numpy as jnp
from jax import lax
from jax.experimental import pallas as pl
from jax.experimental.pallas import tpu as pltpu

F32 = jnp.float32
BF16 = jnp.bfloat16

CHUNK = 64
HEAD_DIM = 128
GROUP_HEADS = 16
GROUP_WIDTH = GROUP_HEADS * HEAD_DIM
CONV_K = 4
PAST_CHUNKS = 8
REL_MAX = 2 * CHUNK
ROPE_BASE = 10000.0
XA_HEADS = 4
N_EXPERTS = 32
TOP_K = 4
SWIGLU_LIMIT = 7.0
SWIGLU_ALPHA = 1.702
LN_EPS = 1e-5
HN_EPS = 1e-6

LANES = 128
VMEM_LIMIT_BYTES = 58 * 1024 * 1024


def _params(*sem):
    return pltpu.CompilerParams(dimension_semantics=sem, vmem_limit_bytes=VMEM_LIMIT_BYTES)


def _mm_kernel(x_ref, w_ref, o_ref, wb_ref):
    @pl.when(pl.program_id(1) == 0)
    def _():
        wb_ref[...] = w_ref[...].astype(BF16)

    o_ref[...] = jnp.dot(x_ref[...], wb_ref[...], preferred_element_type=F32).astype(o_ref.dtype)


def matmul_stacked(x, w, layer, n_cols, *, tm, tn, out_dtype):
    m, k = x.shape
    return pl.pallas_call(
        _mm_kernel,
        grid=(n_cols // tn, m // tm),
        in_specs=[
            pl.BlockSpec((tm, k), lambda j, i: (i, 0)),
            pl.BlockSpec((None, k, tn), lambda j, i: (layer, 0, j)),
        ],
        out_specs=pl.BlockSpec((tm, tn), lambda j, i: (i, j)),
        out_shape=jax.ShapeDtypeStruct((m, n_cols), out_dtype),
        scratch_shapes=[pltpu.VMEM((k, tn), BF16)],
        compiler_params=_params("arbitrary", "arbitrary"),
        name="mm_in",
    )(x, w)


_NT = (((1,), (1,)), ((), ()))
_TN = (((0,), (0,)), ((), ()))


def _mm_nt_kernel(x_ref, w_ref, o_ref, wb_ref):
    @pl.when(pl.program_id(1) == 0)
    def _():
        wb_ref[...] = w_ref[...].astype(BF16)

    o_ref[...] = lax.dot_general(x_ref[...], wb_ref[...], _NT, preferred_element_type=F32).astype(o_ref.dtype)


def matmul_nt_stacked(x, wt, layer, n_cols, *, tm, tn, out_dtype):
    m, k = x.shape
    return pl.pallas_call(
        _mm_nt_kernel,
        grid=(n_cols // tn, m // tm),
        in_specs=[
            pl.BlockSpec((tm, k), lambda j, i: (i, 0)),
            pl.BlockSpec((None, tn, k), lambda j, i: (layer, j, 0)),
        ],
        out_specs=pl.BlockSpec((tm, tn), lambda j, i: (i, j)),
        out_shape=jax.ShapeDtypeStruct((m, n_cols), out_dtype),
        scratch_shapes=[pltpu.VMEM((tn, k), BF16)],
        compiler_params=_params("arbitrary", "arbitrary"),
        name="mm_in_nt",
    )(x, wt)


def _gates_kernel(x_ref, w_ref, b_ref, o_ref, wb_ref):
    @pl.when(pl.program_id(0) == 0)
    def _():
        wb_ref[...] = jnp.zeros_like(wb_ref)
        wb_ref[:w_ref.shape[0], :] = w_ref[...].astype(BF16)

    o_ref[...] = lax.dot_general(x_ref[...], wb_ref[...], _NT, preferred_element_type=F32) + b_ref[...]


def gates_nt(x, wt, layer, row0, n_rows, b_pad, *, tm):
    m, k = x.shape
    return pl.pallas_call(
        _gates_kernel,
        grid=(m // tm,),
        in_specs=[
            pl.BlockSpec((tm, k), lambda i: (i, 0)),
            pl.BlockSpec((None, n_rows, k), lambda i: (layer, row0 // n_rows, 0)),
            pl.BlockSpec((1, LANES), lambda i: (0, 0)),
        ],
        out_specs=pl.BlockSpec((tm, LANES), lambda i: (i, 0)),
        out_shape=jax.ShapeDtypeStruct((m, LANES), F32),
        scratch_shapes=[pltpu.VMEM((LANES, k), BF16)],
        compiler_params=_params("arbitrary"),
        name="gates",
    )(x, wt, b_pad)


def _pack_bf16_pair(y):
    half = y.shape[1] // 2
    hi = pltpu.bitcast(y[:, :half].astype(BF16).astype(F32), jnp.uint32)
    lo = pltpu.bitcast(y[:, half:].astype(BF16).astype(F32), jnp.uint32)
    return hi | (lo >> 16)


def _unpack_bf16_pair(w):
    return (pltpu.bitcast(w & jnp.uint32(0xFFFF0000), F32), pltpu.bitcast(w << 16, F32))


def _ln_rows(z_ref, g_ref, b_ref, of_ref, ob_ref, rows, packed=False):
    tm = z_ref.shape[0]

    def body(r, carry):
        sl = pl.ds(pl.multiple_of(r * rows, rows), rows)
        z = z_ref[sl, :]
        mu = jnp.mean(z, axis=-1, keepdims=True)
        zc = z - mu
        var = jnp.mean(zc * zc, axis=-1, keepdims=True)
        y = zc * lax.rsqrt(var + LN_EPS) * g_ref[...] + b_ref[...]
        of_ref[sl, :] = y
        ob_ref[sl, :] = _pack_bf16_pair(y) if packed else y.astype(BF16)
        return carry

    lax.fori_loop(0, tm // rows, body, 0)


def _mm_ln_kernel(x_ref, w_ref, h_ref, g_ref, b_ref, of_ref, ob_ref, *, nk, nj, tn, alpha):
    k = pl.program_id(1)
    j = pl.program_id(2)
    part = jnp.dot(x_ref[...], w_ref[...].astype(BF16), preferred_element_type=F32)
    for jj in range(nj):
        sl = slice(jj * tn, (jj + 1) * tn)

        @pl.when((j == jj) & (k == 0))
        def _():
            of_ref[:, sl] = alpha * h_ref[...] + part

        @pl.when((j == jj) & (k > 0))
        def _():
            of_ref[:, sl] += part

    @pl.when((k == nk - 1) & (j == nj - 1))
    def _():
        _ln_rows(of_ref, g_ref, b_ref, of_ref, ob_ref, 32)


def matmul_ln(x, w, h, g, b, alpha, *, tm=512, tn=512, tk=4096):
    m, kdim = x.shape
    n = h.shape[1]
    tk = min(tk, kdim)
    nk, nj = kdim // tk, n // tn
    in_specs = [
        pl.BlockSpec((tm, tk), lambda i, k, j: (i, k)),
        pl.BlockSpec((tk, tn), lambda i, k, j: (k, j)),
        pl.BlockSpec((tm, tn), lambda i, k, j: (i, j)),
        pl.BlockSpec((1, n), lambda i, k, j: (0, 0)),
        pl.BlockSpec((1, n), lambda i, k, j: (0, 0)),
    ]
    args = [x, w, h, g, b]
    return pl.pallas_call(
        functools.partial(_mm_ln_kernel, nk=nk, nj=nj, tn=tn, alpha=alpha),
        grid=(m // tm, nk, nj),
        in_specs=in_specs,
        out_specs=[
            pl.BlockSpec((tm, n), lambda i, k, j: (i, 0)),
            pl.BlockSpec((tm, n), lambda i, k, j: (i, 0)),
        ],
        out_shape=[jax.ShapeDtypeStruct((m, n), F32), jax.ShapeDtypeStruct((m, n), BF16)],
        compiler_params=_params("arbitrary", "arbitrary", "arbitrary"),
        name="mm_ln",
    )(*args)


def _xattn_kernel(hb_ref, h_ref, wq_ref, kv_ref, wo_ref, g_ref, b_ref, of_ref, ob_ref, z_ref, *, alpha, heads, hd):
    q = jnp.dot(hb_ref[...], wq_ref[...], preferred_element_type=F32)
    scale = hd ** -0.5
    outs = []
    for hh in range(heads):
        qh = (q[:, hh * hd:(hh + 1) * hd] * scale).astype(BF16)
        kh = kv_ref[:, hh * hd:(hh + 1) * hd]
        vh = kv_ref[:, (heads + hh) * hd:(heads + hh + 1) * hd]
        s = lax.dot_general(qh, kh, (((1,), (1,)), ((), ())), preferred_element_type=F32)
        s = s - jnp.max(s, axis=-1, keepdims=True)
        p = jnp.exp(s)
        l = jnp.sum(p, axis=-1, keepdims=True)
        o = jnp.dot(p.astype(BF16), vh, preferred_element_type=F32) / l
        outs.append(o.astype(BF16))
    o_all = jnp.concatenate(outs, axis=-1)
    z_ref[...] = alpha * h_ref[...] + jnp.dot(o_all, wo_ref[...], preferred_element_type=F32)
    _ln_rows(z_ref, g_ref, b_ref, of_ref, ob_ref, 32, packed=True)


def xattn_ln(hb, h, wq_b, kv_b, wo_b, g, b, alpha, *, seq, tm=256):
    m, d = h.shape
    xw = wq_b.shape[1]
    n_mem = kv_b.shape[0] // (m // seq)
    per_b = seq // tm
    return pl.pallas_call(
        functools.partial(_xattn_kernel, alpha=alpha, heads=XA_HEADS, hd=xw // XA_HEADS),
        grid=(m // tm,),
        in_specs=[
            pl.BlockSpec((tm, d), lambda i: (i, 0)),
            pl.BlockSpec((tm, d), lambda i: (i, 0)),
            pl.BlockSpec((d, xw), lambda i: (0, 0)),
            pl.BlockSpec((n_mem, 2 * xw), lambda i: (i // per_b, 0)),
            pl.BlockSpec((xw, d), lambda i: (0, 0)),
            pl.BlockSpec((1, d), lambda i: (0, 0)),
            pl.BlockSpec((1, d), lambda i: (0, 0)),
        ],
        out_specs=[pl.BlockSpec((tm, d), lambda i: (i, 0)), pl.BlockSpec((tm, d // 2), lambda i: (i, 0))],
        out_shape=[jax.ShapeDtypeStruct((m, d), F32), jax.ShapeDtypeStruct((m, d // 2), jnp.uint32)],
        scratch_shapes=[pltpu.VMEM((tm, d), F32)],
        compiler_params=_params("arbitrary"),
        name="xattn_ln",
    )(hb, h, wq_b, kv_b, wo_b, g, b)


def _router_kernel(h_ref, w_ref, b_ref, topw_ref, topi_ref, sel_ref):
    logits = jnp.dot(h_ref[...], w_ref[...], preferred_element_type=F32,
                     precision=lax.Precision.HIGHEST) + b_ref[...]
    lane = lax.broadcasted_iota(jnp.int32, logits.shape, 1)
    neg = jnp.float32(-jnp.inf)
    masked = jnp.where(lane < N_EXPERTS, logits, neg)
    top_vals, top_idx = [], []
    sel = jnp.zeros_like(logits)
    for _ in range(TOP_K):
        mval = jnp.max(masked, axis=-1, keepdims=True)
        idx = jnp.min(jnp.where(masked == mval, lane, LANES), axis=-1, keepdims=True)
        hot = lane == idx
        top_vals.append(mval)
        top_idx.append(idx)
        sel = jnp.where(hot, 1.0, sel)
        masked = jnp.where(hot, neg, masked)
    exps = [jnp.exp(v - top_vals[0]) for v in top_vals]
    denom = exps[0]
    for e in exps[1:]:
        denom = denom + e
    topw = jnp.zeros_like(logits)
    topi = jnp.zeros(logits.shape, jnp.int32)
    for k in range(TOP_K):
        topw = jnp.where(lane == k, exps[k] / denom, topw)
        topi = jnp.where(lane == k, top_idx[k], topi)
    topw_ref[...] = topw
    topi_ref[...] = topi
    sel_ref[...] = sel


def moe_router(h, w_pad, b_pad, *, tm=512):
    m, d = h.shape
    out_spec = pl.BlockSpec((tm, LANES), lambda i: (i, 0))
    return pl.pallas_call(
        _router_kernel,
        grid=(m // tm,),
        in_specs=[
            pl.BlockSpec((tm, d), lambda i: (i, 0)),
            pl.BlockSpec((d, LANES), lambda i: (0, 0)),
            pl.BlockSpec((1, LANES), lambda i: (0, 0)),
        ],
        out_specs=[out_spec, out_spec, out_spec],
        out_shape=[jax.ShapeDtypeStruct((m, LANES), F32), jax.ShapeDtypeStruct((m, LANES), jnp.int32),
                   jax.ShapeDtypeStruct((m, LANES), F32)],
        compiler_params=_params("arbitrary"),
        name="moe_router",
    )(h, w_pad, b_pad)


MOE_TM = 256


def moe_route_plan(topi, sel, *, tm):
    t = topi.shape[0]
    n_tiles = t * TOP_K // tm + N_EXPERTS
    rows = n_tiles * tm
    top4 = topi[:, :TOP_K]
    seli = sel[:, :N_EXPERTS].astype(jnp.int32)
    pos_incl = jnp.cumsum(seli, axis=0)
    counts = pos_incl[-1]
    pos = pos_incl - seli
    tiles_e = (counts + tm - 1) // tm
    tile_end = jnp.cumsum(tiles_e)
    row_off = (tile_end - tiles_e) * tm
    dest = (row_off[top4] + jnp.take_along_axis(pos, top4, axis=1)).reshape(-1)
    slot_rows = (jnp.arange(t, dtype=jnp.int32)[:, None] + t * jnp.arange(TOP_K, dtype=jnp.int32)[None, :]).reshape(-1)
    ydst = (TOP_K * t + jnp.arange(rows, dtype=jnp.int32) % (2 * tm)).at[dest].set(slot_rows, unique_indices=True)
    tok = ydst % t
    tile_ids = jnp.arange(n_tiles, dtype=jnp.int32)
    tile_expert = jnp.minimum(jnp.sum(tile_end[None, :] <= tile_ids[:, None], axis=1), N_EXPERTS - 1)
    n_valid = tile_end[-1:].astype(jnp.int32)
    return (tok.reshape(n_tiles, 1, tm), ydst.reshape(n_tiles, 1, tm), tile_expert.astype(jnp.int32), n_valid)


def _moe_ffn_kernel(te_ref, nv_ref, tok_ref, tokn_ref, ydst_ref, h_hbm, wg_ref, wu_ref, wd_ref,
                    bg_ref, bu_ref, bd_ref, y_hbm, xg, og, wgu_b, wd_b, gsem, ssem, *, tm, ff):
    r = pl.program_id(0)
    nv = nv_ref[0]
    slot = lax.rem(r, 2)

    def row_gather(idx_ref, s):
        def body(i, c):
            pltpu.make_async_copy(h_hbm.at[pl.ds(idx_ref[0, i], 1), :], xg.at[s, pl.ds(i, 1), :], gsem.at[s]).start()
            return c
        lax.fori_loop(0, tm, body, 0, unroll=8)

    def gather_wait(s):
        pltpu.make_async_copy(h_hbm.at[pl.ds(0, tm), :], xg.at[s], gsem.at[s]).wait()

    def scatter_wait(s):
        pltpu.make_async_copy(og.at[s], y_hbm.at[pl.ds(0, tm), :], ssem.at[s]).wait()

    @pl.when(r == 0)
    def _():
        row_gather(tok_ref, 0)
        og[1] = jnp.zeros(og.shape[1:], og.dtype)
        base = y_hbm.shape[0] - 2 * tm
        for part in range(2):
            fill = pltpu.make_async_copy(og.at[1], y_hbm.at[pl.ds(base + part * tm, tm), :], ssem.at[1])
            fill.start()
            fill.wait()

    @pl.when(r + 1 < nv)
    def _():
        row_gather(tokn_ref, 1 - slot)

    @pl.when(r < nv)
    def _():
        gather_wait(slot)

        @pl.when(r >= 2)
        def _():
            scatter_wait(slot)

        @pl.when((r == 0) | (te_ref[r] != te_ref[jnp.maximum(r - 1, 0)]))
        def _():
            wgu_b[:, :ff] = wg_ref[...].astype(BF16)
            wgu_b[:, ff:] = wu_ref[...].astype(BF16)
            wd_b[...] = wd_ref[...].astype(BF16)

        x_hi, x_lo = _unpack_bf16_pair(xg[slot])
        half = x_hi.shape[1]
        gu = (jnp.dot(x_hi.astype(BF16), wgu_b[:half, :], preferred_element_type=F32)
              + jnp.dot(x_lo.astype(BF16), wgu_b[half:, :], preferred_element_type=F32))
        g = jnp.minimum(gu[:, :ff] + bg_ref[...], SWIGLU_LIMIT)
        u = jnp.clip(gu[:, ff:] + bu_ref[...], -SWIGLU_LIMIT, SWIGLU_LIMIT)
        act = (g * jax.nn.sigmoid(SWIGLU_ALPHA * g) * (u + 1.0)).astype(BF16)
        og[slot] = _pack_bf16_pair(jnp.dot(act, wd_b[...], preferred_element_type=F32) + bd_ref[...])

        def body(i, c):
            pltpu.make_async_copy(og.at[slot, pl.ds(i, 1), :], y_hbm.at[pl.ds(ydst_ref[0, i], 1), :],
                                  ssem.at[slot]).start()
            return c
        lax.fori_loop(0, tm, body, 0, unroll=8)

        @pl.when(r == nv - 1)
        def _():
            scatter_wait(slot)

            @pl.when(r >= 1)
            def _():
                scatter_wait(1 - slot)


def moe_experts(hp, plan, w_gate, b_gate, w_up, b_up, w_down, b_down, layer, *, tm):
    tok, ydst, tile_expert, n_valid = plan
    t = hp.shape[0]
    d = 2 * hp.shape[1]
    n_tiles = tok.shape[0]
    n_e, ff = w_gate.shape[1], w_gate.shape[3]
    bg = b_gate.reshape(b_gate.shape[0], n_e, 1, ff)
    bu = b_up.reshape(b_up.shape[0], n_e, 1, ff)
    bd = b_down.reshape(b_down.shape[0], n_e, 1, d)

    def expert(r, te, nv):
        return te[jnp.minimum(r, nv[0] - 1)]

    smem_blk = functools.partial(pl.BlockSpec, (None, 1, tm), memory_space=pltpu.SMEM)
    grid_spec = pltpu.PrefetchScalarGridSpec(
        num_scalar_prefetch=2,
        grid=(n_tiles,),
        in_specs=[
            smem_blk(lambda r, te, nv: (r, 0, 0)),
            smem_blk(lambda r, te, nv: (jnp.minimum(r + 1, n_tiles - 1), 0, 0)),
            smem_blk(lambda r, te, nv: (r, 0, 0)),
            pl.BlockSpec(memory_space=pl.ANY),
            pl.BlockSpec((None, None, d, ff), lambda r, te, nv: (layer, expert(r, te, nv), 0, 0)),
            pl.BlockSpec((None, None, d, ff), lambda r, te, nv: (layer, expert(r, te, nv), 0, 0)),
            pl.BlockSpec((None, None, ff, d), lambda r, te, nv: (layer, expert(r, te, nv), 0, 0)),
            pl.BlockSpec((None, None, 1, ff), lambda r, te, nv: (layer, expert(r, te, nv), 0, 0)),
            pl.BlockSpec((None, None, 1, ff), lambda r, te, nv: (layer, expert(r, te, nv), 0, 0)),
            pl.BlockSpec((None, None, 1, d), lambda r, te, nv: (layer, expert(r, te, nv), 0, 0)),
        ],
        out_specs=pl.BlockSpec(memory_space=pl.ANY),
        scratch_shapes=[
            pltpu.VMEM((2, tm, d // 2), jnp.uint32),
            pltpu.VMEM((2, tm, d // 2), jnp.uint32),
            pltpu.VMEM((d, 2 * ff), BF16),
            pltpu.VMEM((ff, d), BF16),
            pltpu.SemaphoreType.DMA((2,)),
            pltpu.SemaphoreType.DMA((2,)),
        ],
    )
    return pl.pallas_call(
        functools.partial(_moe_ffn_kernel, tm=tm, ff=ff),
        grid_spec=grid_spec,
        out_shape=jax.ShapeDtypeStruct((TOP_K * t + 2 * tm, d // 2), jnp.uint32),
        compiler_params=_params("arbitrary"),
        name="moe_experts",
    )(tile_expert, n_valid, tok, tok, ydst, hp, w_gate, w_up, w_down, bg, bu, bd)


def _moe_combine_kernel(y0_ref, y1_ref, y2_ref, y3_ref, w_ref, h_ref, g_ref, b_ref, of_ref, ob_ref, *, alpha, rows):
    tm = h_ref.shape[0]
    y_refs = (y0_ref, y1_ref, y2_ref, y3_ref)

    def body(rr, carry):
        sl = pl.ds(pl.multiple_of(rr * rows, rows), rows)
        w = w_ref[sl, :]
        hrow = h_ref[sl, :]
        half = hrow.shape[1] // 2
        z_hi = alpha * hrow[:, :half]
        z_lo = alpha * hrow[:, half:]
        for k, y_ref in enumerate(y_refs):
            y_hi, y_lo = _unpack_bf16_pair(y_ref[sl, :])
            z_hi = z_hi + w[:, k:k + 1] * y_hi
            z_lo = z_lo + w[:, k:k + 1] * y_lo
        z = jnp.concatenate([z_hi, z_lo], axis=1)
        mu = jnp.mean(z, axis=-1, keepdims=True)
        zc = z - mu
        var = jnp.mean(zc * zc, axis=-1, keepdims=True)
        y = zc * lax.rsqrt(var + LN_EPS) * g_ref[...] + b_ref[...]
        of_ref[sl, :] = y
        ob_ref[sl, :] = y.astype(BF16)
        return carry

    lax.fori_loop(0, tm // rows, body, 0)


def moe_combine_ln(y, topw, h, g, b, alpha, *, tm=128):
    t, d = h.shape
    nb = t // tm

    def y_spec(k):
        return pl.BlockSpec((tm, d // 2), lambda i: (k * nb + i, 0))

    row_spec = pl.BlockSpec((tm, d), lambda i: (i, 0))
    return pl.pallas_call(
        functools.partial(_moe_combine_kernel, alpha=alpha, rows=32),
        grid=(nb,),
        in_specs=[y_spec(0), y_spec(1), y_spec(2), y_spec(3),
                  pl.BlockSpec((tm, LANES), lambda i: (i, 0)), row_spec,
                  pl.BlockSpec((1, d), lambda i: (0, 0)), pl.BlockSpec((1, d), lambda i: (0, 0))],
        out_specs=[row_spec, row_spec],
        out_shape=[jax.ShapeDtypeStruct((t, d), F32), jax.ShapeDtypeStruct((t, d), BF16)],
        compiler_params=_params("arbitrary"),
        name="moe_combine_ln",
    )(y, y, y, y, topw, h, g, b)


NEG = -1e30
MIX_CHUNK = 256
BAND_BLOCK = 256
SB_TQ = 512
SB_TK = 256

def _head_norm_rows(x, gain):
    mu = jnp.mean(x, axis=-1, keepdims=True)
    xc = x - mu
    var = jnp.mean(xc * xc, axis=-1, keepdims=True)
    return xc * lax.rsqrt(var + HN_EPS) * gain


def _log_sigmoid(x):
    return jnp.minimum(x, 0.0) - jnp.log(1.0 + jnp.exp(-jnp.abs(x)))


def _split_bf16(x):
    hi = x.astype(BF16)
    lo = (x - hi.astype(F32)).astype(BF16)
    return hi, lo


def _lane_select(x, lane_idx):
    lane = lax.broadcasted_iota(jnp.int32, x.shape, 1)
    col = jnp.sum(jnp.where(lane == lane_idx, x, 0.0), axis=-1, keepdims=True)
    return jnp.broadcast_to(col, x.shape)


def _retention_body(q_ref, k_ref, v_ref, g_ref, cos_ref, sin_ref, lg_ref, gain_ref, o_ref,
                    qs_ref, ks_ref, st_ref, *, scale):
    seq = q_ref.shape[0]
    L = MIX_CHUNK
    half = HEAD_DIM // 2
    cos = cos_ref[...]
    sin = sin_ref[...]
    q = q_ref[...].astype(F32)
    k = k_ref[...].astype(F32)
    qs_ref[...] = (q * cos + pltpu.roll(q, half, 1) * sin).astype(BF16)
    ks_ref[...] = ((k * cos + pltpu.roll(k, half, 1) * sin) * scale).astype(BF16)

    lg = lg_ref[...]
    ri = lax.broadcasted_iota(jnp.int32, (L, L), 0)
    ci = lax.broadcasted_iota(jnp.int32, (L, L), 1)
    intra = jnp.where(ri >= ci, jnp.exp(lg * jnp.maximum(ri - ci, 0).astype(F32)), 0.0)
    rr = lax.broadcasted_iota(jnp.int32, (L, HEAD_DIM), 0).astype(F32)
    lg_d = lg[:, :HEAD_DIM]
    q_dec = jnp.exp(lg_d * (rr + 1.0))
    k_dec = jnp.exp(lg_d * (L - 1.0 - rr))
    c_dec = jnp.exp(lg_d * float(L))
    gain = gain_ref[...]
    st_ref[...] = jnp.zeros_like(st_ref)

    def chunk(c, carry):
        sl = pl.ds(pl.multiple_of(c * L, L), L)
        qc = qs_ref[sl, :]
        kc = ks_ref[sl, :]
        vc = v_ref[sl, :]
        state = st_ref[:, :HEAD_DIM]
        att = lax.dot_general(qc, kc, _NT, preferred_element_type=F32) * intra
        o = (jnp.dot(att.astype(BF16), vc, preferred_element_type=F32)
             + jnp.dot((qc.astype(F32) * q_dec).astype(BF16), state.astype(BF16), preferred_element_type=F32))
        st_ref[:, :HEAD_DIM] = state * c_dec + lax.dot_general(
            (kc.astype(F32) * k_dec).astype(BF16), vc, _TN, preferred_element_type=F32)
        gv = g_ref[sl, :].astype(F32)
        o_ref[sl, :] = (_head_norm_rows(o, gain) * (gv * jax.nn.sigmoid(gv))).astype(o_ref.dtype)
        return carry

    lax.fori_loop(0, seq // L, chunk, 0, unroll=2)


def _mlstm_body(u_ref, v_ref, og_ref, gates_ref, cw_ref, cb_ref, wq_ref, wk_ref, gain_ref, o_ref,
                qs_ref, ks_ref, st_ref, m_ref, *, head, heads, scale):
    seq = u_ref.shape[0]
    L = MIX_CHUNK
    d = HEAD_DIM
    x = u_ref[...].astype(F32)
    row = lax.broadcasted_iota(jnp.int32, x.shape, 0)
    cw = cw_ref[...]
    y = x * cw[CONV_K - 1:CONV_K, :] + cb_ref[...]
    for sh in range(1, CONV_K):
        xs = jnp.where(row >= sh, pltpu.roll(x, sh, 0), 0.0)
        y = y + xs * cw[CONV_K - 1 - sh:CONV_K - sh, :]
    ub = (y * jax.nn.sigmoid(y)).astype(BF16)
    qs_ref[...] = jnp.dot(ub, wq_ref[...].astype(BF16), preferred_element_type=F32).astype(BF16)
    ks_ref[...] = (jnp.dot(ub, wk_ref[...].astype(BF16), preferred_element_type=F32) * scale).astype(BF16)

    ri = lax.broadcasted_iota(jnp.int32, (L, L), 0)
    ci = lax.broadcasted_iota(jnp.int32, (L, L), 1)
    causal = ri >= ci
    tri = jnp.where(causal, 1.0, 0.0).astype(BF16)
    ones_v = jnp.ones((L, d), BF16)
    gain = gain_ref[...]
    st_ref[...] = jnp.zeros_like(st_ref)
    m_ref[...] = jnp.zeros_like(m_ref)

    def chunk(c, carry):
        sl = pl.ds(pl.multiple_of(c * L, L), L)
        qc = qs_ref[sl, :]
        kc = ks_ref[sl, :]
        v_ext = jnp.concatenate([v_ref[sl, :], ones_v], axis=1)
        gts = gates_ref[sl, :]
        ic = _lane_select(gts, head)
        lf = _log_sigmoid(_lane_select(gts, heads + head))
        lf_hi, lf_lo = _split_bf16(lf)
        bcum = (jnp.dot(tri, lf_hi, preferred_element_type=F32)
                + jnp.dot(tri, lf_lo, preferred_element_type=F32))
        m_st = m_ref[...]
        src = jnp.transpose(ic - bcum)[:1, :]
        bcum2 = jnp.concatenate([bcum, bcum], axis=1)
        log_intra = jnp.where(causal, bcum2 + src, NEG)
        m_intra = jnp.max(log_intra, axis=-1, keepdims=True)
        log_cross = bcum + m_st
        m_row = jnp.maximum(log_cross, m_intra)
        m_row2 = jnp.concatenate([m_row, m_row], axis=1)
        w_intra = jnp.exp(log_intra - m_row2)
        w_cross = jnp.exp(log_cross - m_row)
        w_cross2 = jnp.concatenate([w_cross, w_cross], axis=1)
        qk = lax.dot_general(qc, kc, _NT, preferred_element_type=F32) * w_intra
        state = st_ref[...]
        res = (jnp.dot(qk.astype(BF16), v_ext, preferred_element_type=F32)
               + w_cross2 * jnp.dot(qc, state.astype(BF16), preferred_element_type=F32))
        num = res[:, :d]
        den = res[:, d:]
        hh = num / jnp.maximum(jnp.abs(den), jnp.exp(-m_row))
        og = og_ref[sl, :].astype(F32)
        o_ref[sl, :] = _head_norm_rows(hh * jax.nn.sigmoid(og), gain).astype(o_ref.dtype)
        b_last = bcum[L - 1:L, :]
        log_state = b_last - bcum + ic
        m_new = jnp.maximum(b_last + m_st, jnp.max(log_state, axis=0, keepdims=True))
        decay = jnp.exp(b_last + m_st - m_new)
        kw = (kc.astype(F32) * jnp.exp(log_state - m_new)).astype(BF16)
        decay2 = jnp.concatenate([decay, decay], axis=1)
        st_ref[...] = decay2 * state + lax.dot_general(kw, v_ext, _TN, preferred_element_type=F32)
        m_ref[...] = m_new
        return carry

    lax.fori_loop(0, seq // L, chunk, 0, unroll=2)


def _mixer_ab_kernel(a0_ref, a1_ref, a2_ref, a3_ref, gates_ref, cos_ref, sin_ref, lg_ref, cw_ref, cb_ref,
                     wq_ref, wk_ref, gain_ref, o_ref, qs_ref, ks_ref, st_ref, m_ref, *, heads, scale):
    g = pl.program_id(1)

    @pl.when(g < heads)
    def _():
        _retention_body(a0_ref, a1_ref, a2_ref, a3_ref, cos_ref, sin_ref, lg_ref, gain_ref, o_ref,
                        qs_ref, ks_ref, st_ref, scale=scale)

    @pl.when(g >= heads)
    def _():
        _mlstm_body(a0_ref, a1_ref, a2_ref, gates_ref, cw_ref, cb_ref, wq_ref, wk_ref, gain_ref, o_ref,
                    qs_ref, ks_ref, st_ref, m_ref, head=g - heads, heads=heads, scale=scale)


def mixer_ab(z, gates, cos2, sin2, log_g, conv_w, conv_b, wq_m, wk_m, gain, layer, *, bsz, seq, heads):
    t = z.shape[0]
    d = HEAD_DIM
    L = MIX_CHUNK

    def col(base_ret, base_ml):
        def index(b, g):
            is_ml = g // heads
            return (b, (1 - is_ml) * (base_ret * heads + g) + is_ml * (base_ml * heads + g - heads))
        return pl.BlockSpec((seq, d), index)

    def ml_head(g):
        return jnp.maximum(g - heads, 0)

    conv_w4 = conv_w.reshape(conv_w.shape[0], CONV_K, heads, d).transpose(0, 2, 1, 3)
    conv_b4 = conv_b.reshape(conv_b.shape[0], heads, 1, d)
    gain4 = gain.reshape(2 * heads, 1, d)
    return pl.pallas_call(
        functools.partial(_mixer_ab_kernel, heads=heads, scale=d ** -0.5),
        grid=(bsz, 2 * heads),
        in_specs=[
            col(0, 4), col(1, 5), col(2, 6), col(3, 6),
            pl.BlockSpec((seq, LANES), lambda b, g: (b, 0)),
            pl.BlockSpec((seq, d), lambda b, g: (0, 0)),
            pl.BlockSpec((seq, d), lambda b, g: (0, 0)),
            pl.BlockSpec((None, 1, L), lambda b, g: (jnp.minimum(g, heads - 1), 0, 0)),
            pl.BlockSpec((None, None, CONV_K, d), lambda b, g: (layer, ml_head(g), 0, 0)),
            pl.BlockSpec((None, None, 1, d), lambda b, g: (layer, ml_head(g), 0, 0)),
            pl.BlockSpec((None, None, d, d), lambda b, g: (layer, ml_head(g), 0, 0)),
            pl.BlockSpec((None, None, d, d), lambda b, g: (layer, ml_head(g), 0, 0)),
            pl.BlockSpec((None, 1, d), lambda b, g: (g, 0, 0)),
        ],
        out_specs=pl.BlockSpec((seq, d), lambda b, g: (b, g)),
        out_shape=jax.ShapeDtypeStruct((t, 2 * heads * d), BF16),
        scratch_shapes=[
            pltpu.VMEM((seq, d), BF16),
            pltpu.VMEM((seq, d), BF16),
            pltpu.VMEM((d, 2 * d), F32),
            pltpu.VMEM((1, d), F32),
        ],
        compiler_params=_params("arbitrary", "arbitrary"),
        name="mixer_ab",
    )(z, z, z, z, gates, cos2, sin2, log_g, conv_w4, conv_b4, wq_m, wk_m, gain4)


def _band_body(q_ref, k_ref, v_ref, bias_ref, o_ref, *, scale):
    seq = q_ref.shape[0]
    bq = BAND_BLOCK
    n_back = bias_ref.shape[0]

    def block(i, carry):
        sl = pl.ds(pl.multiple_of(i * bq, bq), bq)
        q = q_ref[sl, :]
        scores, vals = [], []
        for dlt in range(n_back):
            ks = pl.ds(pl.multiple_of(jnp.maximum(i - dlt, 0) * bq, bq), bq)
            s = lax.dot_general(q, k_ref[ks, :], _NT, preferred_element_type=F32) * scale + bias_ref[dlt]
            scores.append(jnp.where(i - dlt >= 0, s, NEG))
            vals.append(v_ref[ks, :])
        m = jnp.max(scores[0], axis=-1, keepdims=True)
        for s in scores[1:]:
            m = jnp.maximum(m, jnp.max(s, axis=-1, keepdims=True))
        acc = jnp.zeros((bq, HEAD_DIM), F32)
        l = jnp.zeros((bq, 1), F32)
        for s, vv in zip(scores, vals):
            p = jnp.exp(s - m)
            l = l + jnp.sum(p, axis=-1, keepdims=True)
            acc = acc + jnp.dot(p.astype(BF16), vv, preferred_element_type=F32)
        o_ref[sl, :] = (acc / l).astype(o_ref.dtype)
        return carry

    lax.fori_loop(0, seq // bq, block, 0, unroll=2)


def _stick_breaking_body(q_ref, k_ref, v_ref, o_ref, acc_ref, r_ref, *, scale):
    seq = q_ref.shape[0]
    tq, tk = SB_TQ, SB_TK
    n_sub = tq // tk
    ri = lax.broadcasted_iota(jnp.int32, (tk, tk), 0)
    ci = lax.broadcasted_iota(jnp.int32, (tk, tk), 1)
    suffix = jnp.where(ri > ci, 1.0, 0.0).astype(BF16)
    strict = (lax.broadcasted_iota(jnp.int32, (tq, tq), 1)
              < lax.broadcasted_iota(jnp.int32, (tq, tq), 0))

    def one_group(q, k0, masked):
        kb = k_ref[pl.ds(k0, tq), :]
        vb = v_ref[pl.ds(k0, tq), :]
        z = lax.dot_general(q, kb, _NT, preferred_element_type=F32) * scale
        sp = jnp.maximum(z, 0.0) + jnp.log(1.0 + jnp.exp(-jnp.abs(z)))
        ls_pos = z - sp
        if masked:
            sp = jnp.where(strict, sp, 0.0)
        sp_b = sp.astype(BF16)
        r = r_ref[...]
        pieces = []
        for s in reversed(range(n_sub)):
            sl = slice(s * tk, (s + 1) * tk)
            between = jnp.dot(sp_b[:, sl], suffix, preferred_element_type=F32)
            pieces.append(ls_pos[:, sl] - between - jnp.concatenate([r] * (tk // LANES), axis=1))
            r = r + jnp.sum(sp[:, sl], axis=-1, keepdims=True)
        p = jnp.exp(jnp.concatenate(pieces[::-1], axis=1))
        if masked:
            p = jnp.where(strict, p, 0.0)
        acc_ref[...] += jnp.dot(p.astype(BF16), vb, preferred_element_type=F32)
        r_ref[...] = r

    for i in range(seq // tq):
        q0 = i * tq
        q = q_ref[pl.ds(q0, tq), :]
        acc_ref[...] = jnp.zeros_like(acc_ref)
        r_ref[...] = jnp.zeros_like(r_ref)
        one_group(q, q0, True)

        def below(jj, c, q=q, i=i):
            one_group(q, pl.multiple_of((i - 1 - jj) * tq, tq), False)
            return c

        lax.fori_loop(0, i, below, 0)
        o_ref[pl.ds(q0, tq), :] = acc_ref[...].astype(o_ref.dtype)


def _mixer_cd_kernel(q_ref, k_ref, v_ref, bias_ref, o_ref, acc_ref, r_ref, *, heads, scale):
    g = pl.program_id(1)

    @pl.when(g < heads)
    def _():
        _band_body(q_ref, k_ref, v_ref, bias_ref, o_ref, scale=scale)

    @pl.when(g >= heads)
    def _():
        _stick_breaking_body(q_ref, k_ref, v_ref, o_ref, acc_ref, r_ref, scale=scale)


def _band_bias_kernel(rrow_ref, o_ref):
    bq = BAND_BLOCK
    shift = CHUNK.bit_length() - 1
    qo = lax.broadcasted_iota(jnp.int32, (bq, bq), 0)
    ck = jnp.right_shift(lax.broadcasted_iota(jnp.int32, (bq, bq), 1), shift)
    for dlt in range(o_ref.shape[0]):
        x = jnp.broadcast_to(rrow_ref[dlt], (bq, 2 * bq))
        toep = pltpu.roll(x, 0, 1, stride=1, stride_axis=0)[:, :bq]
        cq = jnp.right_shift(qo + bq * dlt, shift)
        allowed = (ck <= cq) & (ck >= cq - PAST_CHUNKS)
        o_ref[dlt] = jnp.where(allowed, toep, NEG)


def band_bias_table(rel_bias):
    bq = BAND_BLOCK
    heads = rel_bias.shape[0]
    n_back = PAST_CHUNKS * CHUNK // bq + 1
    m = jnp.arange(2 * bq)
    key_minus_query = jnp.where(m < bq, m, m - 2 * bq)
    dist = bq * jnp.arange(n_back)[:, None] - key_minus_query[None, :]
    idx = jnp.clip(dist, -(CHUNK - 1), REL_MAX) + (CHUNK - 1)
    rrow = rel_bias.astype(F32)[:, idx].reshape(heads, n_back, 1, 2 * bq)
    return pl.pallas_call(
        _band_bias_kernel,
        grid=(heads,),
        in_specs=[pl.BlockSpec((None, n_back, 1, 2 * bq), lambda hh: (hh, 0, 0, 0))],
        out_specs=pl.BlockSpec((None, n_back, bq, bq), lambda hh: (hh, 0, 0, 0)),
        out_shape=jax.ShapeDtypeStruct((heads, n_back, bq, bq), F32),
        compiler_params=_params("arbitrary"),
        name="band_bias",
    )(rrow)


def mixer_cd(z, bias_tab, *, bsz, seq, heads):
    t = z.shape[0]
    d = HEAD_DIM

    def col(which):
        return pl.BlockSpec((seq, d), lambda b, g: (b, (3 * (g // heads) + which) * heads + g % heads))

    return pl.pallas_call(
        functools.partial(_mixer_cd_kernel, heads=heads, scale=d ** -0.5),
        grid=(bsz, 2 * heads),
        in_specs=[
            col(0), col(1), col(2),
            pl.BlockSpec((None,) + bias_tab.shape[1:], lambda b, g: (jnp.minimum(g, heads - 1), 0, 0, 0)),
        ],
        out_specs=pl.BlockSpec((seq, d), lambda b, g: (b, g)),
        out_shape=jax.ShapeDtypeStruct((t, 2 * heads * d), BF16),
        scratch_shapes=[pltpu.VMEM((SB_TQ, d), F32), pltpu.VMEM((SB_TQ, LANES), F32)],
        compiler_params=_params("arbitrary", "arbitrary"),
        name="mixer_cd",
    )(z, z, z, bias_tab)


def rope_tables(seq_len):
    pos = jnp.arange(seq_len, dtype=F32)
    inv_freq = ROPE_BASE ** (-jnp.arange(0, HEAD_DIM, 2, dtype=F32) / HEAD_DIM)
    ang = pos[:, None] * inv_freq[None, :]
    cos, sin = jnp.cos(ang), jnp.sin(ang)
    return jnp.concatenate([cos, cos], axis=1), jnp.concatenate([-sin, sin], axis=1)


def retention_log_decay(heads):
    lg = jnp.log1p(-jnp.exp2(-(5.0 + jnp.arange(heads, dtype=F32))))
    return jnp.broadcast_to(lg[:, None, None], (heads, 1, MIX_CHUNK))


def _pad_lanes(a, value=0.0):
    return jnp.pad(a, ((0, 0), (0, LANES - a.shape[1])), constant_values=value)


def kernel(x, mem, ab_w_in, ab_gate_b, ab_conv_w, ab_conv_b, ab_wq, ab_wk, ab_ret_norm_g, ab_mlstm_norm_g, ab_w_out, cd_w_in, cd_rel_bias, cd_w_out, mix_ln_g, mix_ln_b, xa_wq, xa_wkv, xa_wo, xa_ln_g, xa_ln_b, moe_router_w, moe_router_b, moe_w_gate, moe_b_gate, moe_w_up, moe_b_up, moe_w_down, moe_b_down, moe_ln_g, moe_ln_b):
    bsz, seq, d = x.shape
    depth = mix_ln_g.shape[0]
    heads = GROUP_HEADS
    alpha = (2.0 * depth) ** 0.25
    t = bsz * seq
    cos2, sin2 = rope_tables(seq)
    log_g = retention_log_decay(heads)
    h = x.reshape(t, d)
    hb = h.astype(BF16)
    memb = mem.reshape(-1, d).astype(BF16)
    n_ab = 7 * GROUP_WIDTH
    n_cd = 6 * GROUP_WIDTH
    ab_w_in_t = jnp.swapaxes(ab_w_in, 1, 2)
    for layer in range(depth):
        i = layer // 2
        if layer % 2 == 0:
            z = matmul_nt_stacked(hb, ab_w_in_t, i, n_ab, tm=1024, tn=512, out_dtype=BF16)
            gates = gates_nt(hb, ab_w_in_t, i, n_ab, 2 * heads, _pad_lanes(ab_gate_b[i][None, :]), tm=1024)
            gain = jnp.concatenate([ab_ret_norm_g[i], ab_mlstm_norm_g[i]])
            yb = mixer_ab(z, gates, cos2, sin2, log_g, ab_conv_w, ab_conv_b, ab_wq, ab_wk, gain, i,
                          bsz=bsz, seq=seq, heads=heads)
            w_out = ab_w_out[i].astype(BF16)
        else:
            z = matmul_stacked(hb, cd_w_in, i, n_cd, tm=1024, tn=512, out_dtype=BF16)
            yb = mixer_cd(z, band_bias_table(cd_rel_bias[i]), bsz=bsz, seq=seq, heads=heads)
            w_out = cd_w_out[i].astype(BF16)
        h, hb = matmul_ln(yb, w_out, h, mix_ln_g[layer][None, :], mix_ln_b[layer][None, :], alpha)

        kvb = matmul_stacked(memb, xa_wkv, layer, xa_wkv.shape[2], tm=memb.shape[0], tn=512, out_dtype=BF16)
        h, hp = xattn_ln(hb, h, xa_wq[layer].astype(BF16), kvb, xa_wo[layer].astype(BF16),
                         xa_ln_g[layer][None, :], xa_ln_b[layer][None, :], alpha, seq=seq)

        topw, topi, sel = moe_router(h, _pad_lanes(moe_router_w[layer]), _pad_lanes(moe_router_b[layer][None, :]))
        plan = moe_route_plan(topi, sel, tm=MOE_TM)
        y = moe_experts(hp, plan, moe_w_gate, moe_b_gate, moe_w_up, moe_b_up, moe_w_down, moe_b_down, layer, tm=MOE_TM)
        h, hb = moe_combine_ln(y, topw, h, moe_ln_g[layer][None, :], moe_ln_b[layer][None, :], alpha)
    return h.reshape(bsz, seq, d)
```

```python
import functools

import jax
import jax.numpy as jnp
from jax import lax
from jax.experimental import pallas as pl
from jax.experimental.pallas import tpu as pltpu

F32 = jnp.float32
BF16 = jnp.bfloat16

CHUNK = 64
HEAD_DIM = 128
GROUP_HEADS = 16
GROUP_WIDTH = GROUP_HEADS * HEAD_DIM
CONV_K = 4
PAST_CHUNKS = 8
REL_MAX = 2 * CHUNK
ROPE_BASE = 10000.0
XA_HEADS = 4
N_EXPERTS = 32
TOP_K = 4
SWIGLU_LIMIT = 7.0
SWIGLU_ALPHA = 1.702
LN_EPS = 1e-5
HN_EPS = 1e-6

LANES = 128
VMEM_LIMIT_BYTES = 58 * 1024 * 1024


def _params(*sem):
    return pltpu.CompilerParams(dimension_semantics=sem, vmem_limit_bytes=VMEM_LIMIT_BYTES)


def _mm_kernel(x_ref, w_ref, o_ref, wb_ref):
    @pl.when(pl.program_id(1) == 0)
    def _():
        wb_ref[...] = w_ref[...].astype(BF16)

    o_ref[...] = jnp.dot(x_ref[...], wb_ref[...], preferred_element_type=F32).astype(o_ref.dtype)


def matmul_stacked(x, w, layer, n_cols, *, tm, tn, out_dtype):
    m, k = x.shape
    return pl.pallas_call(
        _mm_kernel,
        grid=(n_cols // tn, m // tm),
        in_specs=[
            pl.BlockSpec((tm, k), lambda j, i: (i, 0)),
            pl.BlockSpec((None, k, tn), lambda j, i: (layer, 0, j)),
        ],
        out_specs=pl.BlockSpec((tm, tn), lambda j, i: (i, j)),
        out_shape=jax.ShapeDtypeStruct((m, n_cols), out_dtype),
        scratch_shapes=[pltpu.VMEM((k, tn), BF16)],
        compiler_params=_params("arbitrary", "arbitrary"),
        name="mm_in",
    )(x, w)


_NT = (((1,), (1,)), ((), ()))
_TN = (((0,), (0,)), ((), ()))


def _mm_nt_kernel(x_ref, w_ref, o_ref, wb_ref):
    @pl.when(pl.program_id(1) == 0)
    def _():
        wb_ref[...] = w_ref[...].astype(BF16)

    o_ref[...] = lax.dot_general(x_ref[...], wb_ref[...], _NT, preferred_element_type=F32).astype(o_ref.dtype)


def matmul_nt_stacked(x, wt, layer, n_cols, *, tm, tn, out_dtype):
    m, k = x.shape
    return pl.pallas_call(
        _mm_nt_kernel,
        grid=(n_cols // tn, m // tm),
        in_specs=[
            pl.BlockSpec((tm, k), lambda j, i: (i, 0)),
            pl.BlockSpec((None, tn, k), lambda j, i: (layer, j, 0)),
        ],
        out_specs=pl.BlockSpec((tm, tn), lambda j, i: (i, j)),
        out_shape=jax.ShapeDtypeStruct((m, n_cols), out_dtype),
        scratch_shapes=[pltpu.VMEM((tn, k), BF16)],
        compiler_params=_params("arbitrary", "arbitrary"),
        name="mm_in_nt",
    )(x, wt)


def _gates_kernel(x_ref, w_ref, b_ref, o_ref, wb_ref):
    @pl.when(pl.program_id(0) == 0)
    def _():
        wb_ref[...] = jnp.zeros_like(wb_ref)
        wb_ref[:w_ref.shape[0], :] = w_ref[...].astype(BF16)

    o_ref[...] = lax.dot_general(x_ref[...], wb_ref[...], _NT, preferred_element_type=F32) + b_ref[...]


def gates_nt(x, wt, layer, row0, n_rows, b_pad, *, tm):
    m, k = x.shape
    return pl.pallas_call(
        _gates_kernel,
        grid=(m // tm,),
        in_specs=[
            pl.BlockSpec((tm, k), lambda i: (i, 0)),
            pl.BlockSpec((None, n_rows, k), lambda i: (layer, row0 // n_rows, 0)),
            pl.BlockSpec((1, LANES), lambda i: (0, 0)),
        ],
        out_specs=pl.BlockSpec((tm, LANES), lambda i: (i, 0)),
        out_shape=jax.ShapeDtypeStruct((m, LANES), F32),
        scratch_shapes=[pltpu.VMEM((LANES, k), BF16)],
        compiler_params=_params("arbitrary"),
        name="gates",
    )(x, wt, b_pad)


def _pack_bf16_pair(y):
    half = y.shape[1] // 2
    hi = pltpu.bitcast(y[:, :half].astype(BF16).astype(F32), jnp.uint32)
    lo = pltpu.bitcast(y[:, half:].astype(BF16).astype(F32), jnp.uint32)
    return hi | (lo >> 16)


def _unpack_bf16_pair(w):
    return (pltpu.bitcast(w & jnp.uint32(0xFFFF0000), F32), pltpu.bitcast(w << 16, F32))


def _ln_rows(z_ref, g_ref, b_ref, of_ref, ob_ref, rows, packed=False):
    tm = z_ref.shape[0]

    def body(r, carry):
        sl = pl.ds(pl.multiple_of(r * rows, rows), rows)
        z = z_ref[sl, :]
        mu = jnp.mean(z, axis=-1, keepdims=True)
        zc = z - mu
        var = jnp.mean(zc * zc, axis=-1, keepdims=True)
        y = zc * lax.rsqrt(var + LN_EPS) * g_ref[...] + b_ref[...]
        of_ref[sl, :] = y
        ob_ref[sl, :] = _pack_bf16_pair(y) if packed else y.astype(BF16)
        return carry

    lax.fori_loop(0, tm // rows, body, 0, unroll=2)


def _mm_ln_kernel(x_ref, w_ref, h_ref, g_ref, b_ref, of_ref, ob_ref, *, nk, nj, tn, alpha):
    k = pl.program_id(1)
    j = pl.program_id(2)
    part = jnp.dot(x_ref[...], w_ref[...].astype(BF16), preferred_element_type=F32)
    for jj in range(nj):
        sl = slice(jj * tn, (jj + 1) * tn)

        @pl.when((j == jj) & (k == 0))
        def _():
            of_ref[:, sl] = alpha * h_ref[...] + part

        @pl.when((j == jj) & (k > 0))
        def _():
            of_ref[:, sl] += part

    @pl.when((k == nk - 1) & (j == nj - 1))
    def _():
        _ln_rows(of_ref, g_ref, b_ref, of_ref, ob_ref, 32)


def matmul_ln(x, w, h, g, b, alpha, *, tm=512, tn=512, tk=4096):
    m, kdim = x.shape
    n = h.shape[1]
    tk = min(tk, kdim)
    nk, nj = kdim // tk, n // tn
    in_specs = [
        pl.BlockSpec((tm, tk), lambda i, k, j: (i, k)),
        pl.BlockSpec((tk, tn), lambda i, k, j: (k, j)),
        pl.BlockSpec((tm, tn), lambda i, k, j: (i, j)),
        pl.BlockSpec((1, n), lambda i, k, j: (0, 0)),
        pl.BlockSpec((1, n), lambda i, k, j: (0, 0)),
    ]
    args = [x, w, h, g, b]
    return pl.pallas_call(
        functools.partial(_mm_ln_kernel, nk=nk, nj=nj, tn=tn, alpha=alpha),
        grid=(m // tm, nk, nj),
        in_specs=in_specs,
        out_specs=[
            pl.BlockSpec((tm, n), lambda i, k, j: (i, 0)),
            pl.BlockSpec((tm, n), lambda i, k, j: (i, 0)),
        ],
        out_shape=[jax.ShapeDtypeStruct((m, n), F32), jax.ShapeDtypeStruct((m, n), BF16)],
        compiler_params=_params("arbitrary", "arbitrary", "arbitrary"),
        name="mm_ln",
    )(*args)


def _xattn_kernel(hb_ref, h_ref, wq_ref, kv_ref, wo_ref, g_ref, b_ref, wrh_ref, wrl_ref, br_ref,
                  of_ref, ob_ref, topw_ref, topi_ref, sel_ref, z_ref, *, alpha, heads, hd):
    q = jnp.dot(hb_ref[...], wq_ref[...], preferred_element_type=F32)
    scale = hd ** -0.5
    outs = []
    for hh in range(heads):
        qh = (q[:, hh * hd:(hh + 1) * hd] * scale).astype(BF16)
        kh = kv_ref[:, hh * hd:(hh + 1) * hd]
        vh = kv_ref[:, (heads + hh) * hd:(heads + hh + 1) * hd]
        s = lax.dot_general(qh, kh, (((1,), (1,)), ((), ())), preferred_element_type=F32)
        s = s - jnp.max(s, axis=-1, keepdims=True)
        p = jnp.exp(s)
        l = jnp.sum(p, axis=-1, keepdims=True)
        o = jnp.dot(p.astype(BF16), vh, preferred_element_type=F32) / l
        outs.append(o.astype(BF16))
    o_all = jnp.concatenate(outs, axis=-1)
    z_ref[...] = alpha * h_ref[...] + jnp.dot(o_all, wo_ref[...], preferred_element_type=F32)
    _ln_rows(z_ref, g_ref, b_ref, of_ref, ob_ref, 32, packed=True)
    topw_ref[...], topi_ref[...], sel_ref[...] = _route_top4(of_ref[...], wrh_ref[...], wrl_ref[...], br_ref[...])


def xattn_ln(hb, h, wq_b, kv_b, wo_b, g, b, w_router, b_router, alpha, *, seq, tm=256):
    m, d = h.shape
    lane_spec = pl.BlockSpec((tm, LANES), lambda i: (i, 0))
    xw = wq_b.shape[1]
    n_mem = kv_b.shape[0] // (m // seq)
    per_b = seq // tm
    return pl.pallas_call(
        functools.partial(_xattn_kernel, alpha=alpha, heads=XA_HEADS, hd=xw // XA_HEADS),
        grid=(m // tm,),
        in_specs=[
            pl.BlockSpec((tm, d), lambda i: (i, 0)),
            pl.BlockSpec((tm, d), lambda i: (i, 0)),
            pl.BlockSpec((d, xw), lambda i: (0, 0)),
            pl.BlockSpec((n_mem, 2 * xw), lambda i: (i // per_b, 0)),
            pl.BlockSpec((xw, d), lambda i: (0, 0)),
            pl.BlockSpec((1, d), lambda i: (0, 0)),
            pl.BlockSpec((1, d), lambda i: (0, 0)),
            pl.BlockSpec((d, LANES), lambda i: (0, 0)),
            pl.BlockSpec((d, LANES), lambda i: (0, 0)),
            pl.BlockSpec((1, LANES), lambda i: (0, 0)),
        ],
        out_specs=[pl.BlockSpec((tm, d), lambda i: (i, 0)), pl.BlockSpec((tm, d // 2), lambda i: (i, 0)),
                   lane_spec, lane_spec, lane_spec],
        out_shape=[jax.ShapeDtypeStruct((m, d), F32), jax.ShapeDtypeStruct((m, d // 2), jnp.uint32),
                   jax.ShapeDtypeStruct((m, LANES), F32), jax.ShapeDtypeStruct((m, LANES), jnp.int32),
                   jax.ShapeDtypeStruct((m, LANES), F32)],
        scratch_shapes=[pltpu.VMEM((tm, d), F32)],
        compiler_params=_params("arbitrary"),
        name="xattn_ln",
    )(hb, h, wq_b, kv_b, wo_b, g, b, *_split_bf16(w_router), b_router)


def _route_top4(h, w_hi, w_lo, b):
    h_hi, h_lo = _split_bf16(h)
    logits = (jnp.dot(h_hi, w_hi, preferred_element_type=F32) + jnp.dot(h_lo, w_hi, preferred_element_type=F32)
              + jnp.dot(h_hi, w_lo, preferred_element_type=F32) + b)
    lane = lax.broadcasted_iota(jnp.int32, logits.shape, 1)
    neg = jnp.float32(-jnp.inf)
    masked = jnp.where(lane < N_EXPERTS, logits, neg)
    top_vals, top_idx = [], []
    sel = jnp.zeros_like(logits)
    for _ in range(TOP_K):
        mval = jnp.max(masked, axis=-1, keepdims=True)
        idx = jnp.min(jnp.where(masked == mval, lane, LANES), axis=-1, keepdims=True)
        hot = lane == idx
        top_vals.append(mval)
        top_idx.append(idx)
        sel = jnp.where(hot, 1.0, sel)
        masked = jnp.where(hot, neg, masked)
    exps = [jnp.exp(v - top_vals[0]) for v in top_vals]
    denom = exps[0]
    for e in exps[1:]:
        denom = denom + e
    topw = jnp.zeros_like(logits)
    topi = jnp.zeros(logits.shape, jnp.int32)
    for k in range(TOP_K):
        topw = jnp.where(lane == k, exps[k] / denom, topw)
        topi = jnp.where(lane == k, top_idx[k], topi)
    return topw, topi, sel


MOE_TM = 256


def moe_route_plan(topi, sel, *, tm):
    t = topi.shape[0]
    n_tiles = t * TOP_K // tm + N_EXPERTS
    rows = n_tiles * tm
    top4 = topi[:, :TOP_K]
    seli = sel[:, :N_EXPERTS].astype(jnp.int32)
    pos_incl = jnp.cumsum(seli, axis=0)
    counts = pos_incl[-1]
    pos = pos_incl - seli
    tiles_e = (counts + tm - 1) // tm
    tile_end = jnp.cumsum(tiles_e)
    row_off = (tile_end - tiles_e) * tm
    dest = (row_off[top4] + jnp.take_along_axis(pos, top4, axis=1)).reshape(-1)
    slot_rows = (jnp.arange(t, dtype=jnp.int32)[:, None] + t * jnp.arange(TOP_K, dtype=jnp.int32)[None, :]).reshape(-1)
    ydst = (TOP_K * t + jnp.arange(rows, dtype=jnp.int32) % (2 * tm)).at[dest].set(slot_rows, unique_indices=True)
    tok = ydst % t
    tile_ids = jnp.arange(n_tiles, dtype=jnp.int32)
    tile_expert = jnp.minimum(jnp.sum(tile_end[None, :] <= tile_ids[:, None], axis=1), N_EXPERTS - 1)
    n_valid = tile_end[-1:].astype(jnp.int32)
    return (tok.reshape(n_tiles, 1, tm), ydst.reshape(n_tiles, 1, tm), tile_expert.astype(jnp.int32), n_valid)


def _moe_ffn_kernel(te_ref, nv_ref, tok_ref, tokn_ref, ydst_ref, h_hbm, wg_ref, wu_ref, wd_ref,
                    bg_ref, bu_ref, bd_ref, y_hbm, xg, og, wgu_b, wd_b, gsem, ssem, *, tm, ff):
    r = pl.program_id(0)
    nv = nv_ref[0]
    slot = lax.rem(r, 2)

    def row_gather(idx_ref, s):
        def body(i, c):
            pltpu.make_async_copy(h_hbm.at[pl.ds(idx_ref[0, i], 1), :], xg.at[s, pl.ds(i, 1), :], gsem.at[s]).start()
            return c
        lax.fori_loop(0, tm, body, 0, unroll=8)

    def gather_wait(s):
        pltpu.make_async_copy(h_hbm.at[pl.ds(0, tm), :], xg.at[s], gsem.at[s]).wait()

    def scatter_wait(s):
        pltpu.make_async_copy(og.at[s], y_hbm.at[pl.ds(0, tm), :], ssem.at[s]).wait()

    @pl.when(r == 0)
    def _():
        row_gather(tok_ref, 0)
        og[1] = jnp.zeros(og.shape[1:], og.dtype)
        base = y_hbm.shape[0] - 2 * tm
        for part in range(2):
            fill = pltpu.make_async_copy(og.at[1], y_hbm.at[pl.ds(base + part * tm, tm), :], ssem.at[1])
            fill.start()
            fill.wait()

    @pl.when(r + 1 < nv)
    def _():
        row_gather(tokn_ref, 1 - slot)

    @pl.when(r < nv)
    def _():
        gather_wait(slot)

        @pl.when(r >= 2)
        def _():
            scatter_wait(slot)

        @pl.when((r == 0) | (te_ref[r] != te_ref[jnp.maximum(r - 1, 0)]))
        def _():
            wgu_b[:, :ff] = wg_ref[...].astype(BF16)
            wgu_b[:, ff:] = wu_ref[...].astype(BF16)
            wd_b[...] = wd_ref[...].astype(BF16)

        x_hi, x_lo = _unpack_bf16_pair(xg[slot])
        half = x_hi.shape[1]
        gu = (jnp.dot(x_hi.astype(BF16), wgu_b[:half, :], preferred_element_type=F32)
              + jnp.dot(x_lo.astype(BF16), wgu_b[half:, :], preferred_element_type=F32))
        g = jnp.minimum(gu[:, :ff] + bg_ref[...], SWIGLU_LIMIT)
        u = jnp.clip(gu[:, ff:] + bu_ref[...], -SWIGLU_LIMIT, SWIGLU_LIMIT)
        act = (g * jax.nn.sigmoid(SWIGLU_ALPHA * g) * (u + 1.0)).astype(BF16)
        og[slot] = _pack_bf16_pair(jnp.dot(act, wd_b[...], preferred_element_type=F32) + bd_ref[...])

        def body(i, c):
            pltpu.make_async_copy(og.at[slot, pl.ds(i, 1), :], y_hbm.at[pl.ds(ydst_ref[0, i], 1), :],
                                  ssem.at[slot]).start()
            return c
        lax.fori_loop(0, tm, body, 0, unroll=8)

        @pl.when(r == nv - 1)
        def _():
            scatter_wait(slot)

            @pl.when(r >= 1)
            def _():
                scatter_wait(1 - slot)


def moe_experts(hp, plan, w_gate, b_gate, w_up, b_up, w_down, b_down, layer, *, tm):
    tok, ydst, tile_expert, n_valid = plan
    t = hp.shape[0]
    d = 2 * hp.shape[1]
    n_tiles = tok.shape[0]
    n_e, ff = w_gate.shape[1], w_gate.shape[3]
    bg = b_gate.reshape(b_gate.shape[0], n_e, 1, ff)
    bu = b_up.reshape(b_up.shape[0], n_e, 1, ff)
    bd = b_down.reshape(b_down.shape[0], n_e, 1, d)

    def expert(r, te, nv):
        return te[jnp.minimum(r, nv[0] - 1)]

    smem_blk = functools.partial(pl.BlockSpec, (None, 1, tm), memory_space=pltpu.SMEM)
    grid_spec = pltpu.PrefetchScalarGridSpec(
        num_scalar_prefetch=2,
        grid=(n_tiles,),
        in_specs=[
            smem_blk(lambda r, te, nv: (r, 0, 0)),
            smem_blk(lambda r, te, nv: (jnp.minimum(r + 1, n_tiles - 1), 0, 0)),
            smem_blk(lambda r, te, nv: (r, 0, 0)),
            pl.BlockSpec(memory_space=pl.ANY),
            pl.BlockSpec((None, None, d, ff), lambda r, te, nv: (layer, expert(r, te, nv), 0, 0)),
            pl.BlockSpec((None, None, d, ff), lambda r, te, nv: (layer, expert(r, te, nv), 0, 0)),
            pl.BlockSpec((None, None, ff, d), lambda r, te, nv: (layer, expert(r, te, nv), 0, 0)),
            pl.BlockSpec((None, None, 1, ff), lambda r, te, nv: (layer, expert(r, te, nv), 0, 0)),
            pl.BlockSpec((None, None, 1, ff), lambda r, te, nv: (layer, expert(r, te, nv), 0, 0)),
            pl.BlockSpec((None, None, 1, d), lambda r, te, nv: (layer, expert(r, te, nv), 0, 0)),
        ],
        out_specs=pl.BlockSpec(memory_space=pl.ANY),
        scratch_shapes=[
            pltpu.VMEM((2, tm, d // 2), jnp.uint32),
            pltpu.VMEM((2, tm, d // 2), jnp.uint32),
            pltpu.VMEM((d, 2 * ff), BF16),
            pltpu.VMEM((ff, d), BF16),
            pltpu.SemaphoreType.DMA((2,)),
            pltpu.SemaphoreType.DMA((2,)),
        ],
    )
    return pl.pallas_call(
        functools.partial(_moe_ffn_kernel, tm=tm, ff=ff),
        grid_spec=grid_spec,
        out_shape=jax.ShapeDtypeStruct((TOP_K * t + 2 * tm, d // 2), jnp.uint32),
        compiler_params=_params("arbitrary"),
        name="moe_experts",
    )(tile_expert, n_valid, tok, tok, ydst, hp, w_gate, w_up, w_down, bg, bu, bd)


def _moe_combine_kernel(y0_ref, y1_ref, y2_ref, y3_ref, w_ref, h_ref, g_ref, b_ref, of_ref, ob_ref, *, alpha, rows):
    tm = h_ref.shape[0]
    y_refs = (y0_ref, y1_ref, y2_ref, y3_ref)

    def body(rr, carry):
        sl = pl.ds(pl.multiple_of(rr * rows, rows), rows)
        w = w_ref[sl, :]
        hrow = h_ref[sl, :]
        half = hrow.shape[1] // 2
        z_hi = alpha * hrow[:, :half]
        z_lo = alpha * hrow[:, half:]
        for k, y_ref in enumerate(y_refs):
            y_hi, y_lo = _unpack_bf16_pair(y_ref[sl, :])
            z_hi = z_hi + w[:, k:k + 1] * y_hi
            z_lo = z_lo + w[:, k:k + 1] * y_lo
        z = jnp.concatenate([z_hi, z_lo], axis=1)
        mu = jnp.mean(z, axis=-1, keepdims=True)
        zc = z - mu
        var = jnp.mean(zc * zc, axis=-1, keepdims=True)
        y = zc * lax.rsqrt(var + LN_EPS) * g_ref[...] + b_ref[...]
        of_ref[sl, :] = y
        ob_ref[sl, :] = y.astype(BF16)
        return carry

    lax.fori_loop(0, tm // rows, body, 0)


def moe_combine_ln(y, topw, h, g, b, alpha, *, tm=128):
    t, d = h.shape
    nb = t // tm

    def y_spec(k):
        return pl.BlockSpec((tm, d // 2), lambda i: (k * nb + i, 0))

    row_spec = pl.BlockSpec((tm, d), lambda i: (i, 0))
    return pl.pallas_call(
        functools.partial(_moe_combine_kernel, alpha=alpha, rows=32),
        grid=(nb,),
        in_specs=[y_spec(0), y_spec(1), y_spec(2), y_spec(3),
                  pl.BlockSpec((tm, LANES), lambda i: (i, 0)), row_spec,
                  pl.BlockSpec((1, d), lambda i: (0, 0)), pl.BlockSpec((1, d), lambda i: (0, 0))],
        out_specs=[row_spec, row_spec],
        out_shape=[jax.ShapeDtypeStruct((t, d), F32), jax.ShapeDtypeStruct((t, d), BF16)],
        compiler_params=_params("arbitrary"),
        name="moe_combine_ln",
    )(y, y, y, y, topw, h, g, b)


NEG = -1e30
MIX_CHUNK = 256
BAND_BLOCK = 256
SB_TQ = 512
SB_TK = 256

def _head_norm_rows(x, gain):
    mu = jnp.mean(x, axis=-1, keepdims=True)
    xc = x - mu
    var = jnp.mean(xc * xc, axis=-1, keepdims=True)
    return xc * lax.rsqrt(var + HN_EPS) * gain


def _log_sigmoid(x):
    return jnp.minimum(x, 0.0) - jnp.log(1.0 + jnp.exp(-jnp.abs(x)))


def _split_bf16(x):
    hi = x.astype(BF16)
    lo = (x - hi.astype(F32)).astype(BF16)
    return hi, lo


def _lane_select(x, lane_idx):
    lane = lax.broadcasted_iota(jnp.int32, x.shape, 1)
    col = jnp.sum(jnp.where(lane == lane_idx, x, 0.0), axis=-1, keepdims=True)
    return jnp.broadcast_to(col, x.shape)


def _retention_body(q_ref, k_ref, v_ref, g_ref, cos_ref, sin_ref, lg_ref, gain_ref, o_ref,
                    qs_ref, ks_ref, st_ref, *, scale):
    seq = q_ref.shape[0]
    L = MIX_CHUNK
    half = HEAD_DIM // 2
    cos = cos_ref[...]
    sin = sin_ref[...]
    q = q_ref[...].astype(F32)
    k = k_ref[...].astype(F32)
    qs_ref[...] = (q * cos + pltpu.roll(q, half, 1) * sin).astype(BF16)
    ks_ref[...] = ((k * cos + pltpu.roll(k, half, 1) * sin) * scale).astype(BF16)

    lg = lg_ref[...]
    ri = lax.broadcasted_iota(jnp.int32, (L, L), 0)
    ci = lax.broadcasted_iota(jnp.int32, (L, L), 1)
    intra = jnp.where(ri >= ci, jnp.exp(lg * jnp.maximum(ri - ci, 0).astype(F32)), 0.0)
    rr = lax.broadcasted_iota(jnp.int32, (L, HEAD_DIM), 0).astype(F32)
    lg_d = lg[:, :HEAD_DIM]
    q_dec = jnp.exp(lg_d * (rr + 1.0))
    k_dec = jnp.exp(lg_d * (L - 1.0 - rr))
    c_dec = jnp.exp(lg_d * float(L))
    gain = gain_ref[...]
    st_ref[...] = jnp.zeros_like(st_ref)

    def chunk(c, carry):
        sl = pl.ds(pl.multiple_of(c * L, L), L)
        qc = qs_ref[sl, :]
        kc = ks_ref[sl, :]
        vc = v_ref[sl, :]
        state = st_ref[:, :HEAD_DIM]
        att = lax.dot_general(qc, kc, _NT, preferred_element_type=F32) * intra
        o = (jnp.dot(att.astype(BF16), vc, preferred_element_type=F32)
             + jnp.dot((qc.astype(F32) * q_dec).astype(BF16), state.astype(BF16), preferred_element_type=F32))
        st_ref[:, :HEAD_DIM] = state * c_dec + lax.dot_general(
            (kc.astype(F32) * k_dec).astype(BF16), vc, _TN, preferred_element_type=F32)
        gv = g_ref[sl, :].astype(F32)
        o_ref[sl, :] = (_head_norm_rows(o, gain) * (gv * jax.nn.sigmoid(gv))).astype(o_ref.dtype)
        return carry

    lax.fori_loop(0, seq // L, chunk, 0, unroll=True)


def _mlstm_body(u_ref, v_ref, og_ref, gates_ref, cw_ref, cb_ref, wq_ref, wk_ref, gain_ref, o_ref,
                qs_ref, ks_ref, st_ref, m_ref, *, head, heads, scale):
    seq = u_ref.shape[0]
    L = MIX_CHUNK
    d = HEAD_DIM
    x = u_ref[...].astype(F32)
    row = lax.broadcasted_iota(jnp.int32, x.shape, 0)
    cw = cw_ref[...]
    y = x * cw[CONV_K - 1:CONV_K, :] + cb_ref[...]
    for sh in range(1, CONV_K):
        xs = jnp.where(row >= sh, pltpu.roll(x, sh, 0), 0.0)
        y = y + xs * cw[CONV_K - 1 - sh:CONV_K - sh, :]
    ub = (y * jax.nn.sigmoid(y)).astype(BF16)
    qs_ref[...] = jnp.dot(ub, wq_ref[...].astype(BF16), preferred_element_type=F32).astype(BF16)
    ks_ref[...] = (jnp.dot(ub, wk_ref[...].astype(BF16), preferred_element_type=F32) * scale).astype(BF16)

    ri = lax.broadcasted_iota(jnp.int32, (L, L), 0)
    ci = lax.broadcasted_iota(jnp.int32, (L, L), 1)
    causal = ri >= ci
    tri = jnp.where(causal, 1.0, 0.0).astype(BF16)
    ones_v = jnp.ones((L, d), BF16)
    gain = gain_ref[...]
    st_ref[...] = jnp.zeros_like(st_ref)
    m_ref[...] = jnp.zeros_like(m_ref)

    def chunk(c, carry):
        sl = pl.ds(pl.multiple_of(c * L, L), L)
        qc = qs_ref[sl, :]
        kc = ks_ref[sl, :]
        v_ext = jnp.concatenate([v_ref[sl, :], ones_v], axis=1)
        gts = gates_ref[sl, :]
        ic = _lane_select(gts, head)
        lf = _log_sigmoid(_lane_select(gts, heads + head))
        lf_hi, lf_lo = _split_bf16(lf)
        bcum = (jnp.dot(tri, lf_hi, preferred_element_type=F32)
                + jnp.dot(tri, lf_lo, preferred_element_type=F32))
        m_st = m_ref[...]
        src = jnp.transpose(ic - bcum)[:1, :]
        bcum2 = jnp.concatenate([bcum, bcum], axis=1)
        log_intra = jnp.where(causal, bcum2 + src, NEG)
        m_intra = jnp.max(log_intra, axis=-1, keepdims=True)
        log_cross = bcum + m_st
        m_row = jnp.maximum(log_cross, m_intra)
        m_row2 = jnp.concatenate([m_row, m_row], axis=1)
        w_intra = jnp.exp(log_intra - m_row2)
        w_cross = jnp.exp(log_cross - m_row)
        w_cross2 = jnp.concatenate([w_cross, w_cross], axis=1)
        qk = lax.dot_general(qc, kc, _NT, preferred_element_type=F32) * w_intra
        state = st_ref[...]
        res = (jnp.dot(qk.astype(BF16), v_ext, preferred_element_type=F32)
               + w_cross2 * jnp.dot(qc, state.astype(BF16), preferred_element_type=F32))
        num = res[:, :d]
        den = res[:, d:]
        hh = num / jnp.maximum(jnp.abs(den), jnp.exp(-m_row))
        og = og_ref[sl, :].astype(F32)
        o_ref[sl, :] = _head_norm_rows(hh * jax.nn.sigmoid(og), gain).astype(o_ref.dtype)
        b_last = bcum[L - 1:L, :]
        log_state = b_last - bcum + ic
        m_new = jnp.maximum(b_last + m_st, jnp.max(log_state, axis=0, keepdims=True))
        decay = jnp.exp(b_last + m_st - m_new)
        kw = (kc.astype(F32) * jnp.exp(log_state - m_new)).astype(BF16)
        decay2 = jnp.concatenate([decay, decay], axis=1)
        st_ref[...] = decay2 * state + lax.dot_general(kw, v_ext, _TN, preferred_element_type=F32)
        m_ref[...] = m_new
        return carry

    lax.fori_loop(0, seq // L, chunk, 0, unroll=True)


def _mixer_ab_kernel(a0_ref, a1_ref, a2_ref, a3_ref, gates_ref, cos_ref, sin_ref, lg_ref, cw_ref, cb_ref,
                     wq_ref, wk_ref, gain_ref, o_ref, qs_ref, ks_ref, st_ref, m_ref, *, heads, scale):
    g = pl.program_id(1)

    @pl.when(g < heads)
    def _():
        _retention_body(a0_ref, a1_ref, a2_ref, a3_ref, cos_ref, sin_ref, lg_ref, gain_ref, o_ref,
                        qs_ref, ks_ref, st_ref, scale=scale)

    @pl.when(g >= heads)
    def _():
        _mlstm_body(a0_ref, a1_ref, a2_ref, gates_ref, cw_ref, cb_ref, wq_ref, wk_ref, gain_ref, o_ref,
                    qs_ref, ks_ref, st_ref, m_ref, head=g - heads, heads=heads, scale=scale)


def mixer_ab(z, gates, cos2, sin2, log_g, conv_w, conv_b, wq_m, wk_m, gain, layer, *, bsz, seq, heads):
    t = z.shape[0]
    d = HEAD_DIM
    L = MIX_CHUNK

    def col(base_ret, base_ml):
        def index(b, g):
            is_ml = g // heads
            return (b, (1 - is_ml) * (base_ret * heads + g) + is_ml * (base_ml * heads + g - heads))
        return pl.BlockSpec((seq, d), index)

    def ml_head(g):
        return jnp.maximum(g - heads, 0)

    conv_w4 = conv_w.reshape(conv_w.shape[0], CONV_K, heads, d).transpose(0, 2, 1, 3)
    conv_b4 = conv_b.reshape(conv_b.shape[0], heads, 1, d)
    gain4 = gain.reshape(2 * heads, 1, d)
    return pl.pallas_call(
        functools.partial(_mixer_ab_kernel, heads=heads, scale=d ** -0.5),
        grid=(bsz, 2 * heads),
        in_specs=[
            col(0, 4), col(1, 5), col(2, 6), col(3, 6),
            pl.BlockSpec((seq, LANES), lambda b, g: (b, 0)),
            pl.BlockSpec((seq, d), lambda b, g: (0, 0)),
            pl.BlockSpec((seq, d), lambda b, g: (0, 0)),
            pl.BlockSpec((None, 1, L), lambda b, g: (jnp.minimum(g, heads - 1), 0, 0)),
            pl.BlockSpec((None, None, CONV_K, d), lambda b, g: (layer, ml_head(g), 0, 0)),
            pl.BlockSpec((None, None, 1, d), lambda b, g: (layer, ml_head(g), 0, 0)),
            pl.BlockSpec((None, None, d, d), lambda b, g: (layer, ml_head(g), 0, 0)),
            pl.BlockSpec((None, None, d, d), lambda b, g: (layer, ml_head(g), 0, 0)),
            pl.BlockSpec((None, 1, d), lambda b, g: (g, 0, 0)),
        ],
        out_specs=pl.BlockSpec((seq, d), lambda b, g: (b, g)),
        out_shape=jax.ShapeDtypeStruct((t, 2 * heads * d), BF16),
        scratch_shapes=[
            pltpu.VMEM((seq, d), BF16),
            pltpu.VMEM((seq, d), BF16),
            pltpu.VMEM((d, 2 * d), F32),
            pltpu.VMEM((1, d), F32),
        ],
        compiler_params=_params("arbitrary", "arbitrary"),
        name="mixer_ab",
    )(z, z, z, z, gates, cos2, sin2, log_g, conv_w4, conv_b4, wq_m, wk_m, gain4)


def _band_body(q_ref, k_ref, v_ref, bias_ref, o_ref, *, scale):
    seq = q_ref.shape[0]
    bq = BAND_BLOCK
    n_back = bias_ref.shape[0]

    def block(i, carry):
        sl = pl.ds(pl.multiple_of(i * bq, bq), bq)
        q = q_ref[sl, :]
        scores, vals = [], []
        for dlt in range(n_back):
            ks = pl.ds(pl.multiple_of(jnp.maximum(i - dlt, 0) * bq, bq), bq)
            s = lax.dot_general(q, k_ref[ks, :], _NT, preferred_element_type=F32) * scale + bias_ref[dlt]
            scores.append(jnp.where(i - dlt >= 0, s, NEG))
            vals.append(v_ref[ks, :])
        m = jnp.max(scores[0], axis=-1, keepdims=True)
        for s in scores[1:]:
            m = jnp.maximum(m, jnp.max(s, axis=-1, keepdims=True))
        acc = jnp.zeros((bq, HEAD_DIM), F32)
        l = jnp.zeros((bq, 1), F32)
        for s, vv in zip(scores, vals):
            p = jnp.exp(s - m)
            l = l + jnp.sum(p, axis=-1, keepdims=True)
            acc = acc + jnp.dot(p.astype(BF16), vv, preferred_element_type=F32)
        o_ref[sl, :] = (acc / l).astype(o_ref.dtype)
        return carry

    lax.fori_loop(0, seq // bq, block, 0, unroll=True)


def _stick_breaking_body(q_ref, k_ref, v_ref, o_ref, acc_ref, r_ref, *, scale):
    seq = q_ref.shape[0]
    tq, tk = SB_TQ, SB_TK
    n_sub = tq // tk
    ri = lax.broadcasted_iota(jnp.int32, (tk, tk), 0)
    ci = lax.broadcasted_iota(jnp.int32, (tk, tk), 1)
    suffix = jnp.where(ri > ci, 1.0, 0.0).astype(BF16)
    strict = (lax.broadcasted_iota(jnp.int32, (tq, tq), 1)
              < lax.broadcasted_iota(jnp.int32, (tq, tq), 0))

    def one_group(q, k0, masked):
        kb = k_ref[pl.ds(k0, tq), :]
        vb = v_ref[pl.ds(k0, tq), :]
        z = lax.dot_general(q, kb, _NT, preferred_element_type=F32) * scale
        sp = jnp.maximum(z, 0.0) + jnp.log(1.0 + jnp.exp(-jnp.abs(z)))
        ls_pos = z - sp
        if masked:
            sp = jnp.where(strict, sp, 0.0)
        sp_b = sp.astype(BF16)
        r = r_ref[...]
        pieces = []
        for s in reversed(range(n_sub)):
            sl = slice(s * tk, (s + 1) * tk)
            between = jnp.dot(sp_b[:, sl], suffix, preferred_element_type=F32)
            pieces.append(ls_pos[:, sl] - between - jnp.concatenate([r] * (tk // LANES), axis=1))
            r = r + jnp.sum(sp[:, sl], axis=-1, keepdims=True)
        p = jnp.exp(jnp.concatenate(pieces[::-1], axis=1))
        if masked:
            p = jnp.where(strict, p, 0.0)
        acc_ref[...] += jnp.dot(p.astype(BF16), vb, preferred_element_type=F32)
        r_ref[...] = r

    for i in range(seq // tq):
        q0 = i * tq
        q = q_ref[pl.ds(q0, tq), :]
        acc_ref[...] = jnp.zeros_like(acc_ref)
        r_ref[...] = jnp.zeros_like(r_ref)
        one_group(q, q0, True)

        def below(jj, c, q=q, i=i):
            one_group(q, pl.multiple_of((i - 1 - jj) * tq, tq), False)
            return c

        lax.fori_loop(0, i, below, 0, unroll=True)
        o_ref[pl.ds(q0, tq), :] = acc_ref[...].astype(o_ref.dtype)


def _mixer_cd_kernel(q_ref, k_ref, v_ref, bias_ref, o_ref, acc_ref, r_ref, *, heads, scale):
    g = pl.program_id(1)

    @pl.when(g < heads)
    def _():
        _band_body(q_ref, k_ref, v_ref, bias_ref, o_ref, scale=scale)

    @pl.when(g >= heads)
    def _():
        _stick_breaking_body(q_ref, k_ref, v_ref, o_ref, acc_ref, r_ref, scale=scale)


def _band_bias_kernel(rrow_ref, o_ref):
    bq = BAND_BLOCK
    shift = CHUNK.bit_length() - 1
    qo = lax.broadcasted_iota(jnp.int32, (bq, bq), 0)
    ck = jnp.right_shift(lax.broadcasted_iota(jnp.int32, (bq, bq), 1), shift)
    for dlt in range(o_ref.shape[0]):
        x = jnp.broadcast_to(rrow_ref[dlt], (bq, 2 * bq))
        toep = pltpu.roll(x, 0, 1, stride=1, stride_axis=0)[:, :bq]
        cq = jnp.right_shift(qo + bq * dlt, shift)
        allowed = (ck <= cq) & (ck >= cq - PAST_CHUNKS)
        o_ref[dlt] = jnp.where(allowed, toep, NEG)


def band_bias_table(rel_bias):
    bq = BAND_BLOCK
    heads = rel_bias.shape[0]
    n_back = PAST_CHUNKS * CHUNK // bq + 1
    m = jnp.arange(2 * bq)
    key_minus_query = jnp.where(m < bq, m, m - 2 * bq)
    dist = bq * jnp.arange(n_back)[:, None] - key_minus_query[None, :]
    idx = jnp.clip(dist, -(CHUNK - 1), REL_MAX) + (CHUNK - 1)
    rrow = rel_bias.astype(F32)[:, idx].reshape(heads, n_back, 1, 2 * bq)
    return pl.pallas_call(
        _band_bias_kernel,
        grid=(heads,),
        in_specs=[pl.BlockSpec((None, n_back, 1, 2 * bq), lambda hh: (hh, 0, 0, 0))],
        out_specs=pl.BlockSpec((None, n_back, bq, bq), lambda hh: (hh, 0, 0, 0)),
        out_shape=jax.ShapeDtypeStruct((heads, n_back, bq, bq), F32),
        compiler_params=_params("arbitrary"),
        name="band_bias",
    )(rrow)


def mixer_cd(z, bias_tab, *, bsz, seq, heads):
    t = z.shape[0]
    d = HEAD_DIM

    def col(which):
        return pl.BlockSpec((seq, d), lambda b, g: (b, (3 * (g // heads) + which) * heads + g % heads))

    return pl.pallas_call(
        functools.partial(_mixer_cd_kernel, heads=heads, scale=d ** -0.5),
        grid=(bsz, 2 * heads),
        in_specs=[
            col(0), col(1), col(2),
            pl.BlockSpec((None,) + bias_tab.shape[1:], lambda b, g: (jnp.minimum(g, heads - 1), 0, 0, 0)),
        ],
        out_specs=pl.BlockSpec((seq, d), lambda b, g: (b, g)),
        out_shape=jax.ShapeDtypeStruct((t, 2 * heads * d), BF16),
        scratch_shapes=[pltpu.VMEM((SB_TQ, d), F32), pltpu.VMEM((SB_TQ, LANES), F32)],
        compiler_params=_params("arbitrary", "arbitrary"),
        name="mixer_cd",
    )(z, z, z, bias_tab)


def rope_tables(seq_len):
    pos = jnp.arange(seq_len, dtype=F32)
    inv_freq = ROPE_BASE ** (-jnp.arange(0, HEAD_DIM, 2, dtype=F32) / HEAD_DIM)
    ang = pos[:, None] * inv_freq[None, :]
    cos, sin = jnp.cos(ang), jnp.sin(ang)
    return jnp.concatenate([cos, cos], axis=1), jnp.concatenate([-sin, sin], axis=1)


def retention_log_decay(heads):
    lg = jnp.log1p(-jnp.exp2(-(5.0 + jnp.arange(heads, dtype=F32))))
    return jnp.broadcast_to(lg[:, None, None], (heads, 1, MIX_CHUNK))


def _pad_lanes(a, value=0.0):
    return jnp.pad(a, ((0, 0), (0, LANES - a.shape[1])), constant_values=value)


def kernel(x, mem, ab_w_in, ab_gate_b, ab_conv_w, ab_conv_b, ab_wq, ab_wk, ab_ret_norm_g, ab_mlstm_norm_g, ab_w_out, cd_w_in, cd_rel_bias, cd_w_out, mix_ln_g, mix_ln_b, xa_wq, xa_wkv, xa_wo, xa_ln_g, xa_ln_b, moe_router_w, moe_router_b, moe_w_gate, moe_b_gate, moe_w_up, moe_b_up, moe_w_down, moe_b_down, moe_ln_g, moe_ln_b):
    bsz, seq, d = x.shape
    depth = mix_ln_g.shape[0]
    heads = GROUP_HEADS
    alpha = (2.0 * depth) ** 0.25
    t = bsz * seq
    cos2, sin2 = rope_tables(seq)
    log_g = retention_log_decay(heads)
    h = x.reshape(t, d)
    hb = h.astype(BF16)
    memb = mem.reshape(-1, d).astype(BF16)
    n_ab = 7 * GROUP_WIDTH
    n_cd = 6 * GROUP_WIDTH
    ab_w_in_t = jnp.swapaxes(ab_w_in, 1, 2)
    for layer in range(depth):
        i = layer // 2
        if layer % 2 == 0:
            z = matmul_nt_stacked(hb, ab_w_in_t, i, n_ab, tm=1024, tn=512, out_dtype=BF16)
            gates = gates_nt(hb, ab_w_in_t, i, n_ab, 2 * heads, _pad_lanes(ab_gate_b[i][None, :]), tm=1024)
            gain = jnp.concatenate([ab_ret_norm_g[i], ab_mlstm_norm_g[i]])
            yb = mixer_ab(z, gates, cos2, sin2, log_g, ab_conv_w, ab_conv_b, ab_wq, ab_wk, gain, i,
                          bsz=bsz, seq=seq, heads=heads)
            w_out = ab_w_out[i].astype(BF16)
        else:
            z = matmul_stacked(hb, cd_w_in, i, n_cd, tm=1024, tn=512, out_dtype=BF16)
            yb = mixer_cd(z, band_bias_table(cd_rel_bias[i]), bsz=bsz, seq=seq, heads=heads)
            w_out = cd_w_out[i].astype(BF16)
        h, hb = matmul_ln(yb, w_out, h, mix_ln_g[layer][None, :], mix_ln_b[layer][None, :], alpha)

        kvb = matmul_stacked(memb, xa_wkv, layer, xa_wkv.shape[2], tm=memb.shape[0], tn=512, out_dtype=BF16)
        h, hp, topw, topi, sel = xattn_ln(
            hb, h, xa_wq[layer].astype(BF16), kvb, xa_wo[layer].astype(BF16),
            xa_ln_g[layer][None, :], xa_ln_b[layer][None, :],
            _pad_lanes(moe_router_w[layer]), _pad_lanes(moe_router_b[layer][None, :]), alpha, seq=seq)
        plan = moe_route_plan(topi, sel, tm=MOE_TM)
        y = moe_experts(hp, plan, moe_w_gate, moe_b_gate, moe_w_up, moe_b_up, moe_w_down, moe_b_down, layer, tm=MOE_TM)
        h, hb = moe_combine_ln(y, topw, h, moe_ln_g[layer][None, :], moe_ln_b[layer][None, :], alpha)
    return h.reshape(bsz, seq, d)
```

```python
import functools

import jax
import jax.numpy as jnp
from jax import lax
from jax.experimental import pallas as pl
from jax.experimental.pallas import tpu as pltpu

F32 = jnp.float32
BF16 = jnp.bfloat16

CHUNK = 64
HEAD_DIM = 128
GROUP_HEADS = 16
GROUP_WIDTH = GROUP_HEADS * HEAD_DIM
CONV_K = 4
PAST_CHUNKS = 8
REL_MAX = 2 * CHUNK
ROPE_BASE = 10000.0
XA_HEADS = 4
N_EXPERTS = 32
TOP_K = 4
SWIGLU_LIMIT = 7.0
SWIGLU_ALPHA = 1.702
LN_EPS = 1e-5
HN_EPS = 1e-6

LANES = 128
VMEM_LIMIT_BYTES = 58 * 1024 * 1024


def _params(*sem):
    return pltpu.CompilerParams(dimension_semantics=sem, vmem_limit_bytes=VMEM_LIMIT_BYTES)


def _mm_kernel(x_ref, w_ref, o_ref, wb_ref):
    @pl.when(pl.program_id(1) == 0)
    def _():
        wb_ref[...] = w_ref[...].astype(BF16)

    o_ref[...] = jnp.dot(x_ref[...], wb_ref[...], preferred_element_type=F32).astype(o_ref.dtype)


def matmul_stacked(x, w, layer, n_cols, *, tm, tn, out_dtype):
    m, k = x.shape
    return pl.pallas_call(
        _mm_kernel,
        grid=(n_cols // tn, m // tm),
        in_specs=[
            pl.BlockSpec((tm, k), lambda j, i: (i, 0)),
            pl.BlockSpec((None, k, tn), lambda j, i: (layer, 0, j)),
        ],
        out_specs=pl.BlockSpec((tm, tn), lambda j, i: (i, j)),
        out_shape=jax.ShapeDtypeStruct((m, n_cols), out_dtype),
        scratch_shapes=[pltpu.VMEM((k, tn), BF16)],
        compiler_params=_params("arbitrary", "arbitrary"),
        name="mm_in",
    )(x, w)


_NT = (((1,), (1,)), ((), ()))
_TN = (((0,), (0,)), ((), ()))


def _mm_nt_kernel(x_ref, w_ref, o_ref, wb_ref):
    @pl.when(pl.program_id(1) == 0)
    def _():
        wb_ref[...] = w_ref[...].astype(BF16)

    o_ref[...] = lax.dot_general(x_ref[...], wb_ref[...], _NT, preferred_element_type=F32).astype(o_ref.dtype)


def matmul_nt_stacked(x, wt, layer, n_cols, *, tm, tn, out_dtype):
    m, k = x.shape
    return pl.pallas_call(
        _mm_nt_kernel,
        grid=(n_cols // tn, m // tm),
        in_specs=[
            pl.BlockSpec((tm, k), lambda j, i: (i, 0)),
            pl.BlockSpec((None, tn, k), lambda j, i: (layer, j, 0)),
        ],
        out_specs=pl.BlockSpec((tm, tn), lambda j, i: (i, j)),
        out_shape=jax.ShapeDtypeStruct((m, n_cols), out_dtype),
        scratch_shapes=[pltpu.VMEM((tn, k), BF16)],
        compiler_params=_params("arbitrary", "arbitrary"),
        name="mm_in_nt",
    )(x, wt)


def _gates_kernel(x_ref, w_ref, b_ref, o_ref, wb_ref):
    @pl.when(pl.program_id(0) == 0)
    def _():
        wb_ref[...] = jnp.zeros_like(wb_ref)
        wb_ref[:w_ref.shape[0], :] = w_ref[...].astype(BF16)

    o_ref[...] = lax.dot_general(x_ref[...], wb_ref[...], _NT, preferred_element_type=F32) + b_ref[...]


def gates_nt(x, wt, layer, row0, n_rows, b_pad, *, tm):
    m, k = x.shape
    return pl.pallas_call(
        _gates_kernel,
        grid=(m // tm,),
        in_specs=[
            pl.BlockSpec((tm, k), lambda i: (i, 0)),
            pl.BlockSpec((None, n_rows, k), lambda i: (layer, row0 // n_rows, 0)),
            pl.BlockSpec((1, LANES), lambda i: (0, 0)),
        ],
        out_specs=pl.BlockSpec((tm, LANES), lambda i: (i, 0)),
        out_shape=jax.ShapeDtypeStruct((m, LANES), F32),
        scratch_shapes=[pltpu.VMEM((LANES, k), BF16)],
        compiler_params=_params("arbitrary"),
        name="gates",
    )(x, wt, b_pad)


def _pack_bf16_pair(y):
    half = y.shape[1] // 2
    hi = pltpu.bitcast(y[:, :half].astype(BF16).astype(F32), jnp.uint32)
    lo = pltpu.bitcast(y[:, half:].astype(BF16).astype(F32), jnp.uint32)
    return hi | (lo >> 16)


def _unpack_bf16_pair(w):
    return (pltpu.bitcast(w & jnp.uint32(0xFFFF0000), F32), pltpu.bitcast(w << 16, F32))


def _ln_rows(z_ref, g_ref, b_ref, of_ref, ob_ref, rows, packed=False):
    tm = z_ref.shape[0]

    def body(r, carry):
        sl = pl.ds(pl.multiple_of(r * rows, rows), rows)
        z = z_ref[sl, :]
        mu = jnp.mean(z, axis=-1, keepdims=True)
        zc = z - mu
        var = jnp.mean(zc * zc, axis=-1, keepdims=True)
        y = zc * lax.rsqrt(var + LN_EPS) * g_ref[...] + b_ref[...]
        of_ref[sl, :] = y
        ob_ref[sl, :] = _pack_bf16_pair(y) if packed else y.astype(BF16)
        return carry

    lax.fori_loop(0, tm // rows, body, 0, unroll=2)


def _mm_ln_kernel(x_ref, w_ref, h_ref, g_ref, b_ref, of_ref, ob_ref, *, nk, nj, tn, alpha):
    k = pl.program_id(1)
    j = pl.program_id(2)
    part = jnp.dot(x_ref[...], w_ref[...].astype(BF16), preferred_element_type=F32)
    for jj in range(nj):
        sl = slice(jj * tn, (jj + 1) * tn)

        @pl.when((j == jj) & (k == 0))
        def _():
            of_ref[:, sl] = alpha * h_ref[...] + part

        @pl.when((j == jj) & (k > 0))
        def _():
            of_ref[:, sl] += part

    @pl.when((k == nk - 1) & (j == nj - 1))
    def _():
        _ln_rows(of_ref, g_ref, b_ref, of_ref, ob_ref, 32)


def matmul_ln(x, w, h, g, b, alpha, *, tm=512, tn=512, tk=4096):
    m, kdim = x.shape
    n = h.shape[1]
    tk = min(tk, kdim)
    nk, nj = kdim // tk, n // tn
    in_specs = [
        pl.BlockSpec((tm, tk), lambda i, k, j: (i, k)),
        pl.BlockSpec((tk, tn), lambda i, k, j: (k, j)),
        pl.BlockSpec((tm, tn), lambda i, k, j: (i, j)),
        pl.BlockSpec((1, n), lambda i, k, j: (0, 0)),
        pl.BlockSpec((1, n), lambda i, k, j: (0, 0)),
    ]
    args = [x, w, h, g, b]
    return pl.pallas_call(
        functools.partial(_mm_ln_kernel, nk=nk, nj=nj, tn=tn, alpha=alpha),
        grid=(m // tm, nk, nj),
        in_specs=in_specs,
        out_specs=[
            pl.BlockSpec((tm, n), lambda i, k, j: (i, 0)),
            pl.BlockSpec((tm, n), lambda i, k, j: (i, 0)),
        ],
        out_shape=[jax.ShapeDtypeStruct((m, n), F32), jax.ShapeDtypeStruct((m, n), BF16)],
        compiler_params=_params("arbitrary", "arbitrary", "arbitrary"),
        name="mm_ln",
    )(*args)


def _xattn_kernel(h_ref, wq_ref, kv_ref, wo_ref, g_ref, b_ref, wrh_ref, wrl_ref, br_ref,
                  of_ref, ob_ref, topw_ref, topi_ref, sel_ref, *, alpha, heads, hd):
    q = jnp.dot(h_ref[...].astype(BF16), wq_ref[...], preferred_element_type=F32)
    scale = hd ** -0.5
    outs = []
    for hh in range(heads):
        qh = (q[:, hh * hd:(hh + 1) * hd] * scale).astype(BF16)
        kh = kv_ref[:, hh * hd:(hh + 1) * hd]
        vh = kv_ref[:, (heads + hh) * hd:(heads + hh + 1) * hd]
        s = lax.dot_general(qh, kh, (((1,), (1,)), ((), ())), preferred_element_type=F32)
        s = s - jnp.max(s, axis=-1, keepdims=True)
        p = jnp.exp(s)
        l = jnp.sum(p, axis=-1, keepdims=True)
        o = jnp.dot(p.astype(BF16), vh, preferred_element_type=F32) / l
        outs.append(o.astype(BF16))
    o_all = jnp.concatenate(outs, axis=-1)
    of_ref[...] = alpha * h_ref[...] + jnp.dot(o_all, wo_ref[...], preferred_element_type=F32)
    _ln_rows(of_ref, g_ref, b_ref, of_ref, ob_ref, 32, packed=True)
    topw_ref[...], topi_ref[...], sel_ref[...] = _route_top4(of_ref[...], wrh_ref[...], wrl_ref[...], br_ref[...])


def xattn_ln(h, wq_b, kv_b, wo_b, g, b, w_router, b_router, alpha, *, seq, tm=512):
    m, d = h.shape
    lane_spec = pl.BlockSpec((tm, LANES), lambda i: (i, 0))
    xw = wq_b.shape[1]
    n_mem = kv_b.shape[0] // (m // seq)
    per_b = seq // tm

    def const_spec(shape):
        return pl.BlockSpec(shape, lambda i: (0, 0), pipeline_mode=pl.Buffered(1))

    return pl.pallas_call(
        functools.partial(_xattn_kernel, alpha=alpha, heads=XA_HEADS, hd=xw // XA_HEADS),
        grid=(m // tm,),
        in_specs=[
            pl.BlockSpec((tm, d), lambda i: (i, 0)),
            const_spec((d, xw)),
            pl.BlockSpec((n_mem, 2 * xw), lambda i: (i // per_b, 0)),
            const_spec((xw, d)),
            const_spec((1, d)),
            const_spec((1, d)),
            const_spec((d, LANES)),
            const_spec((d, LANES)),
            const_spec((1, LANES)),
        ],
        out_specs=[pl.BlockSpec((tm, d), lambda i: (i, 0)), pl.BlockSpec((tm, d // 2), lambda i: (i, 0)),
                   lane_spec, lane_spec, lane_spec],
        out_shape=[jax.ShapeDtypeStruct((m, d), F32), jax.ShapeDtypeStruct((m, d // 2), jnp.uint32),
                   jax.ShapeDtypeStruct((m, LANES), F32), jax.ShapeDtypeStruct((m, LANES), jnp.int32),
                   jax.ShapeDtypeStruct((m, LANES), F32)],
        compiler_params=_params("arbitrary"),
        name="xattn_ln",
    )(h, wq_b, kv_b, wo_b, g, b, *_split_bf16(w_router), b_router)


def _route_top4(h, w_hi, w_lo, b):
    h_hi, h_lo = _split_bf16(h)
    both = jnp.dot(h_hi, jnp.concatenate([w_hi, w_lo], axis=1), preferred_element_type=F32)
    logits = both[:, :LANES] + both[:, LANES:] + jnp.dot(h_lo, w_hi, preferred_element_type=F32) + b
    lane = lax.broadcasted_iota(jnp.int32, logits.shape, 1)
    neg = jnp.float32(-jnp.inf)
    masked = jnp.where(lane < N_EXPERTS, logits, neg)
    top_vals, top_idx = [], []
    sel = jnp.zeros_like(logits)
    for _ in range(TOP_K):
        mval = jnp.max(masked, axis=-1, keepdims=True)
        idx = jnp.min(jnp.where(masked == mval, lane, LANES), axis=-1, keepdims=True)
        hot = lane == idx
        top_vals.append(mval)
        top_idx.append(idx)
        sel = jnp.where(hot, 1.0, sel)
        masked = jnp.where(hot, neg, masked)
    exps = [jnp.exp(v - top_vals[0]) for v in top_vals]
    denom = exps[0]
    for e in exps[1:]:
        denom = denom + e
    topw = jnp.zeros_like(logits)
    topi = jnp.zeros(logits.shape, jnp.int32)
    for k in range(TOP_K):
        topw = jnp.where(lane == k, exps[k] / denom, topw)
        topi = jnp.where(lane == k, top_idx[k], topi)
    return topw, topi, sel


MOE_TM = 256


def moe_route_plan(topi, sel, *, tm):
    t = topi.shape[0]
    n_tiles = t * TOP_K // tm + N_EXPERTS
    rows = n_tiles * tm
    top4 = topi[:, :TOP_K]
    seli = sel[:, :N_EXPERTS].astype(jnp.int32)
    pos_incl = jnp.cumsum(seli, axis=0)
    counts = pos_incl[-1]
    pos = pos_incl - seli
    tiles_e = (counts + tm - 1) // tm
    tile_end = jnp.cumsum(tiles_e)
    row_off = (tile_end - tiles_e) * tm
    dest = (row_off[top4] + jnp.take_along_axis(pos, top4, axis=1)).reshape(-1)
    slot_rows = (jnp.arange(t, dtype=jnp.int32)[:, None] + t * jnp.arange(TOP_K, dtype=jnp.int32)[None, :]).reshape(-1)
    ydst = (TOP_K * t + jnp.arange(rows, dtype=jnp.int32) % (2 * tm)).at[dest].set(slot_rows, unique_indices=True)
    tok = ydst % t
    tile_ids = jnp.arange(n_tiles, dtype=jnp.int32)
    tile_expert = jnp.minimum(jnp.sum(tile_end[None, :] <= tile_ids[:, None], axis=1), N_EXPERTS - 1)
    n_valid = tile_end[-1:].astype(jnp.int32)
    return (tok.reshape(n_tiles, 1, tm), ydst.reshape(n_tiles, 1, tm), tile_expert.astype(jnp.int32), n_valid)


def _moe_ffn_kernel(te_ref, nv_ref, tok0_ref, tok1_ref, tok2_ref, ydst_ref, h_hbm, wg_ref, wu_ref, wd_ref,
                    bg_ref, bu_ref, bd_ref, y_hbm, xg, og, wgu_b, wd_b, gsem, ssem, *, tm, ff):
    r = pl.program_id(0)
    nv = nv_ref[0]
    slot = lax.rem(r, 2)
    gslot = lax.rem(r, 3)

    def row_gather(idx_ref, s):
        for i in range(tm):
            pltpu.make_async_copy(h_hbm.at[pl.ds(idx_ref[0, i], 1), :], xg.at[s, pl.ds(i, 1), :], gsem.at[s]).start()

    def gather_wait(s):
        pltpu.make_async_copy(h_hbm.at[pl.ds(0, tm), :], xg.at[s], gsem.at[s]).wait()

    def scatter_wait(s):
        pltpu.make_async_copy(og.at[s], y_hbm.at[pl.ds(0, tm), :], ssem.at[s]).wait()

    @pl.when(r == 0)
    def _():
        row_gather(tok0_ref, 0)
        row_gather(tok1_ref, 1)
        og[1] = jnp.zeros(og.shape[1:], og.dtype)
        base = y_hbm.shape[0] - 2 * tm
        for part in range(2):
            fill = pltpu.make_async_copy(og.at[1], y_hbm.at[pl.ds(base + part * tm, tm), :], ssem.at[1])
            fill.start()
            fill.wait()

    @pl.when(r < nv)
    def _():
        gather_wait(gslot)

        @pl.when(r >= 2)
        def _():
            scatter_wait(slot)

        @pl.when((r == 0) | (te_ref[r] != te_ref[jnp.maximum(r - 1, 0)]))
        def _():
            wgu_b[:, :ff] = wg_ref[...].astype(BF16)
            wgu_b[:, ff:] = wu_ref[...].astype(BF16)
            wd_b[...] = wd_ref[...].astype(BF16)

        x_hi, x_lo = _unpack_bf16_pair(xg[gslot])
        half = x_hi.shape[1]
        gu = (jnp.dot(x_hi.astype(BF16), wgu_b[:half, :], preferred_element_type=F32)
              + jnp.dot(x_lo.astype(BF16), wgu_b[half:, :], preferred_element_type=F32))
        g = jnp.minimum(gu[:, :ff] + bg_ref[...], SWIGLU_LIMIT)
        u = jnp.clip(gu[:, ff:] + bu_ref[...], -SWIGLU_LIMIT, SWIGLU_LIMIT)
        act = (g * jax.nn.sigmoid(SWIGLU_ALPHA * g) * (u + 1.0)).astype(BF16)
        res = _pack_bf16_pair(jnp.dot(act, wd_b[...], preferred_element_type=F32) + bd_ref[...])
        row_gather(tok2_ref, lax.rem(r + 2, 3))
        og[slot] = res
        for i in range(tm):
            pltpu.make_async_copy(og.at[slot, pl.ds(i, 1), :], y_hbm.at[pl.ds(ydst_ref[0, i], 1), :],
                                  ssem.at[slot]).start(priority=i % 2)

        @pl.when(r == nv - 1)
        def _():
            gather_wait(lax.rem(r + 1, 3))
            gather_wait(lax.rem(r + 2, 3))
            scatter_wait(slot)

            @pl.when(r >= 1)
            def _():
                scatter_wait(1 - slot)


def moe_experts(hp, plan, w_gate, b_gate, w_up, b_up, w_down, b_down, layer, *, tm):
    tok, ydst, tile_expert, n_valid = plan
    t = hp.shape[0]
    d = 2 * hp.shape[1]
    n_tiles = tok.shape[0]
    n_e, ff = w_gate.shape[1], w_gate.shape[3]
    bg = b_gate.reshape(b_gate.shape[0], n_e, 1, ff)
    bu = b_up.reshape(b_up.shape[0], n_e, 1, ff)
    bd = b_down.reshape(b_down.shape[0], n_e, 1, d)

    def expert(r, te, nv):
        return te[jnp.minimum(r, nv[0] - 1)]

    smem_blk = functools.partial(pl.BlockSpec, (None, 1, tm), memory_space=pltpu.SMEM)
    grid_spec = pltpu.PrefetchScalarGridSpec(
        num_scalar_prefetch=2,
        grid=(n_tiles,),
        in_specs=[
            smem_blk(lambda r, te, nv: (jnp.minimum(r, nv[0] - 1), 0, 0)),
            smem_blk(lambda r, te, nv: (jnp.minimum(r + 1, nv[0] - 1), 0, 0)),
            smem_blk(lambda r, te, nv: (jnp.minimum(r + 2, nv[0] - 1), 0, 0)),
            smem_blk(lambda r, te, nv: (r, 0, 0)),
            pl.BlockSpec(memory_space=pl.ANY),
            pl.BlockSpec((None, None, d, ff), lambda r, te, nv: (layer, expert(r, te, nv), 0, 0)),
            pl.BlockSpec((None, None, d, ff), lambda r, te, nv: (layer, expert(r, te, nv), 0, 0)),
            pl.BlockSpec((None, None, ff, d), lambda r, te, nv: (layer, expert(r, te, nv), 0, 0)),
            pl.BlockSpec((None, None, 1, ff), lambda r, te, nv: (layer, expert(r, te, nv), 0, 0)),
            pl.BlockSpec((None, None, 1, ff), lambda r, te, nv: (layer, expert(r, te, nv), 0, 0)),
            pl.BlockSpec((None, None, 1, d), lambda r, te, nv: (layer, expert(r, te, nv), 0, 0)),
        ],
        out_specs=pl.BlockSpec(memory_space=pl.ANY),
        scratch_shapes=[
            pltpu.VMEM((3, tm, d // 2), jnp.uint32),
            pltpu.VMEM((2, tm, d // 2), jnp.uint32),
            pltpu.VMEM((d, 2 * ff), BF16),
            pltpu.VMEM((ff, d), BF16),
            pltpu.SemaphoreType.DMA((3,)),
            pltpu.SemaphoreType.DMA((2,)),
        ],
    )
    return pl.pallas_call(
        functools.partial(_moe_ffn_kernel, tm=tm, ff=ff),
        grid_spec=grid_spec,
        out_shape=jax.ShapeDtypeStruct((TOP_K * t + 2 * tm, d // 2), jnp.uint32),
        compiler_params=_params("arbitrary"),
        name="moe_experts",
    )(tile_expert, n_valid, tok, tok, tok, ydst, hp, w_gate, w_up, w_down, bg, bu, bd)


def _moe_combine_kernel(y0_ref, y1_ref, y2_ref, y3_ref, w_ref, h_ref, g_ref, b_ref, of_ref, ob_ref, *, alpha, rows):
    tm = h_ref.shape[0]
    y_refs = (y0_ref, y1_ref, y2_ref, y3_ref)

    def body(rr, carry):
        sl = pl.ds(pl.multiple_of(rr * rows, rows), rows)
        w = w_ref[sl, :]
        hrow = h_ref[sl, :]
        half = hrow.shape[1] // 2
        z_hi = alpha * hrow[:, :half]
        z_lo = alpha * hrow[:, half:]
        for k, y_ref in enumerate(y_refs):
            y_hi, y_lo = _unpack_bf16_pair(y_ref[sl, :])
            z_hi = z_hi + w[:, k:k + 1] * y_hi
            z_lo = z_lo + w[:, k:k + 1] * y_lo
        z = jnp.concatenate([z_hi, z_lo], axis=1)
        mu = jnp.mean(z, axis=-1, keepdims=True)
        zc = z - mu
        var = jnp.mean(zc * zc, axis=-1, keepdims=True)
        y = zc * lax.rsqrt(var + LN_EPS) * g_ref[...] + b_ref[...]
        of_ref[sl, :] = y
        ob_ref[sl, :] = y.astype(BF16)
        return carry

    lax.fori_loop(0, tm // rows, body, 0)


def moe_combine_ln(y, topw, h, g, b, alpha, *, tm=128):
    t, d = h.shape
    nb = t // tm

    def y_spec(k):
        return pl.BlockSpec((tm, d // 2), lambda i: (k * nb + i, 0))

    row_spec = pl.BlockSpec((tm, d), lambda i: (i, 0))
    return pl.pallas_call(
        functools.partial(_moe_combine_kernel, alpha=alpha, rows=32),
        grid=(nb,),
        in_specs=[y_spec(0), y_spec(1), y_spec(2), y_spec(3),
                  pl.BlockSpec((tm, LANES), lambda i: (i, 0)), row_spec,
                  pl.BlockSpec((1, d), lambda i: (0, 0)), pl.BlockSpec((1, d), lambda i: (0, 0))],
        out_specs=[row_spec, row_spec],
        out_shape=[jax.ShapeDtypeStruct((t, d), F32), jax.ShapeDtypeStruct((t, d), BF16)],
        compiler_params=_params("arbitrary"),
        name="moe_combine_ln",
    )(y, y, y, y, topw, h, g, b)


NEG = -1e30
MIX_CHUNK = 256
BAND_BLOCK = 256
SB_TQ = 512
SB_TK = 256

def _head_norm_rows(x, gain):
    mu = jnp.mean(x, axis=-1, keepdims=True)
    xc = x - mu
    var = jnp.mean(xc * xc, axis=-1, keepdims=True)
    return xc * lax.rsqrt(var + HN_EPS) * gain


def _log_sigmoid(x):
    return jnp.minimum(x, 0.0) - jnp.log(1.0 + jnp.exp(-jnp.abs(x)))


def _split_bf16(x):
    hi = x.astype(BF16)
    lo = (x - hi.astype(F32)).astype(BF16)
    return hi, lo


def _lane_select(x, lane_idx):
    lane = lax.broadcasted_iota(jnp.int32, x.shape, 1)
    col = jnp.sum(jnp.where(lane == lane_idx, x, 0.0), axis=-1, keepdims=True)
    return jnp.broadcast_to(col, x.shape)


def _retention_body(q_ref, k_ref, v_ref, g_ref, cos_ref, sin_ref, lg_ref, gain_ref, o_ref,
                    qs_ref, ks_ref, st_ref, *, scale):
    seq = q_ref.shape[0]
    L = MIX_CHUNK
    half = HEAD_DIM // 2
    cos = cos_ref[...]
    sin = sin_ref[...]
    q = q_ref[...].astype(F32)
    k = k_ref[...].astype(F32)
    qs_ref[...] = (q * cos + pltpu.roll(q, half, 1) * sin).astype(BF16)
    ks_ref[...] = ((k * cos + pltpu.roll(k, half, 1) * sin) * scale).astype(BF16)

    lg = lg_ref[...]
    ri = lax.broadcasted_iota(jnp.int32, (L, L), 0)
    ci = lax.broadcasted_iota(jnp.int32, (L, L), 1)
    intra = jnp.where(ri >= ci, jnp.exp(lg * jnp.maximum(ri - ci, 0).astype(F32)), 0.0)
    rr = lax.broadcasted_iota(jnp.int32, (L, HEAD_DIM), 0).astype(F32)
    lg_d = lg[:, :HEAD_DIM]
    q_dec = jnp.exp(lg_d * (rr + 1.0))
    k_dec = jnp.exp(lg_d * (L - 1.0 - rr))
    c_dec = jnp.exp(lg_d * float(L))
    gain = gain_ref[...]
    st_ref[...] = jnp.zeros_like(st_ref)

    def chunk(c, carry):
        sl = pl.ds(pl.multiple_of(c * L, L), L)
        qc = qs_ref[sl, :]
        kc = ks_ref[sl, :]
        vc = v_ref[sl, :]
        state = st_ref[:, :HEAD_DIM]
        att = lax.dot_general(qc, kc, _NT, preferred_element_type=F32) * intra
        o = (jnp.dot(att.astype(BF16), vc, preferred_element_type=F32)
             + jnp.dot((qc.astype(F32) * q_dec).astype(BF16), state.astype(BF16), preferred_element_type=F32))
        st_ref[:, :HEAD_DIM] = state * c_dec + lax.dot_general(
            (kc.astype(F32) * k_dec).astype(BF16), vc, _TN, preferred_element_type=F32)
        gv = g_ref[sl, :].astype(F32)
        o_ref[sl, :] = (_head_norm_rows(o, gain) * (gv * jax.nn.sigmoid(gv))).astype(o_ref.dtype)
        return carry

    lax.fori_loop(0, seq // L, chunk, 0, unroll=True)


def _mlstm_body(u_ref, v_ref, og_ref, gates_ref, cw_ref, cb_ref, wq_ref, wk_ref, gain_ref, o_ref,
                qs_ref, ks_ref, st_ref, m_ref, *, head, heads, scale):
    seq = u_ref.shape[0]
    L = MIX_CHUNK
    d = HEAD_DIM
    x = u_ref[...].astype(F32)
    row = lax.broadcasted_iota(jnp.int32, x.shape, 0)
    cw = cw_ref[...]
    y = x * cw[CONV_K - 1:CONV_K, :] + cb_ref[...]
    for sh in range(1, CONV_K):
        xs = jnp.where(row >= sh, pltpu.roll(x, sh, 0), 0.0)
        y = y + xs * cw[CONV_K - 1 - sh:CONV_K - sh, :]
    ub = (y * jax.nn.sigmoid(y)).astype(BF16)
    qs_ref[...] = jnp.dot(ub, wq_ref[...].astype(BF16), preferred_element_type=F32).astype(BF16)
    ks_ref[...] = (jnp.dot(ub, wk_ref[...].astype(BF16), preferred_element_type=F32) * scale).astype(BF16)

    ri = lax.broadcasted_iota(jnp.int32, (L, L), 0)
    ci = lax.broadcasted_iota(jnp.int32, (L, L), 1)
    causal = ri >= ci
    tri = jnp.where(causal, 1.0, 0.0).astype(BF16)
    ones_v = jnp.ones((L, d), BF16)
    gain = gain_ref[...]
    st_ref[...] = jnp.zeros_like(st_ref)
    m_ref[...] = jnp.zeros_like(m_ref)

    def chunk(c, carry):
        sl = pl.ds(pl.multiple_of(c * L, L), L)
        qc = qs_ref[sl, :]
        kc = ks_ref[sl, :]
        v_ext = jnp.concatenate([v_ref[sl, :], ones_v], axis=1)
        gts = gates_ref[sl, :]
        ic = _lane_select(gts, head)
        lf = _log_sigmoid(_lane_select(gts, heads + head))
        lf_hi, lf_lo = _split_bf16(lf)
        bcum = (jnp.dot(tri, lf_hi, preferred_element_type=F32)
                + jnp.dot(tri, lf_lo, preferred_element_type=F32))
        m_st = m_ref[...]
        src = jnp.transpose(ic - bcum)[:1, :]
        bcum2 = jnp.concatenate([bcum, bcum], axis=1)
        log_intra = jnp.where(causal, bcum2 + src, NEG)
        m_intra = jnp.max(log_intra, axis=-1, keepdims=True)
        log_cross = bcum + m_st
        m_row = jnp.maximum(log_cross, m_intra)
        m_row2 = jnp.concatenate([m_row, m_row], axis=1)
        w_intra = jnp.exp(log_intra - m_row2)
        w_cross = jnp.exp(log_cross - m_row)
        w_cross2 = jnp.concatenate([w_cross, w_cross], axis=1)
        qk = lax.dot_general(qc, kc, _NT, preferred_element_type=F32) * w_intra
        state = st_ref[...]
        res = (jnp.dot(qk.astype(BF16), v_ext, preferred_element_type=F32)
               + w_cross2 * jnp.dot(qc, state.astype(BF16), preferred_element_type=F32))
        num = res[:, :d]
        den = res[:, d:]
        hh = num / jnp.maximum(jnp.abs(den), jnp.exp(-m_row))
        og = og_ref[sl, :].astype(F32)
        o_ref[sl, :] = _head_norm_rows(hh * jax.nn.sigmoid(og), gain).astype(o_ref.dtype)
        b_last = bcum[L - 1:L, :]
        log_state = b_last - bcum + ic
        m_new = jnp.maximum(b_last + m_st, jnp.max(log_state, axis=0, keepdims=True))
        decay = jnp.exp(b_last + m_st - m_new)
        kw = (kc.astype(F32) * jnp.exp(log_state - m_new)).astype(BF16)
        decay2 = jnp.concatenate([decay, decay], axis=1)
        st_ref[...] = decay2 * state + lax.dot_general(kw, v_ext, _TN, preferred_element_type=F32)
        m_ref[...] = m_new
        return carry

    lax.fori_loop(0, seq // L, chunk, 0, unroll=True)


def _mixer_ab_kernel(a0_ref, a1_ref, a2_ref, a3_ref, gates_ref, cos_ref, sin_ref, lg_ref, cw_ref, cb_ref,
                     wq_ref, wk_ref, gain_ref, o_ref, qs_ref, ks_ref, st_ref, m_ref, *, heads, scale):
    g = pl.program_id(1)

    @pl.when(g < heads)
    def _():
        _retention_body(a0_ref, a1_ref, a2_ref, a3_ref, cos_ref, sin_ref, lg_ref, gain_ref, o_ref,
                        qs_ref, ks_ref, st_ref, scale=scale)

    @pl.when(g >= heads)
    def _():
        _mlstm_body(a0_ref, a1_ref, a2_ref, gates_ref, cw_ref, cb_ref, wq_ref, wk_ref, gain_ref, o_ref,
                    qs_ref, ks_ref, st_ref, m_ref, head=g - heads, heads=heads, scale=scale)


def mixer_ab(z, gates, cos2, sin2, log_g, conv_w, conv_b, wq_m, wk_m, gain, layer, *, bsz, seq, heads):
    t = z.shape[0]
    d = HEAD_DIM
    L = MIX_CHUNK

    def col(base_ret, base_ml):
        def index(b, g):
            is_ml = g // heads
            return (b, (1 - is_ml) * (base_ret * heads + g) + is_ml * (base_ml * heads + g - heads))
        return pl.BlockSpec((seq, d), index)

    def ml_head(g):
        return jnp.maximum(g - heads, 0)

    conv_w4 = conv_w.reshape(conv_w.shape[0], CONV_K, heads, d).transpose(0, 2, 1, 3)
    conv_b4 = conv_b.reshape(conv_b.shape[0], heads, 1, d)
    gain4 = gain.reshape(2 * heads, 1, d)
    return pl.pallas_call(
        functools.partial(_mixer_ab_kernel, heads=heads, scale=d ** -0.5),
        grid=(bsz, 2 * heads),
        in_specs=[
            col(0, 4), col(1, 5), col(2, 6), col(3, 6),
            pl.BlockSpec((seq, LANES), lambda b, g: (b, 0)),
            pl.BlockSpec((seq, d), lambda b, g: (0, 0)),
            pl.BlockSpec((seq, d), lambda b, g: (0, 0)),
            pl.BlockSpec((None, 1, L), lambda b, g: (jnp.minimum(g, heads - 1), 0, 0)),
            pl.BlockSpec((None, None, CONV_K, d), lambda b, g: (layer, ml_head(g), 0, 0)),
            pl.BlockSpec((None, None, 1, d), lambda b, g: (layer, ml_head(g), 0, 0)),
            pl.BlockSpec((None, None, d, d), lambda b, g: (layer, ml_head(g), 0, 0)),
            pl.BlockSpec((None, None, d, d), lambda b, g: (layer, ml_head(g), 0, 0)),
            pl.BlockSpec((None, 1, d), lambda b, g: (g, 0, 0)),
        ],
        out_specs=pl.BlockSpec((seq, d), lambda b, g: (b, g)),
        out_shape=jax.ShapeDtypeStruct((t, 2 * heads * d), BF16),
        scratch_shapes=[
            pltpu.VMEM((seq, d), BF16),
            pltpu.VMEM((seq, d), BF16),
            pltpu.VMEM((d, 2 * d), F32),
            pltpu.VMEM((1, d), F32),
        ],
        compiler_params=_params("arbitrary", "arbitrary"),
        name="mixer_ab",
    )(z, z, z, z, gates, cos2, sin2, log_g, conv_w4, conv_b4, wq_m, wk_m, gain4)


def _band_body(q_ref, k_ref, v_ref, bias_ref, o_ref, *, scale):
    seq = q_ref.shape[0]
    bq = BAND_BLOCK
    n_back = bias_ref.shape[0]

    def block(i, carry):
        sl = pl.ds(pl.multiple_of(i * bq, bq), bq)
        q = q_ref[sl, :]
        scores, vals = [], []
        for dlt in range(n_back):
            ks = pl.ds(pl.multiple_of(jnp.maximum(i - dlt, 0) * bq, bq), bq)
            s = lax.dot_general(q, k_ref[ks, :], _NT, preferred_element_type=F32) * scale + bias_ref[dlt]
            scores.append(jnp.where(i - dlt >= 0, s, NEG))
            vals.append(v_ref[ks, :])
        m = jnp.max(scores[0], axis=-1, keepdims=True)
        for s in scores[1:]:
            m = jnp.maximum(m, jnp.max(s, axis=-1, keepdims=True))
        acc = jnp.zeros((bq, HEAD_DIM), F32)
        l = jnp.zeros((bq, 1), F32)
        for s, vv in zip(scores, vals):
            p = jnp.exp(s - m)
            l = l + jnp.sum(p, axis=-1, keepdims=True)
            acc = acc + jnp.dot(p.astype(BF16), vv, preferred_element_type=F32)
        o_ref[sl, :] = (acc / l).astype(o_ref.dtype)
        return carry

    lax.fori_loop(0, seq // bq, block, 0, unroll=True)


def _stick_breaking_body(q_ref, k_ref, v_ref, o_ref, acc_ref, r_ref, *, scale):
    seq = q_ref.shape[0]
    tq, tk = SB_TQ, SB_TK
    n_sub = tq // tk
    ri = lax.broadcasted_iota(jnp.int32, (tk, tk), 0)
    ci = lax.broadcasted_iota(jnp.int32, (tk, tk), 1)
    suffix = jnp.where(ri > ci, 1.0, 0.0).astype(BF16)
    strict = (lax.broadcasted_iota(jnp.int32, (tq, tq), 1)
              < lax.broadcasted_iota(jnp.int32, (tq, tq), 0))

    def one_group(q, k0, masked):
        kb = k_ref[pl.ds(k0, tq), :]
        vb = v_ref[pl.ds(k0, tq), :]
        z = lax.dot_general(q, kb, _NT, preferred_element_type=F32) * scale
        sp = jnp.maximum(z, 0.0) + jnp.log(1.0 + jnp.exp(-jnp.abs(z)))
        ls_pos = z - sp
        if masked:
            sp = jnp.where(strict, sp, 0.0)
        sp_b = sp.astype(BF16)
        r = r_ref[...]
        pieces = []
        for s in reversed(range(n_sub)):
            sl = slice(s * tk, (s + 1) * tk)
            between = jnp.dot(sp_b[:, sl], suffix, preferred_element_type=F32)
            pieces.append(ls_pos[:, sl] - between - jnp.concatenate([r] * (tk // LANES), axis=1))
            r = r + jnp.sum(sp[:, sl], axis=-1, keepdims=True)
        p = jnp.exp(jnp.concatenate(pieces[::-1], axis=1))
        if masked:
            p = jnp.where(strict, p, 0.0)
        acc_ref[...] += jnp.dot(p.astype(BF16), vb, preferred_element_type=F32)
        r_ref[...] = r

    for i in range(seq // tq):
        q0 = i * tq
        q = q_ref[pl.ds(q0, tq), :]
        acc_ref[...] = jnp.zeros_like(acc_ref)
        r_ref[...] = jnp.zeros_like(r_ref)
        one_group(q, q0, True)

        def below(jj, c, q=q, i=i):
            one_group(q, pl.multiple_of((i - 1 - jj) * tq, tq), False)
            return c

        lax.fori_loop(0, i, below, 0, unroll=True)
        o_ref[pl.ds(q0, tq), :] = acc_ref[...].astype(o_ref.dtype)


def _mixer_cd_kernel(q_ref, k_ref, v_ref, bias_ref, o_ref, acc_ref, r_ref, *, heads, scale):
    g = pl.program_id(1)

    @pl.when(g < heads)
    def _():
        _band_body(q_ref, k_ref, v_ref, bias_ref, o_ref, scale=scale)

    @pl.when(g >= heads)
    def _():
        _stick_breaking_body(q_ref, k_ref, v_ref, o_ref, acc_ref, r_ref, scale=scale)


def _band_bias_kernel(rrow_ref, o_ref):
    bq = BAND_BLOCK
    shift = CHUNK.bit_length() - 1
    qo = lax.broadcasted_iota(jnp.int32, (bq, bq), 0)
    ck = jnp.right_shift(lax.broadcasted_iota(jnp.int32, (bq, bq), 1), shift)
    for dlt in range(o_ref.shape[0]):
        x = jnp.broadcast_to(rrow_ref[dlt], (bq, 2 * bq))
        toep = pltpu.roll(x, 0, 1, stride=1, stride_axis=0)[:, :bq]
        cq = jnp.right_shift(qo + bq * dlt, shift)
        allowed = (ck <= cq) & (ck >= cq - PAST_CHUNKS)
        o_ref[dlt] = jnp.where(allowed, toep, NEG)


def band_bias_table(rel_bias):
    bq = BAND_BLOCK
    heads = rel_bias.shape[0]
    n_back = PAST_CHUNKS * CHUNK // bq + 1
    m = jnp.arange(2 * bq)
    key_minus_query = jnp.where(m < bq, m, m - 2 * bq)
    dist = bq * jnp.arange(n_back)[:, None] - key_minus_query[None, :]
    idx = jnp.clip(dist, -(CHUNK - 1), REL_MAX) + (CHUNK - 1)
    rrow = rel_bias.astype(F32)[:, idx].reshape(heads, n_back, 1, 2 * bq)
    return pl.pallas_call(
        _band_bias_kernel,
        grid=(heads,),
        in_specs=[pl.BlockSpec((None, n_back, 1, 2 * bq), lambda hh: (hh, 0, 0, 0))],
        out_specs=pl.BlockSpec((None, n_back, bq, bq), lambda hh: (hh, 0, 0, 0)),
        out_shape=jax.ShapeDtypeStruct((heads, n_back, bq, bq), F32),
        compiler_params=_params("arbitrary"),
        name="band_bias",
    )(rrow)


def mixer_cd(z, bias_tab, *, bsz, seq, heads):
    t = z.shape[0]
    d = HEAD_DIM

    def col(which):
        return pl.BlockSpec((seq, d), lambda b, g: (b, (3 * (g // heads) + which) * heads + g % heads))

    return pl.pallas_call(
        functools.partial(_mixer_cd_kernel, heads=heads, scale=d ** -0.5),
        grid=(bsz, 2 * heads),
        in_specs=[
            col(0), col(1), col(2),
            pl.BlockSpec((None,) + bias_tab.shape[1:], lambda b, g: (jnp.minimum(g, heads - 1), 0, 0, 0)),
        ],
        out_specs=pl.BlockSpec((seq, d), lambda b, g: (b, g)),
        out_shape=jax.ShapeDtypeStruct((t, 2 * heads * d), BF16),
        scratch_shapes=[pltpu.VMEM((SB_TQ, d), F32), pltpu.VMEM((SB_TQ, LANES), F32)],
        compiler_params=_params("arbitrary", "arbitrary"),
        name="mixer_cd",
    )(z, z, z, bias_tab)


def rope_tables(seq_len):
    pos = jnp.arange(seq_len, dtype=F32)
    inv_freq = ROPE_BASE ** (-jnp.arange(0, HEAD_DIM, 2, dtype=F32) / HEAD_DIM)
    ang = pos[:, None] * inv_freq[None, :]
    cos, sin = jnp.cos(ang), jnp.sin(ang)
    return jnp.concatenate([cos, cos], axis=1), jnp.concatenate([-sin, sin], axis=1)


def retention_log_decay(heads):
    lg = jnp.log1p(-jnp.exp2(-(5.0 + jnp.arange(heads, dtype=F32))))
    return jnp.broadcast_to(lg[:, None, None], (heads, 1, MIX_CHUNK))


def _pad_lanes(a, value=0.0):
    return jnp.pad(a, ((0, 0), (0, LANES - a.shape[1])), constant_values=value)


def kernel(x, mem, ab_w_in, ab_gate_b, ab_conv_w, ab_conv_b, ab_wq, ab_wk, ab_ret_norm_g, ab_mlstm_norm_g, ab_w_out, cd_w_in, cd_rel_bias, cd_w_out, mix_ln_g, mix_ln_b, xa_wq, xa_wkv, xa_wo, xa_ln_g, xa_ln_b, moe_router_w, moe_router_b, moe_w_gate, moe_b_gate, moe_w_up, moe_b_up, moe_w_down, moe_b_down, moe_ln_g, moe_ln_b):
    bsz, seq, d = x.shape
    depth = mix_ln_g.shape[0]
    heads = GROUP_HEADS
    alpha = (2.0 * depth) ** 0.25
    t = bsz * seq
    cos2, sin2 = rope_tables(seq)
    log_g = retention_log_decay(heads)
    h = x.reshape(t, d)
    hb = h.astype(BF16)
    memb = mem.reshape(-1, d).astype(BF16)
    n_ab = 7 * GROUP_WIDTH
    n_cd = 6 * GROUP_WIDTH
    ab_w_in_t = jnp.swapaxes(ab_w_in, 1, 2)
    for layer in range(depth):
        i = layer // 2
        if layer % 2 == 0:
            z = matmul_nt_stacked(hb, ab_w_in_t, i, n_ab, tm=1024, tn=512, out_dtype=BF16)
            gates = gates_nt(hb, ab_w_in_t, i, n_ab, 2 * heads, _pad_lanes(ab_gate_b[i][None, :]), tm=1024)
            gain = jnp.concatenate([ab_ret_norm_g[i], ab_mlstm_norm_g[i]])
            yb = mixer_ab(z, gates, cos2, sin2, log_g, ab_conv_w, ab_conv_b, ab_wq, ab_wk, gain, i,
                          bsz=bsz, seq=seq, heads=heads)
            w_out = ab_w_out[i].astype(BF16)
        else:
            z = matmul_stacked(hb, cd_w_in, i, n_cd, tm=1024, tn=512, out_dtype=BF16)
            yb = mixer_cd(z, band_bias_table(cd_rel_bias[i]), bsz=bsz, seq=seq, heads=heads)
            w_out = cd_w_out[i].astype(BF16)
        h, hb = matmul_ln(yb, w_out, h, mix_ln_g[layer][None, :], mix_ln_b[layer][None, :], alpha)

        kvb = matmul_stacked(memb, xa_wkv, layer, xa_wkv.shape[2], tm=memb.shape[0], tn=512, out_dtype=BF16)
        h, hp, topw, topi, sel = xattn_ln(
            h, xa_wq[layer].astype(BF16), kvb, xa_wo[layer].astype(BF16),
            xa_ln_g[layer][None, :], xa_ln_b[layer][None, :],
            _pad_lanes(moe_router_w[layer]), _pad_lanes(moe_router_b[layer][None, :]), alpha, seq=seq)
        plan = moe_route_plan(topi, sel, tm=MOE_TM)
        y = moe_experts(hp, plan, moe_w_gate, moe_b_gate, moe_w_up, moe_b_up, moe_w_down, moe_b_down, layer, tm=MOE_TM)
        h, hb = moe_combine_ln(y, topw, h, moe_ln_g[layer][None, :], moe_ln_b[layer][None, :], alpha)
    return h.reshape(bsz, seq, d)
```

```python
import functools

import jax
import jax.numpy as jnp
from jax import lax
from jax.experimental import pallas as pl
from jax.experimental.pallas import tpu as pltpu

F32 = jnp.float32
BF16 = jnp.bfloat16

CHUNK = 64
HEAD_DIM = 128
GROUP_HEADS = 16
GROUP_WIDTH = GROUP_HEADS * HEAD_DIM
CONV_K = 4
PAST_CHUNKS = 8
REL_MAX = 2 * CHUNK
ROPE_BASE = 10000.0
XA_HEADS = 4
N_EXPERTS = 32
TOP_K = 4
SWIGLU_LIMIT = 7.0
SWIGLU_ALPHA = 1.702
LN_EPS = 1e-5
HN_EPS = 1e-6

LANES = 128
VMEM_LIMIT_BYTES = 58 * 1024 * 1024


def _params(*sem):
    return pltpu.CompilerParams(dimension_semantics=sem, vmem_limit_bytes=VMEM_LIMIT_BYTES)


def _mm_kernel(x_ref, w_ref, o_ref, wb_ref):
    @pl.when(pl.program_id(1) == 0)
    def _():
        wb_ref[...] = w_ref[...].astype(BF16)

    o_ref[...] = jnp.dot(x_ref[...], wb_ref[...], preferred_element_type=F32).astype(o_ref.dtype)


def matmul_stacked(x, w, layer, n_cols, *, tm, tn, out_dtype):
    m, k = x.shape
    return pl.pallas_call(
        _mm_kernel,
        grid=(n_cols // tn, m // tm),
        in_specs=[
            pl.BlockSpec((tm, k), lambda j, i: (i, 0)),
            pl.BlockSpec((None, k, tn), lambda j, i: (layer, 0, j)),
        ],
        out_specs=pl.BlockSpec((tm, tn), lambda j, i: (i, j)),
        out_shape=jax.ShapeDtypeStruct((m, n_cols), out_dtype),
        scratch_shapes=[pltpu.VMEM((k, tn), BF16)],
        compiler_params=_params("arbitrary", "arbitrary"),
        name="mm_in",
    )(x, w)


_NT = (((1,), (1,)), ((), ()))
_TN = (((0,), (0,)), ((), ()))


def _mm_nt_kernel(x_ref, w_ref, o_ref, wb_ref):
    @pl.when(pl.program_id(1) == 0)
    def _():
        wb_ref[...] = w_ref[...].astype(BF16)

    o_ref[...] = lax.dot_general(x_ref[...], wb_ref[...], _NT, preferred_element_type=F32).astype(o_ref.dtype)


def matmul_nt_stacked(x, wt, layer, n_cols, *, tm, tn, out_dtype):
    m, k = x.shape
    return pl.pallas_call(
        _mm_nt_kernel,
        grid=(n_cols // tn, m // tm),
        in_specs=[
            pl.BlockSpec((tm, k), lambda j, i: (i, 0)),
            pl.BlockSpec((None, tn, k), lambda j, i: (layer, j, 0)),
        ],
        out_specs=pl.BlockSpec((tm, tn), lambda j, i: (i, j)),
        out_shape=jax.ShapeDtypeStruct((m, n_cols), out_dtype),
        scratch_shapes=[pltpu.VMEM((tn, k), BF16)],
        compiler_params=_params("arbitrary", "arbitrary"),
        name="mm_in_nt",
    )(x, wt)


def _gates_kernel(x_ref, w_ref, b_ref, o_ref, wb_ref):
    @pl.when(pl.program_id(0) == 0)
    def _():
        wb_ref[...] = jnp.zeros_like(wb_ref)
        wb_ref[:w_ref.shape[0], :] = w_ref[...].astype(BF16)

    o_ref[...] = lax.dot_general(x_ref[...], wb_ref[...], _NT, preferred_element_type=F32) + b_ref[...]


def gates_nt(x, wt, layer, row0, n_rows, b_pad, *, tm):
    m, k = x.shape
    return pl.pallas_call(
        _gates_kernel,
        grid=(m // tm,),
        in_specs=[
            pl.BlockSpec((tm, k), lambda i: (i, 0)),
            pl.BlockSpec((None, n_rows, k), lambda i: (layer, row0 // n_rows, 0)),
            pl.BlockSpec((1, LANES), lambda i: (0, 0)),
        ],
        out_specs=pl.BlockSpec((tm, LANES), lambda i: (i, 0)),
        out_shape=jax.ShapeDtypeStruct((m, LANES), F32),
        scratch_shapes=[pltpu.VMEM((LANES, k), BF16)],
        compiler_params=_params("arbitrary"),
        name="gates",
    )(x, wt, b_pad)


def _pack_bf16_pair(y):
    half = y.shape[1] // 2
    hi = pltpu.bitcast(y[:, :half].astype(BF16).astype(F32), jnp.uint32)
    lo = pltpu.bitcast(y[:, half:].astype(BF16).astype(F32), jnp.uint32)
    return hi | (lo >> 16)


def _unpack_bf16_pair(w):
    return (pltpu.bitcast(w & jnp.uint32(0xFFFF0000), F32), pltpu.bitcast(w << 16, F32))


def _ln_rows(z_ref, g_ref, b_ref, of_ref, ob_ref, rows, packed=False):
    tm = z_ref.shape[0]

    def body(r, carry):
        sl = pl.ds(pl.multiple_of(r * rows, rows), rows)
        z = z_ref[sl, :]
        mu = jnp.mean(z, axis=-1, keepdims=True)
        zc = z - mu
        var = jnp.mean(zc * zc, axis=-1, keepdims=True)
        y = zc * lax.rsqrt(var + LN_EPS) * g_ref[...] + b_ref[...]
        of_ref[sl, :] = y
        ob_ref[sl, :] = _pack_bf16_pair(y) if packed else y.astype(BF16)
        return carry

    lax.fori_loop(0, tm // rows, body, 0, unroll=2)


def _mm_ln_kernel(x_ref, w_ref, h_ref, g_ref, b_ref, of_ref, ob_ref, *, nk, nj, tn, alpha):
    k = pl.program_id(1)
    j = pl.program_id(2)
    part = jnp.dot(x_ref[...], w_ref[...].astype(BF16), preferred_element_type=F32)
    for jj in range(nj):
        sl = slice(jj * tn, (jj + 1) * tn)

        @pl.when((j == jj) & (k == 0))
        def _():
            of_ref[:, sl] = alpha * h_ref[...] + part

        @pl.when((j == jj) & (k > 0))
        def _():
            of_ref[:, sl] += part

    @pl.when((k == nk - 1) & (j == nj - 1))
    def _():
        _ln_rows(of_ref, g_ref, b_ref, of_ref, ob_ref, 64)


def matmul_ln(x, w, layer, h, g, b, alpha, *, tm=512, tn=512, tk=4096):
    m, kdim = x.shape
    n = h.shape[1]
    tk = min(tk, kdim)
    nk, nj = kdim // tk, n // tn
    in_specs = [
        pl.BlockSpec((tm, tk), lambda i, k, j: (i, k)),
        pl.BlockSpec((None, tk, tn), lambda i, k, j: (layer, k, j)),
        pl.BlockSpec((tm, tn), lambda i, k, j: (i, j)),
        pl.BlockSpec((1, n), lambda i, k, j: (0, 0)),
        pl.BlockSpec((1, n), lambda i, k, j: (0, 0)),
    ]
    args = [x, w, h, g, b]
    return pl.pallas_call(
        functools.partial(_mm_ln_kernel, nk=nk, nj=nj, tn=tn, alpha=alpha),
        grid=(m // tm, nk, nj),
        in_specs=in_specs,
        out_specs=[
            pl.BlockSpec((tm, n), lambda i, k, j: (i, 0)),
            pl.BlockSpec((tm, n), lambda i, k, j: (i, 0)),
        ],
        out_shape=[jax.ShapeDtypeStruct((m, n), F32), jax.ShapeDtypeStruct((m, n), BF16)],
        compiler_params=_params("arbitrary", "arbitrary", "arbitrary"),
        name="mm_ln",
    )(*args)


def _xattn_kernel(h_ref, wq_ref, kv_ref, wo_ref, g_ref, b_ref, wrh_ref, wrl_ref, br_ref,
                  of_ref, ob_ref, topw_ref, topi_ref, sel_ref, *, alpha, heads, hd):
    q = jnp.dot(h_ref[...].astype(BF16), wq_ref[...], preferred_element_type=F32)
    scale = hd ** -0.5
    outs = []
    for hh in range(heads):
        qh = (q[:, hh * hd:(hh + 1) * hd] * scale).astype(BF16)
        kh = kv_ref[:, hh * hd:(hh + 1) * hd]
        vh = kv_ref[:, (heads + hh) * hd:(heads + hh + 1) * hd]
        s = lax.dot_general(qh, kh, (((1,), (1,)), ((), ())), preferred_element_type=F32)
        s = s - jnp.max(s, axis=-1, keepdims=True)
        p = jnp.exp(s)
        l = jnp.sum(p, axis=-1, keepdims=True)
        o = jnp.dot(p.astype(BF16), vh, preferred_element_type=F32) / l
        outs.append(o.astype(BF16))
    o_all = jnp.concatenate(outs, axis=-1)
    of_ref[...] = alpha * h_ref[...] + jnp.dot(o_all, wo_ref[...], preferred_element_type=F32)
    _ln_rows(of_ref, g_ref, b_ref, of_ref, ob_ref, 64, packed=True)
    topw_ref[...], topi_ref[...], sel_ref[...] = _route_top4(of_ref[...], wrh_ref[...], wrl_ref[...], br_ref[...])


def xattn_ln(h, wq_b, kv_b, wo_b, g, b, w_router, b_router, alpha, *, seq, tm=512):
    m, d = h.shape
    lane_spec = pl.BlockSpec((tm, LANES), lambda i: (i, 0))
    xw = wq_b.shape[1]
    n_mem = kv_b.shape[0] // (m // seq)
    per_b = seq // tm

    def const_spec(shape):
        return pl.BlockSpec(shape, lambda i: (0, 0), pipeline_mode=pl.Buffered(1))

    return pl.pallas_call(
        functools.partial(_xattn_kernel, alpha=alpha, heads=XA_HEADS, hd=xw // XA_HEADS),
        grid=(m // tm,),
        in_specs=[
            pl.BlockSpec((tm, d), lambda i: (i, 0)),
            const_spec((d, xw)),
            pl.BlockSpec((n_mem, 2 * xw), lambda i: (i // per_b, 0)),
            const_spec((xw, d)),
            const_spec((1, d)),
            const_spec((1, d)),
            const_spec((d, LANES)),
            const_spec((d, LANES)),
            const_spec((1, LANES)),
        ],
        out_specs=[pl.BlockSpec((tm, d), lambda i: (i, 0)), pl.BlockSpec((tm, d // 2), lambda i: (i, 0)),
                   lane_spec, lane_spec, lane_spec],
        out_shape=[jax.ShapeDtypeStruct((m, d), F32), jax.ShapeDtypeStruct((m, d // 2), jnp.uint32),
                   jax.ShapeDtypeStruct((m, LANES), F32), jax.ShapeDtypeStruct((m, LANES), jnp.int32),
                   jax.ShapeDtypeStruct((m, LANES), F32)],
        compiler_params=_params("arbitrary"),
        name="xattn_ln",
    )(h, wq_b, kv_b, wo_b, g, b, *_split_bf16(w_router), b_router)


def _route_top4(h, w_hi, w_lo, b):
    h_hi, h_lo = _split_bf16(h)
    both = jnp.dot(h_hi, jnp.concatenate([w_hi, w_lo], axis=1), preferred_element_type=F32)
    logits = both[:, :LANES] + both[:, LANES:] + jnp.dot(h_lo, w_hi, preferred_element_type=F32) + b
    lane = lax.broadcasted_iota(jnp.int32, logits.shape, 1)
    neg = jnp.float32(-jnp.inf)
    masked = jnp.where(lane < N_EXPERTS, logits, neg)
    top_vals, top_idx = [], []
    sel = jnp.zeros_like(logits)
    for _ in range(TOP_K):
        mval = jnp.max(masked, axis=-1, keepdims=True)
        idx = jnp.min(jnp.where(masked == mval, lane, LANES), axis=-1, keepdims=True)
        hot = lane == idx
        top_vals.append(mval)
        top_idx.append(idx)
        sel = jnp.where(hot, 1.0, sel)
        masked = jnp.where(hot, neg, masked)
    exps = [jnp.exp(v - top_vals[0]) for v in top_vals]
    denom = exps[0]
    for e in exps[1:]:
        denom = denom + e
    topw = jnp.zeros_like(logits)
    topi = jnp.zeros(logits.shape, jnp.int32)
    for k in range(TOP_K):
        topw = jnp.where(lane == k, exps[k] / denom, topw)
        topi = jnp.where(lane == k, top_idx[k], topi)
    return topw, topi, sel


MOE_TM = 256


def moe_route_plan(topi, sel, *, tm):
    t = topi.shape[0]
    n_tiles = t * TOP_K // tm + N_EXPERTS
    rows = n_tiles * tm
    top4 = topi[:, :TOP_K]
    seli = sel[:, :N_EXPERTS].astype(jnp.int32)
    pos_incl = jnp.cumsum(seli, axis=0)
    counts = pos_incl[-1]
    pos = pos_incl - seli
    tiles_e = (counts + tm - 1) // tm
    tile_end = jnp.cumsum(tiles_e)
    row_off = (tile_end - tiles_e) * tm
    dest = (row_off[top4] + jnp.take_along_axis(pos, top4, axis=1)).reshape(-1)
    slot_rows = (jnp.arange(t, dtype=jnp.int32)[:, None] + t * jnp.arange(TOP_K, dtype=jnp.int32)[None, :]).reshape(-1)
    ydst = (TOP_K * t + jnp.arange(rows, dtype=jnp.int32) % (2 * tm)).at[dest].set(slot_rows, unique_indices=True)
    tok = ydst % t
    tile_ids = jnp.arange(n_tiles, dtype=jnp.int32)
    tile_expert = jnp.minimum(jnp.sum(tile_end[None, :] <= tile_ids[:, None], axis=1), N_EXPERTS - 1)
    n_valid = tile_end[-1:].astype(jnp.int32)
    return (tok.reshape(n_tiles, 1, tm), ydst.reshape(n_tiles, 1, tm), tile_expert.astype(jnp.int32), n_valid)


def _moe_ffn_kernel(te_ref, nv_ref, tok0_ref, tok1_ref, tok2_ref, ydst_ref, h_hbm, wg_ref, wu_ref, wd_ref,
                    bg_ref, bu_ref, bd_ref, y_hbm, xg, og, wgu_b, wd_b, gsem, ssem, *, tm, ff):
    r = pl.program_id(0)
    nv = nv_ref[0]
    slot = lax.rem(r, 2)
    gslot = lax.rem(r, 3)

    def row_gather(idx_ref, s):
        for i in range(tm):
            pltpu.make_async_copy(h_hbm.at[pl.ds(idx_ref[0, i], 1), :], xg.at[s, pl.ds(i, 1), :], gsem.at[s]).start()

    def gather_wait(s):
        pltpu.make_async_copy(h_hbm.at[pl.ds(0, tm), :], xg.at[s], gsem.at[s]).wait()

    def scatter_wait(s):
        pltpu.make_async_copy(og.at[s], y_hbm.at[pl.ds(0, tm), :], ssem.at[s]).wait()

    @pl.when(r == 0)
    def _():
        row_gather(tok0_ref, 0)
        row_gather(tok1_ref, 1)
        og[1] = jnp.zeros(og.shape[1:], og.dtype)
        base = y_hbm.shape[0] - 2 * tm
        for part in range(2):
            fill = pltpu.make_async_copy(og.at[1], y_hbm.at[pl.ds(base + part * tm, tm), :], ssem.at[1])
            fill.start()
            fill.wait()

    @pl.when(r < nv)
    def _():
        gather_wait(gslot)

        @pl.when(r >= 2)
        def _():
            scatter_wait(slot)

        @pl.when((r == 0) | (te_ref[r] != te_ref[jnp.maximum(r - 1, 0)]))
        def _():
            wgu_b[:, :ff] = wg_ref[...].astype(BF16)
            wgu_b[:, ff:] = wu_ref[...].astype(BF16)
            wd_b[...] = wd_ref[...].astype(BF16)

        x_hi, x_lo = _unpack_bf16_pair(xg[gslot])
        half = x_hi.shape[1]
        gu = (jnp.dot(x_hi.astype(BF16), wgu_b[:half, :], preferred_element_type=F32)
              + jnp.dot(x_lo.astype(BF16), wgu_b[half:, :], preferred_element_type=F32))
        g = jnp.minimum(gu[:, :ff] + bg_ref[...], SWIGLU_LIMIT)
        u = jnp.clip(gu[:, ff:] + bu_ref[...], -SWIGLU_LIMIT, SWIGLU_LIMIT)
        act = (g * jax.nn.sigmoid(SWIGLU_ALPHA * g) * (u + 1.0)).astype(BF16)
        res = _pack_bf16_pair(jnp.dot(act, wd_b[...], preferred_element_type=F32) + bd_ref[...])
        row_gather(tok2_ref, lax.rem(r + 2, 3))
        og[slot] = res
        for i in range(tm):
            pltpu.make_async_copy(og.at[slot, pl.ds(i, 1), :], y_hbm.at[pl.ds(ydst_ref[0, i], 1), :],
                                  ssem.at[slot]).start(priority=i % 2)

        @pl.when(r == nv - 1)
        def _():
            gather_wait(lax.rem(r + 1, 3))
            gather_wait(lax.rem(r + 2, 3))
            scatter_wait(slot)

            @pl.when(r >= 1)
            def _():
                scatter_wait(1 - slot)


def moe_experts(hp, plan, w_gate, b_gate, w_up, b_up, w_down, b_down, layer, *, tm):
    tok, ydst, tile_expert, n_valid = plan
    t = hp.shape[0]
    d = 2 * hp.shape[1]
    n_tiles = tok.shape[0]
    n_e, ff = w_gate.shape[1], w_gate.shape[3]
    bg = b_gate.reshape(b_gate.shape[0], n_e, 1, ff)
    bu = b_up.reshape(b_up.shape[0], n_e, 1, ff)
    bd = b_down.reshape(b_down.shape[0], n_e, 1, d)

    def expert(r, te, nv):
        return te[jnp.minimum(r, nv[0] - 1)]

    smem_blk = functools.partial(pl.BlockSpec, (None, 1, tm), memory_space=pltpu.SMEM)
    grid_spec = pltpu.PrefetchScalarGridSpec(
        num_scalar_prefetch=2,
        grid=(n_tiles,),
        in_specs=[
            smem_blk(lambda r, te, nv: (jnp.minimum(r, nv[0] - 1), 0, 0)),
            smem_blk(lambda r, te, nv: (jnp.minimum(r + 1, nv[0] - 1), 0, 0)),
            smem_blk(lambda r, te, nv: (jnp.minimum(r + 2, nv[0] - 1), 0, 0)),
            smem_blk(lambda r, te, nv: (r, 0, 0)),
            pl.BlockSpec(memory_space=pl.ANY),
            pl.BlockSpec((None, None, d, ff), lambda r, te, nv: (layer, expert(r, te, nv), 0, 0)),
            pl.BlockSpec((None, None, d, ff), lambda r, te, nv: (layer, expert(r, te, nv), 0, 0)),
            pl.BlockSpec((None, None, ff, d), lambda r, te, nv: (layer, expert(r, te, nv), 0, 0)),
            pl.BlockSpec((None, None, 1, ff), lambda r, te, nv: (layer, expert(r, te, nv), 0, 0)),
            pl.BlockSpec((None, None, 1, ff), lambda r, te, nv: (layer, expert(r, te, nv), 0, 0)),
            pl.BlockSpec((None, None, 1, d), lambda r, te, nv: (layer, expert(r, te, nv), 0, 0)),
        ],
        out_specs=pl.BlockSpec(memory_space=pl.ANY),
        scratch_shapes=[
            pltpu.VMEM((3, tm, d // 2), jnp.uint32),
            pltpu.VMEM((2, tm, d // 2), jnp.uint32),
            pltpu.VMEM((d, 2 * ff), BF16),
            pltpu.VMEM((ff, d), BF16),
            pltpu.SemaphoreType.DMA((3,)),
            pltpu.SemaphoreType.DMA((2,)),
        ],
    )
    return pl.pallas_call(
        functools.partial(_moe_ffn_kernel, tm=tm, ff=ff),
        grid_spec=grid_spec,
        out_shape=jax.ShapeDtypeStruct((TOP_K * t + 2 * tm, d // 2), jnp.uint32),
        compiler_params=_params("arbitrary"),
        name="moe_experts",
    )(tile_expert, n_valid, tok, tok, tok, ydst, hp, w_gate, w_up, w_down, bg, bu, bd)


def _moe_combine_kernel(y0_ref, y1_ref, y2_ref, y3_ref, w_ref, h_ref, g_ref, b_ref, of_ref, ob_ref, *, alpha, rows):
    tm = h_ref.shape[0]
    y_refs = (y0_ref, y1_ref, y2_ref, y3_ref)

    def body(rr, carry):
        sl = pl.ds(pl.multiple_of(rr * rows, rows), rows)
        w = w_ref[sl, :]
        hrow = h_ref[sl, :]
        half = hrow.shape[1] // 2
        z_hi = alpha * hrow[:, :half]
        z_lo = alpha * hrow[:, half:]
        for k, y_ref in enumerate(y_refs):
            y_hi, y_lo = _unpack_bf16_pair(y_ref[sl, :])
            z_hi = z_hi + w[:, k:k + 1] * y_hi
            z_lo = z_lo + w[:, k:k + 1] * y_lo
        z = jnp.concatenate([z_hi, z_lo], axis=1)
        mu = jnp.mean(z, axis=-1, keepdims=True)
        zc = z - mu
        var = jnp.mean(zc * zc, axis=-1, keepdims=True)
        y = zc * lax.rsqrt(var + LN_EPS) * g_ref[...] + b_ref[...]
        of_ref[sl, :] = y
        ob_ref[sl, :] = y.astype(BF16)
        return carry

    lax.fori_loop(0, tm // rows, body, 0)


def moe_combine_ln(y, topw, h, g, b, alpha, *, tm=128):
    t, d = h.shape
    nb = t // tm

    def y_spec(k):
        return pl.BlockSpec((tm, d // 2), lambda i: (k * nb + i, 0))

    row_spec = pl.BlockSpec((tm, d), lambda i: (i, 0))
    return pl.pallas_call(
        functools.partial(_moe_combine_kernel, alpha=alpha, rows=32),
        grid=(nb,),
        in_specs=[y_spec(0), y_spec(1), y_spec(2), y_spec(3),
                  pl.BlockSpec((tm, LANES), lambda i: (i, 0)), row_spec,
                  pl.BlockSpec((1, d), lambda i: (0, 0)), pl.BlockSpec((1, d), lambda i: (0, 0))],
        out_specs=[row_spec, row_spec],
        out_shape=[jax.ShapeDtypeStruct((t, d), F32), jax.ShapeDtypeStruct((t, d), BF16)],
        compiler_params=_params("arbitrary"),
        name="moe_combine_ln",
    )(y, y, y, y, topw, h, g, b)


NEG = -1e30
MIX_CHUNK = 256
BAND_BLOCK = 256
SB_TQ = 512
SB_TK = 256

def _head_norm_rows(x, gain):
    mu = jnp.mean(x, axis=-1, keepdims=True)
    xc = x - mu
    var = jnp.mean(xc * xc, axis=-1, keepdims=True)
    return xc * lax.rsqrt(var + HN_EPS) * gain


def _log_sigmoid(x):
    return jnp.minimum(x, 0.0) - jnp.log(1.0 + jnp.exp(-jnp.abs(x)))


def _split_bf16(x):
    hi = x.astype(BF16)
    lo = (x - hi.astype(F32)).astype(BF16)
    return hi, lo


def _lane_select(x, lane_idx):
    lane = lax.broadcasted_iota(jnp.int32, x.shape, 1)
    col = jnp.sum(jnp.where(lane == lane_idx, x, 0.0), axis=-1, keepdims=True)
    return jnp.broadcast_to(col, x.shape)


def _retention_body(q_ref, k_ref, v_ref, g_ref, cos_ref, sin_ref, lg_ref, gain_ref, o_ref,
                    qs_ref, ks_ref, st_ref, *, scale):
    seq = q_ref.shape[0]
    L = MIX_CHUNK
    half = HEAD_DIM // 2
    cos = cos_ref[...]
    sin = sin_ref[...]
    q = q_ref[...].astype(F32)
    k = k_ref[...].astype(F32)
    qs_ref[...] = (q * cos + pltpu.roll(q, half, 1) * sin).astype(BF16)
    ks_ref[...] = ((k * cos + pltpu.roll(k, half, 1) * sin) * scale).astype(BF16)

    lg = lg_ref[...]
    ri = lax.broadcasted_iota(jnp.int32, (L, L), 0)
    ci = lax.broadcasted_iota(jnp.int32, (L, L), 1)
    intra = jnp.where(ri >= ci, jnp.exp(lg * jnp.maximum(ri - ci, 0).astype(F32)), 0.0)
    rr = lax.broadcasted_iota(jnp.int32, (L, HEAD_DIM), 0).astype(F32)
    lg_d = lg[:, :HEAD_DIM]
    q_dec = jnp.exp(lg_d * (rr + 1.0))
    k_dec = jnp.exp(lg_d * (L - 1.0 - rr))
    c_dec = jnp.exp(lg_d * float(L))
    gain = gain_ref[...]
    st_ref[...] = jnp.zeros_like(st_ref)

    def chunk(c, carry):
        sl = pl.ds(pl.multiple_of(c * L, L), L)
        qc = qs_ref[sl, :]
        kc = ks_ref[sl, :]
        vc = v_ref[sl, :]
        state = st_ref[:, :HEAD_DIM]
        att = lax.dot_general(qc, kc, _NT, preferred_element_type=F32) * intra
        o = (jnp.dot(att.astype(BF16), vc, preferred_element_type=F32)
             + jnp.dot((qc.astype(F32) * q_dec).astype(BF16), state.astype(BF16), preferred_element_type=F32))
        st_ref[:, :HEAD_DIM] = state * c_dec + lax.dot_general(
            (kc.astype(F32) * k_dec).astype(BF16), vc, _TN, preferred_element_type=F32)
        gv = g_ref[sl, :].astype(F32)
        o_ref[sl, :] = (_head_norm_rows(o, gain) * (gv * jax.nn.sigmoid(gv))).astype(o_ref.dtype)
        return carry

    lax.fori_loop(0, seq // L, chunk, 0, unroll=True)


def _mlstm_body(u_ref, v_ref, og_ref, gates_ref, cw_ref, cb_ref, wq_ref, wk_ref, gain_ref, o_ref,
                qs_ref, ks_ref, st_ref, m_ref, *, head, heads, scale):
    seq = u_ref.shape[0]
    L = MIX_CHUNK
    d = HEAD_DIM
    x = u_ref[...].astype(F32)
    row = lax.broadcasted_iota(jnp.int32, x.shape, 0)
    cw = cw_ref[...]
    y = x * cw[CONV_K - 1:CONV_K, :] + cb_ref[...]
    for sh in range(1, CONV_K):
        xs = jnp.where(row >= sh, pltpu.roll(x, sh, 0), 0.0)
        y = y + xs * cw[CONV_K - 1 - sh:CONV_K - sh, :]
    ub = (y * jax.nn.sigmoid(y)).astype(BF16)
    qs_ref[...] = jnp.dot(ub, wq_ref[...].astype(BF16), preferred_element_type=F32).astype(BF16)
    ks_ref[...] = (jnp.dot(ub, wk_ref[...].astype(BF16), preferred_element_type=F32) * scale).astype(BF16)

    ri = lax.broadcasted_iota(jnp.int32, (L, L), 0)
    ci = lax.broadcasted_iota(jnp.int32, (L, L), 1)
    causal = ri >= ci
    tri = jnp.where(causal, 1.0, 0.0).astype(BF16)
    ones_v = jnp.ones((L, d), BF16)
    gain = gain_ref[...]
    st_ref[...] = jnp.zeros_like(st_ref)
    m_ref[...] = jnp.zeros_like(m_ref)

    def chunk(c, carry):
        sl = pl.ds(pl.multiple_of(c * L, L), L)
        qc = qs_ref[sl, :]
        kc = ks_ref[sl, :]
        v_ext = jnp.concatenate([v_ref[sl, :], ones_v], axis=1)
        gts = gates_ref[sl, :]
        ic = _lane_select(gts, head)
        lf = _log_sigmoid(_lane_select(gts, heads + head))
        lf_hi, lf_lo = _split_bf16(lf)
        bcum = (jnp.dot(tri, lf_hi, preferred_element_type=F32)
                + jnp.dot(tri, lf_lo, preferred_element_type=F32))
        m_st = m_ref[...]
        src = jnp.transpose(ic - bcum)[:1, :]
        bcum2 = jnp.concatenate([bcum, bcum], axis=1)
        log_intra = jnp.where(causal, bcum2 + src, NEG)
        m_intra = jnp.max(log_intra, axis=-1, keepdims=True)
        log_cross = bcum + m_st
        m_row = jnp.maximum(log_cross, m_intra)
        m_row2 = jnp.concatenate([m_row, m_row], axis=1)
        w_intra = jnp.exp(log_intra - m_row2)
        w_cross = jnp.exp(log_cross - m_row)
        w_cross2 = jnp.concatenate([w_cross, w_cross], axis=1)
        qk = lax.dot_general(qc, kc, _NT, preferred_element_type=F32) * w_intra
        state = st_ref[...]
        res = (jnp.dot(qk.astype(BF16), v_ext, preferred_element_type=F32)
               + w_cross2 * jnp.dot(qc, state.astype(BF16), preferred_element_type=F32))
        num = res[:, :d]
        den = res[:, d:]
        hh = num / jnp.maximum(jnp.abs(den), jnp.exp(-m_row))
        og = og_ref[sl, :].astype(F32)
        o_ref[sl, :] = _head_norm_rows(hh * jax.nn.sigmoid(og), gain).astype(o_ref.dtype)
        b_last = bcum[L - 1:L, :]
        log_state = b_last - bcum + ic
        m_new = jnp.maximum(b_last + m_st, jnp.max(log_state, axis=0, keepdims=True))
        decay = jnp.exp(b_last + m_st - m_new)
        kw = (kc.astype(F32) * jnp.exp(log_state - m_new)).astype(BF16)
        decay2 = jnp.concatenate([decay, decay], axis=1)
        st_ref[...] = decay2 * state + lax.dot_general(kw, v_ext, _TN, preferred_element_type=F32)
        m_ref[...] = m_new
        return carry

    lax.fori_loop(0, seq // L, chunk, 0, unroll=True)


def _mixer_ab_kernel(a0_ref, a1_ref, a2_ref, a3_ref, gates_ref, cos_ref, sin_ref, lg_ref, cw_ref, cb_ref,
                     wq_ref, wk_ref, gain_ref, o_ref, qs_ref, ks_ref, st_ref, m_ref, *, heads, scale):
    g = pl.program_id(1)

    @pl.when(g < heads)
    def _():
        _retention_body(a0_ref, a1_ref, a2_ref, a3_ref, cos_ref, sin_ref, lg_ref, gain_ref, o_ref,
                        qs_ref, ks_ref, st_ref, scale=scale)

    @pl.when(g >= heads)
    def _():
        _mlstm_body(a0_ref, a1_ref, a2_ref, gates_ref, cw_ref, cb_ref, wq_ref, wk_ref, gain_ref, o_ref,
                    qs_ref, ks_ref, st_ref, m_ref, head=g - heads, heads=heads, scale=scale)


def mixer_ab(z, gates, cos2, sin2, log_g, conv_w, conv_b, wq_m, wk_m, gain, layer, *, bsz, seq, heads):
    t = z.shape[0]
    d = HEAD_DIM
    L = MIX_CHUNK

    def col(base_ret, base_ml):
        def index(b, g):
            is_ml = g // heads
            return (b, (1 - is_ml) * (base_ret * heads + g) + is_ml * (base_ml * heads + g - heads))
        return pl.BlockSpec((seq, d), index)

    def ml_head(g):
        return jnp.maximum(g - heads, 0)

    conv_w4 = conv_w.reshape(conv_w.shape[0], CONV_K, heads, d).transpose(0, 2, 1, 3)
    conv_b4 = conv_b.reshape(conv_b.shape[0], heads, 1, d)
    gain4 = gain.reshape(2 * heads, 1, d)
    return pl.pallas_call(
        functools.partial(_mixer_ab_kernel, heads=heads, scale=d ** -0.5),
        grid=(bsz, 2 * heads),
        in_specs=[
            col(0, 4), col(1, 5), col(2, 6), col(3, 6),
            pl.BlockSpec((seq, LANES), lambda b, g: (b, 0)),
            pl.BlockSpec((seq, d), lambda b, g: (0, 0)),
            pl.BlockSpec((seq, d), lambda b, g: (0, 0)),
            pl.BlockSpec((None, 1, L), lambda b, g: (jnp.minimum(g, heads - 1), 0, 0)),
            pl.BlockSpec((None, None, CONV_K, d), lambda b, g: (layer, ml_head(g), 0, 0)),
            pl.BlockSpec((None, None, 1, d), lambda b, g: (layer, ml_head(g), 0, 0)),
            pl.BlockSpec((None, None, d, d), lambda b, g: (layer, ml_head(g), 0, 0)),
            pl.BlockSpec((None, None, d, d), lambda b, g: (layer, ml_head(g), 0, 0)),
            pl.BlockSpec((None, 1, d), lambda b, g: (g, 0, 0)),
        ],
        out_specs=pl.BlockSpec((seq, d), lambda b, g: (b, g)),
        out_shape=jax.ShapeDtypeStruct((t, 2 * heads * d), BF16),
        scratch_shapes=[
            pltpu.VMEM((seq, d), BF16),
            pltpu.VMEM((seq, d), BF16),
            pltpu.VMEM((d, 2 * d), F32),
            pltpu.VMEM((1, d), F32),
        ],
        compiler_params=_params("arbitrary", "arbitrary"),
        name="mixer_ab",
    )(z, z, z, z, gates, cos2, sin2, log_g, conv_w4, conv_b4, wq_m, wk_m, gain4)


def _band_body(q_ref, k_ref, v_ref, bias_ref, o_ref, *, scale):
    seq = q_ref.shape[0]
    bq = BAND_BLOCK
    n_back = bias_ref.shape[0]

    def block(i, carry):
        sl = pl.ds(pl.multiple_of(i * bq, bq), bq)
        q = q_ref[sl, :]
        scores, vals = [], []
        for dlt in range(n_back):
            ks = pl.ds(pl.multiple_of(jnp.maximum(i - dlt, 0) * bq, bq), bq)
            s = lax.dot_general(q, k_ref[ks, :], _NT, preferred_element_type=F32) * scale + bias_ref[dlt]
            scores.append(jnp.where(i - dlt >= 0, s, NEG))
            vals.append(v_ref[ks, :])
        m = jnp.max(scores[0], axis=-1, keepdims=True)
        for s in scores[1:]:
            m = jnp.maximum(m, jnp.max(s, axis=-1, keepdims=True))
        acc = jnp.zeros((bq, HEAD_DIM), F32)
        l = jnp.zeros((bq, 1), F32)
        for s, vv in zip(scores, vals):
            p = jnp.exp(s - m)
            l = l + jnp.sum(p, axis=-1, keepdims=True)
            acc = acc + jnp.dot(p.astype(BF16), vv, preferred_element_type=F32)
        o_ref[sl, :] = (acc / l).astype(o_ref.dtype)
        return carry

    lax.fori_loop(0, seq // bq, block, 0, unroll=True)


def _stick_breaking_body(q_ref, k_ref, v_ref, o_ref, acc_ref, r_ref, *, scale):
    seq = q_ref.shape[0]
    tq, tk = SB_TQ, SB_TK
    n_sub = tq // tk
    ri = lax.broadcasted_iota(jnp.int32, (tk, tk), 0)
    ci = lax.broadcasted_iota(jnp.int32, (tk, tk), 1)
    suffix = jnp.where(ri > ci, 1.0, 0.0).astype(BF16)
    strict = (lax.broadcasted_iota(jnp.int32, (tq, tq), 1)
              < lax.broadcasted_iota(jnp.int32, (tq, tq), 0))

    def one_group(q, k0, masked):
        kb = k_ref[pl.ds(k0, tq), :]
        vb = v_ref[pl.ds(k0, tq), :]
        z = lax.dot_general(q, kb, _NT, preferred_element_type=F32) * scale
        sp = jnp.maximum(z, 0.0) + jnp.log(1.0 + jnp.exp(-jnp.abs(z)))
        ls_pos = z - sp
        if masked:
            sp = jnp.where(strict, sp, 0.0)
        sp_b = sp.astype(BF16)
        r = r_ref[...]
        pieces = []
        for s in reversed(range(n_sub)):
            sl = slice(s * tk, (s + 1) * tk)
            between = jnp.dot(sp_b[:, sl], suffix, preferred_element_type=F32)
            pieces.append(ls_pos[:, sl] - between - jnp.concatenate([r] * (tk // LANES), axis=1))
            r = r + jnp.sum(sp[:, sl], axis=-1, keepdims=True)
        p = jnp.exp(jnp.concatenate(pieces[::-1], axis=1))
        if masked:
            p = jnp.where(strict, p, 0.0)
        acc_ref[...] += jnp.dot(p.astype(BF16), vb, preferred_element_type=F32)
        r_ref[...] = r

    for i in range(seq // tq):
        q0 = i * tq
        q = q_ref[pl.ds(q0, tq), :]
        acc_ref[...] = jnp.zeros_like(acc_ref)
        r_ref[...] = jnp.zeros_like(r_ref)
        one_group(q, q0, True)

        def below(jj, c, q=q, i=i):
            one_group(q, pl.multiple_of((i - 1 - jj) * tq, tq), False)
            return c

        lax.fori_loop(0, i, below, 0, unroll=True)
        o_ref[pl.ds(q0, tq), :] = acc_ref[...].astype(o_ref.dtype)


def _mixer_cd_kernel(q_ref, k_ref, v_ref, bias_ref, o_ref, acc_ref, r_ref, *, heads, scale):
    g = pl.program_id(1)

    @pl.when(g < heads)
    def _():
        _band_body(q_ref, k_ref, v_ref, bias_ref, o_ref, scale=scale)

    @pl.when(g >= heads)
    def _():
        _stick_breaking_body(q_ref, k_ref, v_ref, o_ref, acc_ref, r_ref, scale=scale)


def _band_bias_kernel(rrow_ref, o_ref):
    bq = BAND_BLOCK
    shift = CHUNK.bit_length() - 1
    qo = lax.broadcasted_iota(jnp.int32, (bq, bq), 0)
    ck = jnp.right_shift(lax.broadcasted_iota(jnp.int32, (bq, bq), 1), shift)
    for dlt in range(o_ref.shape[0]):
        x = jnp.broadcast_to(rrow_ref[dlt], (bq, 2 * bq))
        toep = pltpu.roll(x, 0, 1, stride=1, stride_axis=0)[:, :bq]
        cq = jnp.right_shift(qo + bq * dlt, shift)
        allowed = (ck <= cq) & (ck >= cq - PAST_CHUNKS)
        o_ref[dlt] = jnp.where(allowed, toep, NEG)


def band_bias_table(rel_bias):
    bq = BAND_BLOCK
    heads = rel_bias.shape[0]
    n_back = PAST_CHUNKS * CHUNK // bq + 1
    m = jnp.arange(2 * bq)
    key_minus_query = jnp.where(m < bq, m, m - 2 * bq)
    dist = bq * jnp.arange(n_back)[:, None] - key_minus_query[None, :]
    idx = jnp.clip(dist, -(CHUNK - 1), REL_MAX) + (CHUNK - 1)
    rrow = rel_bias.astype(F32)[:, idx].reshape(heads, n_back, 1, 2 * bq)
    return pl.pallas_call(
        _band_bias_kernel,
        grid=(heads,),
        in_specs=[pl.BlockSpec((None, n_back, 1, 2 * bq), lambda hh: (hh, 0, 0, 0))],
        out_specs=pl.BlockSpec((None, n_back, bq, bq), lambda hh: (hh, 0, 0, 0)),
        out_shape=jax.ShapeDtypeStruct((heads, n_back, bq, bq), F32),
        compiler_params=_params("arbitrary"),
        name="band_bias",
    )(rrow)


def mixer_cd(z, bias_tab, *, bsz, seq, heads):
    t = z.shape[0]
    d = HEAD_DIM

    def col(which):
        return pl.BlockSpec((seq, d), lambda b, g: (b, (3 * (g // heads) + which) * heads + g % heads))

    return pl.pallas_call(
        functools.partial(_mixer_cd_kernel, heads=heads, scale=d ** -0.5),
        grid=(bsz, 2 * heads),
        in_specs=[
            col(0), col(1), col(2),
            pl.BlockSpec((None,) + bias_tab.shape[1:], lambda b, g: (jnp.minimum(g, heads - 1), 0, 0, 0)),
        ],
        out_specs=pl.BlockSpec((seq, d), lambda b, g: (b, g)),
        out_shape=jax.ShapeDtypeStruct((t, 2 * heads * d), BF16),
        scratch_shapes=[pltpu.VMEM((SB_TQ, d), F32), pltpu.VMEM((SB_TQ, LANES), F32)],
        compiler_params=_params("arbitrary", "arbitrary"),
        name="mixer_cd",
    )(z, z, z, bias_tab)


def rope_tables(seq_len):
    pos = jnp.arange(seq_len, dtype=F32)
    inv_freq = ROPE_BASE ** (-jnp.arange(0, HEAD_DIM, 2, dtype=F32) / HEAD_DIM)
    ang = pos[:, None] * inv_freq[None, :]
    cos, sin = jnp.cos(ang), jnp.sin(ang)
    return jnp.concatenate([cos, cos], axis=1), jnp.concatenate([-sin, sin], axis=1)


def retention_log_decay(heads):
    lg = jnp.log1p(-jnp.exp2(-(5.0 + jnp.arange(heads, dtype=F32))))
    return jnp.broadcast_to(lg[:, None, None], (heads, 1, MIX_CHUNK))


def _pad_lanes(a, value=0.0):
    return jnp.pad(a, ((0, 0), (0, LANES - a.shape[1])), constant_values=value)


def kernel(x, mem, ab_w_in, ab_gate_b, ab_conv_w, ab_conv_b, ab_wq, ab_wk, ab_ret_norm_g, ab_mlstm_norm_g, ab_w_out, cd_w_in, cd_rel_bias, cd_w_out, mix_ln_g, mix_ln_b, xa_wq, xa_wkv, xa_wo, xa_ln_g, xa_ln_b, moe_router_w, moe_router_b, moe_w_gate, moe_b_gate, moe_w_up, moe_b_up, moe_w_down, moe_b_down, moe_ln_g, moe_ln_b):
    bsz, seq, d = x.shape
    depth = mix_ln_g.shape[0]
    heads = GROUP_HEADS
    alpha = (2.0 * depth) ** 0.25
    t = bsz * seq
    cos2, sin2 = rope_tables(seq)
    log_g = retention_log_decay(heads)
    h = x.reshape(t, d)
    hb = h.astype(BF16)
    memb = mem.reshape(-1, d).astype(BF16)
    n_ab = 7 * GROUP_WIDTH
    n_cd = 6 * GROUP_WIDTH
    ab_w_in_t = jnp.swapaxes(ab_w_in, 1, 2)
    ab_w_out_b = ab_w_out.astype(BF16)
    cd_w_out_b = cd_w_out.astype(BF16)
    for layer in range(depth):
        i = layer // 2
        if layer % 2 == 0:
            z = matmul_nt_stacked(hb, ab_w_in_t, i, n_ab, tm=1024, tn=512, out_dtype=BF16)
            gates = gates_nt(hb, ab_w_in_t, i, n_ab, 2 * heads, _pad_lanes(ab_gate_b[i][None, :]), tm=1024)
            gain = jnp.concatenate([ab_ret_norm_g[i], ab_mlstm_norm_g[i]])
            yb = mixer_ab(z, gates, cos2, sin2, log_g, ab_conv_w, ab_conv_b, ab_wq, ab_wk, gain, i,
                          bsz=bsz, seq=seq, heads=heads)
            w_out = ab_w_out_b
        else:
            z = matmul_stacked(hb, cd_w_in, i, n_cd, tm=1024, tn=512, out_dtype=BF16)
            yb = mixer_cd(z, band_bias_table(cd_rel_bias[i]), bsz=bsz, seq=seq, heads=heads)
            w_out = cd_w_out_b
        h, hb = matmul_ln(yb, w_out, i, h, mix_ln_g[layer][None, :], mix_ln_b[layer][None, :], alpha)

        kvb = matmul_stacked(memb, xa_wkv, layer, xa_wkv.shape[2], tm=memb.shape[0], tn=512, out_dtype=BF16)
        h, hp, topw, topi, sel = xattn_ln(
            h, xa_wq[layer].astype(BF16), kvb, xa_wo[layer].astype(BF16),
            xa_ln_g[layer][None, :], xa_ln_b[layer][None, :],
            _pad_lanes(moe_router_w[layer]), _pad_lanes(moe_router_b[layer][None, :]), alpha, seq=seq)
        plan = moe_route_plan(topi, sel, tm=MOE_TM)
        y = moe_experts(hp, plan, moe_w_gate, moe_b_gate, moe_w_up, moe_b_up, moe_w_down, moe_b_down, layer, tm=MOE_TM)
        h, hb = moe_combine_ln(y, topw, h, moe_ln_g[layer][None, :], moe_ln_b[layer][None, :], alpha)
    return h.reshape(bsz, seq, d)
```

```python
import functools

import jax
import jax.numpy as jnp
from jax import lax
from jax.experimental import pallas as pl
from jax.experimental.pallas import tpu as pltpu

F32 = jnp.float32
BF16 = jnp.bfloat16

CHUNK = 64
HEAD_DIM = 128
GROUP_HEADS = 16
GROUP_WIDTH = GROUP_HEADS * HEAD_DIM
CONV_K = 4
PAST_CHUNKS = 8
REL_MAX = 2 * CHUNK
ROPE_BASE = 10000.0
XA_HEADS = 4
N_EXPERTS = 32
TOP_K = 4
SWIGLU_LIMIT = 7.0
SWIGLU_ALPHA = 1.702
LN_EPS = 1e-5
HN_EPS = 1e-6

LANES = 128
VMEM_LIMIT_BYTES = 58 * 1024 * 1024


def _params(*sem):
    return pltpu.CompilerParams(dimension_semantics=sem, vmem_limit_bytes=VMEM_LIMIT_BYTES)


def _mm_kernel(x_ref, w_ref, o_ref, wb_ref):
    @pl.when(pl.program_id(1) == 0)
    def _():
        wb_ref[...] = w_ref[...].astype(BF16)

    o_ref[...] = jnp.dot(x_ref[...], wb_ref[...], preferred_element_type=F32).astype(o_ref.dtype)


def matmul_stacked(x, w, layer, n_cols, *, tm, tn, out_dtype):
    m, k = x.shape
    return pl.pallas_call(
        _mm_kernel,
        grid=(n_cols // tn, m // tm),
        in_specs=[
            pl.BlockSpec((tm, k), lambda j, i: (i, 0)),
            pl.BlockSpec((None, k, tn), lambda j, i: (layer, 0, j)),
        ],
        out_specs=pl.BlockSpec((tm, tn), lambda j, i: (i, j)),
        out_shape=jax.ShapeDtypeStruct((m, n_cols), out_dtype),
        scratch_shapes=[pltpu.VMEM((k, tn), BF16)],
        compiler_params=_params("arbitrary", "arbitrary"),
        name="mm_in",
    )(x, w)


_NT = (((1,), (1,)), ((), ()))
_TN = (((0,), (0,)), ((), ()))


def _mm_nt_kernel(x_ref, w_ref, o_ref, wb_ref):
    @pl.when(pl.program_id(1) == 0)
    def _():
        wb_ref[...] = w_ref[...].astype(BF16)

    o_ref[...] = lax.dot_general(x_ref[...], wb_ref[...], _NT, preferred_element_type=F32).astype(o_ref.dtype)


def matmul_nt_stacked(x, wt, layer, n_cols, *, tm, tn, out_dtype):
    m, k = x.shape
    return pl.pallas_call(
        _mm_nt_kernel,
        grid=(n_cols // tn, m // tm),
        in_specs=[
            pl.BlockSpec((tm, k), lambda j, i: (i, 0)),
            pl.BlockSpec((None, tn, k), lambda j, i: (layer, j, 0)),
        ],
        out_specs=pl.BlockSpec((tm, tn), lambda j, i: (i, j)),
        out_shape=jax.ShapeDtypeStruct((m, n_cols), out_dtype),
        scratch_shapes=[pltpu.VMEM((tn, k), BF16)],
        compiler_params=_params("arbitrary", "arbitrary"),
        name="mm_in_nt",
    )(x, wt)


def _gates_kernel(x_ref, w_ref, b_ref, o_ref, wb_ref):
    @pl.when(pl.program_id(0) == 0)
    def _():
        wb_ref[...] = jnp.zeros_like(wb_ref)
        wb_ref[:w_ref.shape[0], :] = w_ref[...].astype(BF16)

    o_ref[...] = lax.dot_general(x_ref[...], wb_ref[...], _NT, preferred_element_type=F32) + b_ref[...]


def gates_nt(x, wt, layer, row0, n_rows, b_pad, *, tm):
    m, k = x.shape
    return pl.pallas_call(
        _gates_kernel,
        grid=(m // tm,),
        in_specs=[
            pl.BlockSpec((tm, k), lambda i: (i, 0)),
            pl.BlockSpec((None, n_rows, k), lambda i: (layer, row0 // n_rows, 0)),
            pl.BlockSpec((1, LANES), lambda i: (0, 0)),
        ],
        out_specs=pl.BlockSpec((tm, LANES), lambda i: (i, 0)),
        out_shape=jax.ShapeDtypeStruct((m, LANES), F32),
        scratch_shapes=[pltpu.VMEM((LANES, k), BF16)],
        compiler_params=_params("arbitrary"),
        name="gates",
    )(x, wt, b_pad)


def _pack_bf16_pair(y):
    half = y.shape[1] // 2
    hi = pltpu.bitcast(y[:, :half].astype(BF16).astype(F32), jnp.uint32)
    lo = pltpu.bitcast(y[:, half:].astype(BF16).astype(F32), jnp.uint32)
    return hi | (lo >> 16)


def _unpack_bf16_pair(w):
    return (pltpu.bitcast(w & jnp.uint32(0xFFFF0000), F32), pltpu.bitcast(w << 16, F32))


def _ln_rows(z_ref, g_ref, b_ref, of_ref, ob_ref, rows, packed=False):
    tm = z_ref.shape[0]

    def body(r, carry):
        sl = pl.ds(pl.multiple_of(r * rows, rows), rows)
        z = z_ref[sl, :]
        mu = jnp.mean(z, axis=-1, keepdims=True)
        zc = z - mu
        var = jnp.mean(zc * zc, axis=-1, keepdims=True)
        y = zc * lax.rsqrt(var + LN_EPS) * g_ref[...] + b_ref[...]
        of_ref[sl, :] = y
        ob_ref[sl, :] = _pack_bf16_pair(y) if packed else y.astype(BF16)
        return carry

    lax.fori_loop(0, tm // rows, body, 0, unroll=2)


def _mm_ln_kernel(x_ref, w_ref, h_ref, g_ref, b_ref, of_ref, ob_ref, *, nk, nj, tn, alpha):
    k = pl.program_id(1)
    j = pl.program_id(2)
    part = jnp.dot(x_ref[...], w_ref[...].astype(BF16), preferred_element_type=F32)
    for jj in range(nj):
        sl = slice(jj * tn, (jj + 1) * tn)

        @pl.when((j == jj) & (k == 0))
        def _():
            of_ref[:, sl] = alpha * h_ref[...] + part

        @pl.when((j == jj) & (k > 0))
        def _():
            of_ref[:, sl] += part

    @pl.when((k == nk - 1) & (j == nj - 1))
    def _():
        _ln_rows(of_ref, g_ref, b_ref, of_ref, ob_ref, 64)


def matmul_ln(x, w, layer, h, g, b, alpha, *, tm=512, tn=1024, tk=4096):
    m, kdim = x.shape
    n = h.shape[1]
    tk = min(tk, kdim)
    nk, nj = kdim // tk, n // tn
    in_specs = [
        pl.BlockSpec((tm, tk), lambda i, k, j: (i, k)),
        pl.BlockSpec((None, tk, tn), lambda i, k, j: (layer, k, j)),
        pl.BlockSpec((tm, tn), lambda i, k, j: (i, j)),
        pl.BlockSpec((1, n), lambda i, k, j: (0, 0)),
        pl.BlockSpec((1, n), lambda i, k, j: (0, 0)),
    ]
    args = [x, w, h, g, b]
    return pl.pallas_call(
        functools.partial(_mm_ln_kernel, nk=nk, nj=nj, tn=tn, alpha=alpha),
        grid=(m // tm, nk, nj),
        in_specs=in_specs,
        out_specs=[
            pl.BlockSpec((tm, n), lambda i, k, j: (i, 0)),
            pl.BlockSpec((tm, n), lambda i, k, j: (i, 0)),
        ],
        out_shape=[jax.ShapeDtypeStruct((m, n), F32), jax.ShapeDtypeStruct((m, n), BF16)],
        compiler_params=_params("arbitrary", "arbitrary", "arbitrary"),
        name="mm_ln",
    )(*args)


def _xattn_kernel(h_ref, wq_ref, kv_ref, wo_ref, g_ref, b_ref, wrh_ref, wrl_ref, br_ref,
                  of_ref, ob_ref, topw_ref, topi_ref, sel_ref, *, alpha, heads, hd):
    q = jnp.dot(h_ref[...].astype(BF16), wq_ref[...], preferred_element_type=F32)
    scale = hd ** -0.5
    outs = []
    for hh in range(heads):
        qh = (q[:, hh * hd:(hh + 1) * hd] * scale).astype(BF16)
        kh = kv_ref[:, hh * hd:(hh + 1) * hd]
        vh = kv_ref[:, (heads + hh) * hd:(heads + hh + 1) * hd]
        s = lax.dot_general(qh, kh, (((1,), (1,)), ((), ())), preferred_element_type=F32)
        s = s - jnp.max(s, axis=-1, keepdims=True)
        p = jnp.exp(s)
        l = jnp.sum(p, axis=-1, keepdims=True)
        o = jnp.dot(p.astype(BF16), vh, preferred_element_type=F32) / l
        outs.append(o.astype(BF16))
    o_all = jnp.concatenate(outs, axis=-1)
    of_ref[...] = alpha * h_ref[...] + jnp.dot(o_all, wo_ref[...], preferred_element_type=F32)
    _ln_rows(of_ref, g_ref, b_ref, of_ref, ob_ref, 64, packed=True)
    topw_ref[...], topi_ref[...], sel_ref[...] = _route_top4(of_ref[...], wrh_ref[...], wrl_ref[...], br_ref[...])


def xattn_ln(h, wq_b, kv_b, wo_b, g, b, w_router, b_router, alpha, *, seq, tm=512):
    m, d = h.shape
    lane_spec = pl.BlockSpec((tm, LANES), lambda i: (i, 0))
    xw = wq_b.shape[1]
    n_mem = kv_b.shape[0] // (m // seq)
    per_b = seq // tm

    def const_spec(shape):
        return pl.BlockSpec(shape, lambda i: (0, 0), pipeline_mode=pl.Buffered(1))

    return pl.pallas_call(
        functools.partial(_xattn_kernel, alpha=alpha, heads=XA_HEADS, hd=xw // XA_HEADS),
        grid=(m // tm,),
        in_specs=[
            pl.BlockSpec((tm, d), lambda i: (i, 0)),
            const_spec((d, xw)),
            pl.BlockSpec((n_mem, 2 * xw), lambda i: (i // per_b, 0)),
            const_spec((xw, d)),
            const_spec((1, d)),
            const_spec((1, d)),
            const_spec((d, LANES)),
            const_spec((d, LANES)),
            const_spec((1, LANES)),
        ],
        out_specs=[pl.BlockSpec((tm, d), lambda i: (i, 0)), pl.BlockSpec((tm, d // 2), lambda i: (i, 0)),
                   lane_spec, lane_spec, lane_spec],
        out_shape=[jax.ShapeDtypeStruct((m, d), F32), jax.ShapeDtypeStruct((m, d // 2), jnp.uint32),
                   jax.ShapeDtypeStruct((m, LANES), F32), jax.ShapeDtypeStruct((m, LANES), jnp.int32),
                   jax.ShapeDtypeStruct((m, LANES), F32)],
        compiler_params=_params("arbitrary"),
        name="xattn_ln",
    )(h, wq_b, kv_b, wo_b, g, b, *_split_bf16(w_router), b_router)


def _route_top4(h, w_hi, w_lo, b):
    h_hi, h_lo = _split_bf16(h)
    both = jnp.dot(h_hi, jnp.concatenate([w_hi, w_lo], axis=1), preferred_element_type=F32)
    logits = both[:, :LANES] + both[:, LANES:] + jnp.dot(h_lo, w_hi, preferred_element_type=F32) + b
    lane = lax.broadcasted_iota(jnp.int32, logits.shape, 1)
    neg = jnp.float32(-jnp.inf)
    masked = jnp.where(lane < N_EXPERTS, logits, neg)
    top_vals, top_idx = [], []
    sel = jnp.zeros_like(logits)
    for _ in range(TOP_K):
        mval = jnp.max(masked, axis=-1, keepdims=True)
        idx = jnp.min(jnp.where(masked == mval, lane, LANES), axis=-1, keepdims=True)
        hot = lane == idx
        top_vals.append(mval)
        top_idx.append(idx)
        sel = jnp.where(hot, 1.0, sel)
        masked = jnp.where(hot, neg, masked)
    exps = [jnp.exp(v - top_vals[0]) for v in top_vals]
    denom = exps[0]
    for e in exps[1:]:
        denom = denom + e
    topw = jnp.zeros_like(logits)
    topi = jnp.zeros(logits.shape, jnp.int32)
    for k in range(TOP_K):
        topw = jnp.where(lane == k, exps[k] / denom, topw)
        topi = jnp.where(lane == k, top_idx[k], topi)
    return topw, topi, sel


MOE_TM = 256


def moe_route_plan(topi, sel, *, tm):
    t = topi.shape[0]
    n_tiles = t * TOP_K // tm + N_EXPERTS
    rows = n_tiles * tm
    top4 = topi[:, :TOP_K]
    seli = sel[:, :N_EXPERTS].astype(jnp.int32)
    pos_incl = jnp.cumsum(seli, axis=0)
    counts = pos_incl[-1]
    pos = pos_incl - seli
    tiles_e = (counts + tm - 1) // tm
    tile_end = jnp.cumsum(tiles_e)
    row_off = (tile_end - tiles_e) * tm
    dest = (row_off[top4] + jnp.take_along_axis(pos, top4, axis=1)).reshape(-1)
    slot_rows = (jnp.arange(t, dtype=jnp.int32)[:, None] + t * jnp.arange(TOP_K, dtype=jnp.int32)[None, :]).reshape(-1)
    ydst = (TOP_K * t + jnp.arange(rows, dtype=jnp.int32) % (2 * tm)).at[dest].set(slot_rows, unique_indices=True)
    tok = ydst % t
    tile_ids = jnp.arange(n_tiles, dtype=jnp.int32)
    tile_expert = jnp.minimum(jnp.sum(tile_end[None, :] <= tile_ids[:, None], axis=1), N_EXPERTS - 1)
    n_valid = tile_end[-1:].astype(jnp.int32)
    return (tok.reshape(n_tiles, 1, tm), ydst.reshape(n_tiles, 1, tm), tile_expert.astype(jnp.int32), n_valid)


def _moe_ffn_kernel(te_ref, nv_ref, tok0_ref, tok1_ref, tok2_ref, ydst_ref, h_hbm, wg_ref, wu_ref, wd_ref,
                    bg_ref, bu_ref, bd_ref, y_hbm, xg, og, wgu_b, wd_b, gsem, ssem, *, tm, ff):
    r = pl.program_id(0)
    nv = nv_ref[0]
    slot = lax.rem(r, 2)
    gslot = lax.rem(r, 3)

    def row_gather(idx_ref, s):
        for i in range(tm):
            pltpu.make_async_copy(h_hbm.at[pl.ds(idx_ref[0, i], 1), :], xg.at[s, pl.ds(i, 1), :], gsem.at[s]).start()

    def gather_wait(s):
        pltpu.make_async_copy(h_hbm.at[pl.ds(0, tm), :], xg.at[s], gsem.at[s]).wait()

    def scatter_wait(s):
        pltpu.make_async_copy(og.at[s], y_hbm.at[pl.ds(0, tm), :], ssem.at[s]).wait()

    @pl.when(r == 0)
    def _():
        row_gather(tok0_ref, 0)
        row_gather(tok1_ref, 1)
        og[1] = jnp.zeros(og.shape[1:], og.dtype)
        base = y_hbm.shape[0] - 2 * tm
        for part in range(2):
            fill = pltpu.make_async_copy(og.at[1], y_hbm.at[pl.ds(base + part * tm, tm), :], ssem.at[1])
            fill.start()
            fill.wait()

    @pl.when(r < nv)
    def _():
        gather_wait(gslot)

        @pl.when(r >= 2)
        def _():
            scatter_wait(slot)

        @pl.when((r == 0) | (te_ref[r] != te_ref[jnp.maximum(r - 1, 0)]))
        def _():
            wgu_b[:, :ff] = wg_ref[...].astype(BF16)
            wgu_b[:, ff:] = wu_ref[...].astype(BF16)
            wd_b[...] = wd_ref[...].astype(BF16)

        x_hi, x_lo = _unpack_bf16_pair(xg[gslot])
        half = x_hi.shape[1]
        gu = (jnp.dot(x_hi.astype(BF16), wgu_b[:half, :], preferred_element_type=F32)
              + jnp.dot(x_lo.astype(BF16), wgu_b[half:, :], preferred_element_type=F32))
        g = jnp.minimum(gu[:, :ff] + bg_ref[...], SWIGLU_LIMIT)
        u = jnp.clip(gu[:, ff:] + bu_ref[...], -SWIGLU_LIMIT, SWIGLU_LIMIT)
        act = (g * jax.nn.sigmoid(SWIGLU_ALPHA * g) * (u + 1.0)).astype(BF16)
        res = _pack_bf16_pair(jnp.dot(act, wd_b[...], preferred_element_type=F32) + bd_ref[...])
        row_gather(tok2_ref, lax.rem(r + 2, 3))
        og[slot] = res
        for i in range(tm):
            pltpu.make_async_copy(og.at[slot, pl.ds(i, 1), :], y_hbm.at[pl.ds(ydst_ref[0, i], 1), :],
                                  ssem.at[slot]).start(priority=i % 2)

        @pl.when(r == nv - 1)
        def _():
            gather_wait(lax.rem(r + 1, 3))
            gather_wait(lax.rem(r + 2, 3))
            scatter_wait(slot)

            @pl.when(r >= 1)
            def _():
                scatter_wait(1 - slot)


def moe_experts(hp, plan, w_gate, b_gate, w_up, b_up, w_down, b_down, layer, *, tm):
    tok, ydst, tile_expert, n_valid = plan
    t = hp.shape[0]
    d = 2 * hp.shape[1]
    n_tiles = tok.shape[0]
    n_e, ff = w_gate.shape[1], w_gate.shape[3]
    bg = b_gate.reshape(b_gate.shape[0], n_e, 1, ff)
    bu = b_up.reshape(b_up.shape[0], n_e, 1, ff)
    bd = b_down.reshape(b_down.shape[0], n_e, 1, d)

    def expert(r, te, nv):
        return te[jnp.minimum(r, nv[0] - 1)]

    smem_blk = functools.partial(pl.BlockSpec, (None, 1, tm), memory_space=pltpu.SMEM)
    grid_spec = pltpu.PrefetchScalarGridSpec(
        num_scalar_prefetch=2,
        grid=(n_tiles,),
        in_specs=[
            smem_blk(lambda r, te, nv: (jnp.minimum(r, nv[0] - 1), 0, 0)),
            smem_blk(lambda r, te, nv: (jnp.minimum(r + 1, nv[0] - 1), 0, 0)),
            smem_blk(lambda r, te, nv: (jnp.minimum(r + 2, nv[0] - 1), 0, 0)),
            smem_blk(lambda r, te, nv: (r, 0, 0)),
            pl.BlockSpec(memory_space=pl.ANY),
            pl.BlockSpec((None, None, d, ff), lambda r, te, nv: (layer, expert(r, te, nv), 0, 0)),
            pl.BlockSpec((None, None, d, ff), lambda r, te, nv: (layer, expert(r, te, nv), 0, 0)),
            pl.BlockSpec((None, None, ff, d), lambda r, te, nv: (layer, expert(r, te, nv), 0, 0)),
            pl.BlockSpec((None, None, 1, ff), lambda r, te, nv: (layer, expert(r, te, nv), 0, 0)),
            pl.BlockSpec((None, None, 1, ff), lambda r, te, nv: (layer, expert(r, te, nv), 0, 0)),
            pl.BlockSpec((None, None, 1, d), lambda r, te, nv: (layer, expert(r, te, nv), 0, 0)),
        ],
        out_specs=pl.BlockSpec(memory_space=pl.ANY),
        scratch_shapes=[
            pltpu.VMEM((3, tm, d // 2), jnp.uint32),
            pltpu.VMEM((2, tm, d // 2), jnp.uint32),
            pltpu.VMEM((d, 2 * ff), BF16),
            pltpu.VMEM((ff, d), BF16),
            pltpu.SemaphoreType.DMA((3,)),
            pltpu.SemaphoreType.DMA((2,)),
        ],
    )
    return pl.pallas_call(
        functools.partial(_moe_ffn_kernel, tm=tm, ff=ff),
        grid_spec=grid_spec,
        out_shape=jax.ShapeDtypeStruct((TOP_K * t + 2 * tm, d // 2), jnp.uint32),
        compiler_params=_params("arbitrary"),
        name="moe_experts",
    )(tile_expert, n_valid, tok, tok, tok, ydst, hp, w_gate, w_up, w_down, bg, bu, bd)


def _moe_combine_kernel(y0_ref, y1_ref, y2_ref, y3_ref, w_ref, h_ref, g_ref, b_ref, of_ref, ob_ref, *, alpha, rows):
    tm = h_ref.shape[0]
    y_refs = (y0_ref, y1_ref, y2_ref, y3_ref)

    def body(rr, carry):
        sl = pl.ds(pl.multiple_of(rr * rows, rows), rows)
        w = w_ref[sl, :]
        hrow = h_ref[sl, :]
        half = hrow.shape[1] // 2
        z_hi = alpha * hrow[:, :half]
        z_lo = alpha * hrow[:, half:]
        for k, y_ref in enumerate(y_refs):
            y_hi, y_lo = _unpack_bf16_pair(y_ref[sl, :])
            z_hi = z_hi + w[:, k:k + 1] * y_hi
            z_lo = z_lo + w[:, k:k + 1] * y_lo
        z = jnp.concatenate([z_hi, z_lo], axis=1)
        mu = jnp.mean(z, axis=-1, keepdims=True)
        zc = z - mu
        var = jnp.mean(zc * zc, axis=-1, keepdims=True)
        y = zc * lax.rsqrt(var + LN_EPS) * g_ref[...] + b_ref[...]
        of_ref[sl, :] = y
        ob_ref[sl, :] = y.astype(BF16)
        return carry

    lax.fori_loop(0, tm // rows, body, 0)


def moe_combine_ln(y, topw, h, g, b, alpha, *, tm=128):
    t, d = h.shape
    nb = t // tm

    def y_spec(k):
        return pl.BlockSpec((tm, d // 2), lambda i: (k * nb + i, 0))

    row_spec = pl.BlockSpec((tm, d), lambda i: (i, 0))
    return pl.pallas_call(
        functools.partial(_moe_combine_kernel, alpha=alpha, rows=32),
        grid=(nb,),
        in_specs=[y_spec(0), y_spec(1), y_spec(2), y_spec(3),
                  pl.BlockSpec((tm, LANES), lambda i: (i, 0)), row_spec,
                  pl.BlockSpec((1, d), lambda i: (0, 0)), pl.BlockSpec((1, d), lambda i: (0, 0))],
        out_specs=[row_spec, row_spec],
        out_shape=[jax.ShapeDtypeStruct((t, d), F32), jax.ShapeDtypeStruct((t, d), BF16)],
        compiler_params=_params("arbitrary"),
        name="moe_combine_ln",
    )(y, y, y, y, topw, h, g, b)


NEG = -1e30
LOG2E = 1.4426950408889634
MIX_CHUNK = 256
BAND_BLOCK = 256
SB_TQ = 512
SB_TK = 256

def _head_norm_rows(x, gain):
    mu = jnp.mean(x, axis=-1, keepdims=True)
    xc = x - mu
    var = jnp.mean(xc * xc, axis=-1, keepdims=True)
    return xc * lax.rsqrt(var + HN_EPS) * gain


def _log_sigmoid(x):
    return jnp.minimum(x, 0.0) - jnp.log(1.0 + jnp.exp(-jnp.abs(x)))


def _split_bf16(x):
    hi = x.astype(BF16)
    lo = (x - hi.astype(F32)).astype(BF16)
    return hi, lo


def _lane_select(x, lane_idx):
    lane = lax.broadcasted_iota(jnp.int32, x.shape, 1)
    col = jnp.sum(jnp.where(lane == lane_idx, x, 0.0), axis=-1, keepdims=True)
    return jnp.broadcast_to(col, x.shape)


def _retention_body(q_ref, k_ref, v_ref, g_ref, cos_ref, sin_ref, lg_ref, gain_ref, o_ref,
                    qs_ref, ks_ref, st_ref, *, scale):
    seq = q_ref.shape[0]
    L = MIX_CHUNK
    half = HEAD_DIM // 2
    cos = cos_ref[...]
    sin = sin_ref[...]
    q = q_ref[...].astype(F32)
    k = k_ref[...].astype(F32)
    qs_ref[...] = (q * cos + pltpu.roll(q, half, 1) * sin).astype(BF16)
    ks_ref[...] = ((k * cos + pltpu.roll(k, half, 1) * sin) * scale).astype(BF16)

    lg = lg_ref[...]
    ri = lax.broadcasted_iota(jnp.int32, (L, L), 0)
    ci = lax.broadcasted_iota(jnp.int32, (L, L), 1)
    intra = jnp.where(ri >= ci, jnp.exp(lg * jnp.maximum(ri - ci, 0).astype(F32)), 0.0)
    rr = lax.broadcasted_iota(jnp.int32, (L, HEAD_DIM), 0).astype(F32)
    lg_d = lg[:, :HEAD_DIM]
    q_dec = jnp.exp(lg_d * (rr + 1.0))
    k_dec = jnp.exp(lg_d * (L - 1.0 - rr))
    c_dec = jnp.exp(lg_d * float(L))
    gain = gain_ref[...]
    st_ref[...] = jnp.zeros_like(st_ref)

    def chunk(c, carry):
        sl = pl.ds(pl.multiple_of(c * L, L), L)
        qc = qs_ref[sl, :]
        kc = ks_ref[sl, :]
        vc = v_ref[sl, :]
        state = st_ref[:, :HEAD_DIM]
        att = lax.dot_general(qc, kc, _NT, preferred_element_type=F32) * intra
        o = (jnp.dot(att.astype(BF16), vc, preferred_element_type=F32)
             + jnp.dot((qc.astype(F32) * q_dec).astype(BF16), state.astype(BF16), preferred_element_type=F32))
        st_ref[:, :HEAD_DIM] = state * c_dec + lax.dot_general(
            (kc.astype(F32) * k_dec).astype(BF16), vc, _TN, preferred_element_type=F32)
        gv = g_ref[sl, :].astype(F32)
        o_ref[sl, :] = (_head_norm_rows(o, gain) * (gv * jax.nn.sigmoid(gv))).astype(o_ref.dtype)
        return carry

    lax.fori_loop(0, seq // L, chunk, 0, unroll=True)


def _mlstm_body(u_ref, v_ref, og_ref, gates_ref, cw_ref, cb_ref, wq_ref, wk_ref, gain_ref, o_ref,
                qs_ref, ks_ref, st_ref, m_ref, *, head, heads, scale):
    seq = u_ref.shape[0]
    L = MIX_CHUNK
    d = HEAD_DIM
    x = u_ref[...].astype(F32)
    row = lax.broadcasted_iota(jnp.int32, x.shape, 0)
    cw = cw_ref[...]
    y = x * cw[CONV_K - 1:CONV_K, :] + cb_ref[...]
    for sh in range(1, CONV_K):
        xs = jnp.where(row >= sh, pltpu.roll(x, sh, 0), 0.0)
        y = y + xs * cw[CONV_K - 1 - sh:CONV_K - sh, :]
    ub = (y * jax.nn.sigmoid(y)).astype(BF16)
    qs_ref[...] = jnp.dot(ub, wq_ref[...].astype(BF16), preferred_element_type=F32).astype(BF16)
    ks_ref[...] = (jnp.dot(ub, wk_ref[...].astype(BF16), preferred_element_type=F32) * scale).astype(BF16)

    ri = lax.broadcasted_iota(jnp.int32, (L, L), 0)
    ci = lax.broadcasted_iota(jnp.int32, (L, L), 1)
    causal = ri >= ci
    tri = jnp.where(causal, 1.0, 0.0).astype(BF16)
    ones_v = jnp.ones((L, d), BF16)
    gain = gain_ref[...]
    st_ref[...] = jnp.zeros_like(st_ref)
    m_ref[...] = jnp.zeros_like(m_ref)

    def chunk(c, carry):
        sl = pl.ds(pl.multiple_of(c * L, L), L)
        qc = qs_ref[sl, :]
        kc = ks_ref[sl, :]
        v_ext = jnp.concatenate([v_ref[sl, :], ones_v], axis=1)
        gts = gates_ref[sl, :]
        ic = _lane_select(gts, head)
        lf = _log_sigmoid(_lane_select(gts, heads + head))
        lf_hi, lf_lo = _split_bf16(lf)
        bcum = (jnp.dot(tri, lf_hi, preferred_element_type=F32)
                + jnp.dot(tri, lf_lo, preferred_element_type=F32))
        m_st = m_ref[...]
        src = jnp.transpose(ic - bcum)[:1, :]
        bcum2 = jnp.concatenate([bcum, bcum], axis=1)
        log_intra = jnp.where(causal, bcum2 + src, NEG)
        m_intra = jnp.max(log_intra, axis=-1, keepdims=True)
        log_cross = bcum + m_st
        m_row = jnp.maximum(log_cross, m_intra)
        m_row2 = jnp.concatenate([m_row, m_row], axis=1)
        w_intra = jnp.exp(log_intra - m_row2)
        w_cross = jnp.exp(log_cross - m_row)
        w_cross2 = jnp.concatenate([w_cross, w_cross], axis=1)
        qk = lax.dot_general(qc, kc, _NT, preferred_element_type=F32) * w_intra
        state = st_ref[...]
        res = (jnp.dot(qk.astype(BF16), v_ext, preferred_element_type=F32)
               + w_cross2 * jnp.dot(qc, state.astype(BF16), preferred_element_type=F32))
        num = res[:, :d]
        den = res[:, d:]
        hh = num / jnp.maximum(jnp.abs(den), jnp.exp(-m_row))
        og = og_ref[sl, :].astype(F32)
        o_ref[sl, :] = _head_norm_rows(hh * jax.nn.sigmoid(og), gain).astype(o_ref.dtype)
        b_last = bcum[L - 1:L, :]
        log_state = b_last - bcum + ic
        m_new = jnp.maximum(b_last + m_st, jnp.max(log_state, axis=0, keepdims=True))
        decay = jnp.exp(b_last + m_st - m_new)
        kw = (kc.astype(F32) * jnp.exp(log_state - m_new)).astype(BF16)
        decay2 = jnp.concatenate([decay, decay], axis=1)
        st_ref[...] = decay2 * state + lax.dot_general(kw, v_ext, _TN, preferred_element_type=F32)
        m_ref[...] = m_new
        return carry

    lax.fori_loop(0, seq // L, chunk, 0, unroll=True)


def _mixer_ab_kernel(a0_ref, a1_ref, a2_ref, a3_ref, gates_ref, cos_ref, sin_ref, lg_ref, cw_ref, cb_ref,
                     wq_ref, wk_ref, gain_ref, o_ref, qs_ref, ks_ref, st_ref, m_ref, *, heads, scale):
    g = pl.program_id(1)

    @pl.when(g < heads)
    def _():
        _retention_body(a0_ref, a1_ref, a2_ref, a3_ref, cos_ref, sin_ref, lg_ref, gain_ref, o_ref,
                        qs_ref, ks_ref, st_ref, scale=scale)

    @pl.when(g >= heads)
    def _():
        _mlstm_body(a0_ref, a1_ref, a2_ref, gates_ref, cw_ref, cb_ref, wq_ref, wk_ref, gain_ref, o_ref,
                    qs_ref, ks_ref, st_ref, m_ref, head=g - heads, heads=heads, scale=scale)


def mixer_ab(z, gates, cos2, sin2, log_g, conv_w, conv_b, wq_m, wk_m, gain, layer, *, bsz, seq, heads):
    t = z.shape[0]
    d = HEAD_DIM
    L = MIX_CHUNK

    def col(base_ret, base_ml):
        def index(b, g):
            is_ml = g // heads
            return (b, (1 - is_ml) * (base_ret * heads + g) + is_ml * (base_ml * heads + g - heads))
        return pl.BlockSpec((seq, d), index)

    def ml_head(g):
        return jnp.maximum(g - heads, 0)

    conv_w4 = conv_w.reshape(conv_w.shape[0], CONV_K, heads, d).transpose(0, 2, 1, 3)
    conv_b4 = conv_b.reshape(conv_b.shape[0], heads, 1, d)
    gain4 = gain.reshape(2 * heads, 1, d)
    return pl.pallas_call(
        functools.partial(_mixer_ab_kernel, heads=heads, scale=d ** -0.5),
        grid=(bsz, 2 * heads),
        in_specs=[
            col(0, 4), col(1, 5), col(2, 6), col(3, 6),
            pl.BlockSpec((seq, LANES), lambda b, g: (b, 0)),
            pl.BlockSpec((seq, d), lambda b, g: (0, 0)),
            pl.BlockSpec((seq, d), lambda b, g: (0, 0)),
            pl.BlockSpec((None, 1, L), lambda b, g: (jnp.minimum(g, heads - 1), 0, 0)),
            pl.BlockSpec((None, None, CONV_K, d), lambda b, g: (layer, ml_head(g), 0, 0)),
            pl.BlockSpec((None, None, 1, d), lambda b, g: (layer, ml_head(g), 0, 0)),
            pl.BlockSpec((None, None, d, d), lambda b, g: (layer, ml_head(g), 0, 0)),
            pl.BlockSpec((None, None, d, d), lambda b, g: (layer, ml_head(g), 0, 0)),
            pl.BlockSpec((None, 1, d), lambda b, g: (g, 0, 0)),
        ],
        out_specs=pl.BlockSpec((seq, d), lambda b, g: (b, g)),
        out_shape=jax.ShapeDtypeStruct((t, 2 * heads * d), BF16),
        scratch_shapes=[
            pltpu.VMEM((seq, d), BF16),
            pltpu.VMEM((seq, d), BF16),
            pltpu.VMEM((d, 2 * d), F32),
            pltpu.VMEM((1, d), F32),
        ],
        compiler_params=_params("arbitrary", "arbitrary"),
        name="mixer_ab",
    )(z, z, z, z, gates, cos2, sin2, log_g, conv_w4, conv_b4, wq_m, wk_m, gain4)


def _band_body(q_ref, k_ref, v_ref, bias_ref, o_ref, *, scale):
    seq = q_ref.shape[0]
    bq = BAND_BLOCK
    n_back = bias_ref.shape[0]

    def block(i, carry):
        sl = pl.ds(pl.multiple_of(i * bq, bq), bq)
        q = q_ref[sl, :]
        scores, vals = [], []
        for dlt in range(n_back):
            ks = pl.ds(pl.multiple_of(jnp.maximum(i - dlt, 0) * bq, bq), bq)
            s = lax.dot_general(q, k_ref[ks, :], _NT, preferred_element_type=F32) * (scale * LOG2E) + bias_ref[dlt]
            scores.append(jnp.where(i - dlt >= 0, s, NEG))
            vals.append(v_ref[ks, :])
        m = jnp.max(scores[0], axis=-1, keepdims=True)
        for s in scores[1:]:
            m = jnp.maximum(m, jnp.max(s, axis=-1, keepdims=True))
        acc = jnp.zeros((bq, HEAD_DIM), F32)
        l = jnp.zeros((bq, 1), F32)
        for s, vv in zip(scores, vals):
            p = jnp.exp2(s - m)
            l = l + jnp.sum(p, axis=-1, keepdims=True)
            acc = acc + jnp.dot(p.astype(BF16), vv, preferred_element_type=F32)
        o_ref[sl, :] = (acc / l).astype(o_ref.dtype)
        return carry

    lax.fori_loop(0, seq // bq, block, 0, unroll=True)


def _stick_breaking_body(q_ref, k_ref, v_ref, o_ref, acc_ref, r_ref, *, scale):
    seq = q_ref.shape[0]
    tq, tk = SB_TQ, SB_TK
    n_sub = tq // tk
    ri = lax.broadcasted_iota(jnp.int32, (tk, tk), 0)
    ci = lax.broadcasted_iota(jnp.int32, (tk, tk), 1)
    suffix = jnp.where(ri > ci, 1.0, 0.0).astype(BF16)
    strict = (lax.broadcasted_iota(jnp.int32, (tq, tq), 1)
              < lax.broadcasted_iota(jnp.int32, (tq, tq), 0))

    def one_group(q, k0, masked):
        kb = k_ref[pl.ds(k0, tq), :]
        vb = v_ref[pl.ds(k0, tq), :]
        z = lax.dot_general(q, kb, _NT, preferred_element_type=F32) * (scale * LOG2E)
        sp = jnp.maximum(z, 0.0) + jnp.log2(1.0 + jnp.exp2(-jnp.abs(z)))
        ls_pos = z - sp
        if masked:
            sp = jnp.where(strict, sp, 0.0)
        sp_b = sp.astype(BF16)
        r = r_ref[...]
        pieces = []
        for s in reversed(range(n_sub)):
            sl = slice(s * tk, (s + 1) * tk)
            between = jnp.dot(sp_b[:, sl], suffix, preferred_element_type=F32)
            pieces.append(ls_pos[:, sl] - between - jnp.concatenate([r] * (tk // LANES), axis=1))
            r = r + jnp.sum(sp[:, sl], axis=-1, keepdims=True)
        p = jnp.exp2(jnp.concatenate(pieces[::-1], axis=1))
        if masked:
            p = jnp.where(strict, p, 0.0)
        acc_ref[...] += jnp.dot(p.astype(BF16), vb, preferred_element_type=F32)
        r_ref[...] = r

    for i in range(seq // tq):
        q0 = i * tq
        q = q_ref[pl.ds(q0, tq), :]
        acc_ref[...] = jnp.zeros_like(acc_ref)
        r_ref[...] = jnp.zeros_like(r_ref)
        one_group(q, q0, True)

        def below(jj, c, q=q, i=i):
            one_group(q, pl.multiple_of((i - 1 - jj) * tq, tq), False)
            return c

        lax.fori_loop(0, i, below, 0, unroll=True)
        o_ref[pl.ds(q0, tq), :] = acc_ref[...].astype(o_ref.dtype)


def _mixer_cd_kernel(q_ref, k_ref, v_ref, bias_ref, o_ref, acc_ref, r_ref, *, heads, scale):
    g = pl.program_id(1)

    @pl.when(g < heads)
    def _():
        _band_body(q_ref, k_ref, v_ref, bias_ref, o_ref, scale=scale)

    @pl.when(g >= heads)
    def _():
        _stick_breaking_body(q_ref, k_ref, v_ref, o_ref, acc_ref, r_ref, scale=scale)


def _band_bias_kernel(rrow_ref, o_ref):
    bq = BAND_BLOCK
    shift = CHUNK.bit_length() - 1
    qo = lax.broadcasted_iota(jnp.int32, (bq, bq), 0)
    ck = jnp.right_shift(lax.broadcasted_iota(jnp.int32, (bq, bq), 1), shift)
    for dlt in range(o_ref.shape[0]):
        x = jnp.broadcast_to(rrow_ref[dlt], (bq, 2 * bq))
        toep = pltpu.roll(x, 0, 1, stride=1, stride_axis=0)[:, :bq]
        cq = jnp.right_shift(qo + bq * dlt, shift)
        allowed = (ck <= cq) & (ck >= cq - PAST_CHUNKS)
        o_ref[dlt] = jnp.where(allowed, toep * LOG2E, NEG)


def band_bias_table(rel_bias):
    bq = BAND_BLOCK
    heads = rel_bias.shape[0]
    n_back = PAST_CHUNKS * CHUNK // bq + 1
    m = jnp.arange(2 * bq)
    key_minus_query = jnp.where(m < bq, m, m - 2 * bq)
    dist = bq * jnp.arange(n_back)[:, None] - key_minus_query[None, :]
    idx = jnp.clip(dist, -(CHUNK - 1), REL_MAX) + (CHUNK - 1)
    rrow = rel_bias.astype(F32)[:, idx].reshape(heads, n_back, 1, 2 * bq)
    return pl.pallas_call(
        _band_bias_kernel,
        grid=(heads,),
        in_specs=[pl.BlockSpec((None, n_back, 1, 2 * bq), lambda hh: (hh, 0, 0, 0))],
        out_specs=pl.BlockSpec((None, n_back, bq, bq), lambda hh: (hh, 0, 0, 0)),
        out_shape=jax.ShapeDtypeStruct((heads, n_back, bq, bq), F32),
        compiler_params=_params("arbitrary"),
        name="band_bias",
    )(rrow)


def mixer_cd(z, bias_tab, *, bsz, seq, heads):
    t = z.shape[0]
    d = HEAD_DIM

    def col(which):
        return pl.BlockSpec((seq, d), lambda b, g: (b, (3 * (g // heads) + which) * heads + g % heads))

    return pl.pallas_call(
        functools.partial(_mixer_cd_kernel, heads=heads, scale=d ** -0.5),
        grid=(bsz, 2 * heads),
        in_specs=[
            col(0), col(1), col(2),
            pl.BlockSpec((None,) + bias_tab.shape[1:], lambda b, g: (jnp.minimum(g, heads - 1), 0, 0, 0)),
        ],
        out_specs=pl.BlockSpec((seq, d), lambda b, g: (b, g)),
        out_shape=jax.ShapeDtypeStruct((t, 2 * heads * d), BF16),
        scratch_shapes=[pltpu.VMEM((SB_TQ, d), F32), pltpu.VMEM((SB_TQ, LANES), F32)],
        compiler_params=_params("arbitrary", "arbitrary"),
        name="mixer_cd",
    )(z, z, z, bias_tab)


def rope_tables(seq_len):
    pos = jnp.arange(seq_len, dtype=F32)
    inv_freq = ROPE_BASE ** (-jnp.arange(0, HEAD_DIM, 2, dtype=F32) / HEAD_DIM)
    ang = pos[:, None] * inv_freq[None, :]
    cos, sin = jnp.cos(ang), jnp.sin(ang)
    return jnp.concatenate([cos, cos], axis=1), jnp.concatenate([-sin, sin], axis=1)


def retention_log_decay(heads):
    lg = jnp.log1p(-jnp.exp2(-(5.0 + jnp.arange(heads, dtype=F32))))
    return jnp.broadcast_to(lg[:, None, None], (heads, 1, MIX_CHUNK))


def _pad_lanes(a, value=0.0):
    return jnp.pad(a, ((0, 0), (0, LANES - a.shape[1])), constant_values=value)


def kernel(x, mem, ab_w_in, ab_gate_b, ab_conv_w, ab_conv_b, ab_wq, ab_wk, ab_ret_norm_g, ab_mlstm_norm_g, ab_w_out, cd_w_in, cd_rel_bias, cd_w_out, mix_ln_g, mix_ln_b, xa_wq, xa_wkv, xa_wo, xa_ln_g, xa_ln_b, moe_router_w, moe_router_b, moe_w_gate, moe_b_gate, moe_w_up, moe_b_up, moe_w_down, moe_b_down, moe_ln_g, moe_ln_b):
    bsz, seq, d = x.shape
    depth = mix_ln_g.shape[0]
    heads = GROUP_HEADS
    alpha = (2.0 * depth) ** 0.25
    t = bsz * seq
    cos2, sin2 = rope_tables(seq)
    log_g = retention_log_decay(heads)
    h = x.reshape(t, d)
    hb = h.astype(BF16)
    memb = mem.reshape(-1, d).astype(BF16)
    n_ab = 7 * GROUP_WIDTH
    n_cd = 6 * GROUP_WIDTH
    ab_w_in_t = jnp.swapaxes(ab_w_in, 1, 2)
    ab_w_out_b = ab_w_out.astype(BF16)
    cd_w_out_b = cd_w_out.astype(BF16)
    for layer in range(depth):
        i = layer // 2
        if layer % 2 == 0:
            z = matmul_nt_stacked(hb, ab_w_in_t, i, n_ab, tm=1024, tn=512, out_dtype=BF16)
            gates = gates_nt(hb, ab_w_in_t, i, n_ab, 2 * heads, _pad_lanes(ab_gate_b[i][None, :]), tm=1024)
            gain = jnp.concatenate([ab_ret_norm_g[i], ab_mlstm_norm_g[i]])
            yb = mixer_ab(z, gates, cos2, sin2, log_g, ab_conv_w, ab_conv_b, ab_wq, ab_wk, gain, i,
                          bsz=bsz, seq=seq, heads=heads)
            w_out = ab_w_out_b
        else:
            z = matmul_stacked(hb, cd_w_in, i, n_cd, tm=1024, tn=512, out_dtype=BF16)
            yb = mixer_cd(z, band_bias_table(cd_rel_bias[i]), bsz=bsz, seq=seq, heads=heads)
            w_out = cd_w_out_b
        h, hb = matmul_ln(yb, w_out, i, h, mix_ln_g[layer][None, :], mix_ln_b[layer][None, :], alpha)

        kvb = matmul_stacked(memb, xa_wkv, layer, xa_wkv.shape[2], tm=memb.shape[0], tn=512, out_dtype=BF16)
        h, hp, topw, topi, sel = xattn_ln(
            h, xa_wq[layer].astype(BF16), kvb, xa_wo[layer].astype(BF16),
            xa_ln_g[layer][None, :], xa_ln_b[layer][None, :],
            _pad_lanes(moe_router_w[layer]), _pad_lanes(moe_router_b[layer][None, :]), alpha, seq=seq)
        plan = moe_route_plan(topi, sel, tm=MOE_TM)
        y = moe_experts(hp, plan, moe_w_gate, moe_b_gate, moe_w_up, moe_b_up, moe_w_down, moe_b_down, layer, tm=MOE_TM)
        h, hb = moe_combine_ln(y, topw, h, moe_ln_g[layer][None, :], moe_ln_b[layer][None, :], alpha)
    return h.reshape(bsz, seq, d)
```

```python
import functools

import jax
import jax.numpy as jnp
from jax import lax
from jax.experimental import pallas as pl
from jax.experimental.pallas import tpu as pltpu

F32 = jnp.float32
BF16 = jnp.bfloat16

CHUNK = 64
HEAD_DIM = 128
GROUP_HEADS = 16
GROUP_WIDTH = GROUP_HEADS * HEAD_DIM
CONV_K = 4
PAST_CHUNKS = 8
REL_MAX = 2 * CHUNK
ROPE_BASE = 10000.0
XA_HEADS = 4
N_EXPERTS = 32
TOP_K = 4
SWIGLU_LIMIT = 7.0
SWIGLU_ALPHA = 1.702
LN_EPS = 1e-5
HN_EPS = 1e-6

LANES = 128
VMEM_LIMIT_BYTES = 58 * 1024 * 1024


def _params(*sem):
    return pltpu.CompilerParams(dimension_semantics=sem, vmem_limit_bytes=VMEM_LIMIT_BYTES)


def _mm_kernel(x_ref, w_ref, o_ref, wb_ref):
    @pl.when(pl.program_id(1) == 0)
    def _():
        wb_ref[...] = w_ref[...].astype(BF16)

    o_ref[...] = jnp.dot(x_ref[...], wb_ref[...], preferred_element_type=F32).astype(o_ref.dtype)


def matmul_stacked(x, w, layer, n_cols, *, tm, tn, out_dtype):
    m, k = x.shape
    return pl.pallas_call(
        _mm_kernel,
        grid=(n_cols // tn, m // tm),
        in_specs=[
            pl.BlockSpec((tm, k), lambda j, i: (i, 0)),
            pl.BlockSpec((None, k, tn), lambda j, i: (layer, 0, j)),
        ],
        out_specs=pl.BlockSpec((tm, tn), lambda j, i: (i, j)),
        out_shape=jax.ShapeDtypeStruct((m, n_cols), out_dtype),
        scratch_shapes=[pltpu.VMEM((k, tn), BF16)],
        compiler_params=_params("arbitrary", "arbitrary"),
        name="mm_in",
    )(x, w)


_NT = (((1,), (1,)), ((), ()))
_TN = (((0,), (0,)), ((), ()))


def _mm_nt_kernel(x_ref, w_ref, o_ref, wb_ref):
    @pl.when(pl.program_id(1) == 0)
    def _():
        wb_ref[...] = w_ref[...].astype(BF16)

    o_ref[...] = lax.dot_general(x_ref[...], wb_ref[...], _NT, preferred_element_type=F32).astype(o_ref.dtype)


def matmul_nt_stacked(x, wt, layer, n_cols, *, tm, tn, out_dtype):
    m, k = x.shape
    return pl.pallas_call(
        _mm_nt_kernel,
        grid=(n_cols // tn, m // tm),
        in_specs=[
            pl.BlockSpec((tm, k), lambda j, i: (i, 0)),
            pl.BlockSpec((None, tn, k), lambda j, i: (layer, j, 0)),
        ],
        out_specs=pl.BlockSpec((tm, tn), lambda j, i: (i, j)),
        out_shape=jax.ShapeDtypeStruct((m, n_cols), out_dtype),
        scratch_shapes=[pltpu.VMEM((tn, k), BF16)],
        compiler_params=_params("arbitrary", "arbitrary"),
        name="mm_in_nt",
    )(x, wt)


def _gates_kernel(x_ref, w_ref, b_ref, o_ref, wb_ref):
    @pl.when(pl.program_id(0) == 0)
    def _():
        wb_ref[...] = jnp.zeros_like(wb_ref)
        wb_ref[:w_ref.shape[0], :] = w_ref[...].astype(BF16)

    o_ref[...] = lax.dot_general(x_ref[...], wb_ref[...], _NT, preferred_element_type=F32) + b_ref[...]


def gates_nt(x, wt, layer, row0, n_rows, b_pad, *, tm):
    m, k = x.shape
    return pl.pallas_call(
        _gates_kernel,
        grid=(m // tm,),
        in_specs=[
            pl.BlockSpec((tm, k), lambda i: (i, 0)),
            pl.BlockSpec((None, n_rows, k), lambda i: (layer, row0 // n_rows, 0)),
            pl.BlockSpec((1, LANES), lambda i: (0, 0)),
        ],
        out_specs=pl.BlockSpec((tm, LANES), lambda i: (i, 0)),
        out_shape=jax.ShapeDtypeStruct((m, LANES), F32),
        scratch_shapes=[pltpu.VMEM((LANES, k), BF16)],
        compiler_params=_params("arbitrary"),
        name="gates",
    )(x, wt, b_pad)


def _pack_bf16_pair(y):
    half = y.shape[1] // 2
    hi = pltpu.bitcast(y[:, :half].astype(BF16).astype(F32), jnp.uint32)
    lo = pltpu.bitcast(y[:, half:].astype(BF16).astype(F32), jnp.uint32)
    return hi | (lo >> 16)


def _unpack_bf16_pair(w):
    return (pltpu.bitcast(w & jnp.uint32(0xFFFF0000), F32), pltpu.bitcast(w << 16, F32))


def _ln_rows(z_ref, g_ref, b_ref, of_ref, ob_ref, rows, packed=False):
    tm = z_ref.shape[0]

    def body(r, carry):
        sl = pl.ds(pl.multiple_of(r * rows, rows), rows)
        z = z_ref[sl, :]
        mu = jnp.mean(z, axis=-1, keepdims=True)
        zc = z - mu
        var = jnp.mean(zc * zc, axis=-1, keepdims=True)
        y = zc * lax.rsqrt(var + LN_EPS) * g_ref[...] + b_ref[...]
        of_ref[sl, :] = y
        ob_ref[sl, :] = _pack_bf16_pair(y) if packed else y.astype(BF16)
        return carry

    lax.fori_loop(0, tm // rows, body, 0, unroll=2)


def _mm_ln_kernel(x_ref, w_ref, h_ref, g_ref, b_ref, of_ref, ob_ref, *, nk, nj, tn, alpha):
    k = pl.program_id(1)
    j = pl.program_id(2)
    part = jnp.dot(x_ref[...], w_ref[...].astype(BF16), preferred_element_type=F32)
    for jj in range(nj):
        sl = slice(jj * tn, (jj + 1) * tn)

        @pl.when((j == jj) & (k == 0))
        def _():
            of_ref[:, sl] = alpha * h_ref[...] + part

        @pl.when((j == jj) & (k > 0))
        def _():
            of_ref[:, sl] += part

    @pl.when((k == nk - 1) & (j == nj - 1))
    def _():
        _ln_rows(of_ref, g_ref, b_ref, of_ref, ob_ref, 64)


def matmul_ln(x, w, layer, h, g, b, alpha, *, tm=512, tn=1024, tk=4096):
    m, kdim = x.shape
    n = h.shape[1]
    tk = min(tk, kdim)
    nk, nj = kdim // tk, n // tn
    in_specs = [
        pl.BlockSpec((tm, tk), lambda i, k, j: (i, k)),
        pl.BlockSpec((None, tk, tn), lambda i, k, j: (layer, k, j)),
        pl.BlockSpec((tm, tn), lambda i, k, j: (i, j)),
        pl.BlockSpec((1, n), lambda i, k, j: (0, 0)),
        pl.BlockSpec((1, n), lambda i, k, j: (0, 0)),
    ]
    args = [x, w, h, g, b]
    return pl.pallas_call(
        functools.partial(_mm_ln_kernel, nk=nk, nj=nj, tn=tn, alpha=alpha),
        grid=(m // tm, nk, nj),
        in_specs=in_specs,
        out_specs=[
            pl.BlockSpec((tm, n), lambda i, k, j: (i, 0)),
            pl.BlockSpec((tm, n), lambda i, k, j: (i, 0)),
        ],
        out_shape=[jax.ShapeDtypeStruct((m, n), F32), jax.ShapeDtypeStruct((m, n), BF16)],
        compiler_params=_params("arbitrary", "arbitrary", "arbitrary"),
        name="mm_ln",
    )(*args)


def _xattn_kernel(h_ref, wq_ref, kv_ref, wo_ref, g_ref, b_ref, wrh_ref, wrl_ref, br_ref,
                  of_ref, ob_ref, topw_ref, topi_ref, sel_ref, *, alpha, heads, hd):
    q = jnp.dot(h_ref[...].astype(BF16), wq_ref[...], preferred_element_type=F32)
    scale = hd ** -0.5
    outs = []
    for hh in range(heads):
        qh = (q[:, hh * hd:(hh + 1) * hd] * scale).astype(BF16)
        kh = kv_ref[:, hh * hd:(hh + 1) * hd]
        vh = kv_ref[:, (heads + hh) * hd:(heads + hh + 1) * hd]
        s = lax.dot_general(qh, kh, (((1,), (1,)), ((), ())), preferred_element_type=F32)
        s = s - jnp.max(s, axis=-1, keepdims=True)
        p = jnp.exp(s)
        l = jnp.sum(p, axis=-1, keepdims=True)
        o = jnp.dot(p.astype(BF16), vh, preferred_element_type=F32) / l
        outs.append(o.astype(BF16))
    o_all = jnp.concatenate(outs, axis=-1)
    of_ref[...] = alpha * h_ref[...] + jnp.dot(o_all, wo_ref[...], preferred_element_type=F32)
    _ln_rows(of_ref, g_ref, b_ref, of_ref, ob_ref, 64, packed=True)
    topw_ref[...], topi_ref[...], sel_ref[...] = _route_top4(of_ref[...], wrh_ref[...], wrl_ref[...], br_ref[...])


def xattn_ln(h, wq_b, kv_b, wo_b, g, b, w_router, b_router, alpha, *, seq, tm=512):
    m, d = h.shape
    lane_spec = pl.BlockSpec((tm, LANES), lambda i: (i, 0))
    xw = wq_b.shape[1]
    n_mem = kv_b.shape[0] // (m // seq)
    per_b = seq // tm

    def const_spec(shape):
        return pl.BlockSpec(shape, lambda i: (0, 0), pipeline_mode=pl.Buffered(1))

    return pl.pallas_call(
        functools.partial(_xattn_kernel, alpha=alpha, heads=XA_HEADS, hd=xw // XA_HEADS),
        grid=(m // tm,),
        in_specs=[
            pl.BlockSpec((tm, d), lambda i: (i, 0)),
            const_spec((d, xw)),
            pl.BlockSpec((n_mem, 2 * xw), lambda i: (i // per_b, 0)),
            const_spec((xw, d)),
            const_spec((1, d)),
            const_spec((1, d)),
            const_spec((d, LANES)),
            const_spec((d, LANES)),
            const_spec((1, LANES)),
        ],
        out_specs=[pl.BlockSpec((tm, d), lambda i: (i, 0)), pl.BlockSpec((tm, d // 2), lambda i: (i, 0)),
                   lane_spec, lane_spec, lane_spec],
        out_shape=[jax.ShapeDtypeStruct((m, d), F32), jax.ShapeDtypeStruct((m, d // 2), jnp.uint32),
                   jax.ShapeDtypeStruct((m, LANES), F32), jax.ShapeDtypeStruct((m, LANES), jnp.int32),
                   jax.ShapeDtypeStruct((m, LANES), F32)],
        compiler_params=_params("arbitrary"),
        name="xattn_ln",
    )(h, wq_b, kv_b, wo_b, g, b, *_split_bf16(w_router), b_router)


def _route_top4(h, w_hi, w_lo, b):
    h_hi, h_lo = _split_bf16(h)
    both = jnp.dot(h_hi, jnp.concatenate([w_hi, w_lo], axis=1), preferred_element_type=F32)
    logits = both[:, :LANES] + both[:, LANES:] + jnp.dot(h_lo, w_hi, preferred_element_type=F32) + b
    lane = lax.broadcasted_iota(jnp.int32, logits.shape, 1)
    neg = jnp.float32(-jnp.inf)
    masked = jnp.where(lane < N_EXPERTS, logits, neg)
    top_vals, top_idx = [], []
    sel = jnp.zeros_like(logits)
    for _ in range(TOP_K):
        mval = jnp.max(masked, axis=-1, keepdims=True)
        idx = jnp.min(jnp.where(masked == mval, lane, LANES), axis=-1, keepdims=True)
        hot = lane == idx
        top_vals.append(mval)
        top_idx.append(idx)
        sel = jnp.where(hot, 1.0, sel)
        masked = jnp.where(hot, neg, masked)
    exps = [jnp.exp(v - top_vals[0]) for v in top_vals]
    denom = exps[0]
    for e in exps[1:]:
        denom = denom + e
    topw = jnp.zeros_like(logits)
    topi = jnp.zeros(logits.shape, jnp.int32)
    for k in range(TOP_K):
        topw = jnp.where(lane == k, exps[k] / denom, topw)
        topi = jnp.where(lane == k, top_idx[k], topi)
    return topw, topi, sel


MOE_TM = 256


def moe_route_plan(topi, sel, *, tm):
    t = topi.shape[0]
    n_tiles = t * TOP_K // tm + N_EXPERTS
    order = jnp.argsort(topi[:, :TOP_K].reshape(-1), stable=True).astype(jnp.int32)
    counts = jnp.sum(sel[:, :N_EXPERTS], axis=0).astype(jnp.int32)
    first_pair = jnp.cumsum(counts) - counts
    tiles_e = (counts + tm - 1) // tm
    tile_end = jnp.cumsum(tiles_e)
    tile_start = tile_end - tiles_e
    tile_ids = jnp.arange(n_tiles, dtype=jnp.int32)
    tile_expert = jnp.minimum(jnp.sum(tile_end[None, :] <= tile_ids[:, None], axis=1), N_EXPERTS - 1).astype(jnp.int32)
    local = (tile_ids - tile_start[tile_expert])[:, None] * tm + jnp.arange(tm, dtype=jnp.int32)[None, :]
    valid = (tile_ids < tile_end[-1])[:, None] & (local < counts[tile_expert][:, None])
    pair = order[jnp.clip(first_pair[tile_expert][:, None] + local, 0, TOP_K * t - 1)]
    pad_row = TOP_K * t + (tile_ids % 2)[:, None] * tm + jnp.arange(tm, dtype=jnp.int32)[None, :]
    ydst = jnp.where(valid, (pair % TOP_K) * t + pair // TOP_K, pad_row)
    tok = ydst % t
    n_valid = tile_end[-1:].astype(jnp.int32)
    return (tok.reshape(n_tiles, 1, tm), ydst.reshape(n_tiles, 1, tm), tile_expert, n_valid)


def _moe_ffn_kernel(te_ref, nv_ref, tok0_ref, tok1_ref, tok2_ref, ydst_ref, h_hbm, wg_ref, wu_ref, wd_ref,
                    bg_ref, bu_ref, bd_ref, y_hbm, xg, og, wgu_b, wd_b, gsem, ssem, *, tm, ff):
    r = pl.program_id(0)
    nv = nv_ref[0]
    slot = lax.rem(r, 2)
    gslot = lax.rem(r, 3)

    def row_gather(idx_ref, s):
        for i in range(tm):
            pltpu.make_async_copy(h_hbm.at[pl.ds(idx_ref[0, i], 1), :], xg.at[s, pl.ds(i, 1), :], gsem.at[s]).start()

    def gather_wait(s):
        pltpu.make_async_copy(h_hbm.at[pl.ds(0, tm), :], xg.at[s], gsem.at[s]).wait()

    def scatter_wait(s):
        pltpu.make_async_copy(og.at[s], y_hbm.at[pl.ds(0, tm), :], ssem.at[s]).wait()

    @pl.when(r == 0)
    def _():
        row_gather(tok0_ref, 0)
        row_gather(tok1_ref, 1)
        og[1] = jnp.zeros(og.shape[1:], og.dtype)
        base = y_hbm.shape[0] - 2 * tm
        for part in range(2):
            fill = pltpu.make_async_copy(og.at[1], y_hbm.at[pl.ds(base + part * tm, tm), :], ssem.at[1])
            fill.start()
            fill.wait()

    @pl.when(r < nv)
    def _():
        gather_wait(gslot)

        @pl.when(r >= 2)
        def _():
            scatter_wait(slot)

        @pl.when((r == 0) | (te_ref[r] != te_ref[jnp.maximum(r - 1, 0)]))
        def _():
            wgu_b[:, :ff] = wg_ref[...].astype(BF16)
            wgu_b[:, ff:] = wu_ref[...].astype(BF16)
            wd_b[...] = wd_ref[...].astype(BF16)

        x_hi, x_lo = _unpack_bf16_pair(xg[gslot])
        half = x_hi.shape[1]
        gu = (jnp.dot(x_hi.astype(BF16), wgu_b[:half, :], preferred_element_type=F32)
              + jnp.dot(x_lo.astype(BF16), wgu_b[half:, :], preferred_element_type=F32))
        g = jnp.minimum(gu[:, :ff] + bg_ref[...], SWIGLU_LIMIT)
        u = jnp.clip(gu[:, ff:] + bu_ref[...], -SWIGLU_LIMIT, SWIGLU_LIMIT)
        act = (g * jax.nn.sigmoid(SWIGLU_ALPHA * g) * (u + 1.0)).astype(BF16)
        res = _pack_bf16_pair(jnp.dot(act, wd_b[...], preferred_element_type=F32) + bd_ref[...])
        row_gather(tok2_ref, lax.rem(r + 2, 3))
        og[slot] = res
        for i in range(tm):
            pltpu.make_async_copy(og.at[slot, pl.ds(i, 1), :], y_hbm.at[pl.ds(ydst_ref[0, i], 1), :],
                                  ssem.at[slot]).start()

        @pl.when(r == nv - 1)
        def _():
            gather_wait(lax.rem(r + 1, 3))
            gather_wait(lax.rem(r + 2, 3))
            scatter_wait(slot)

            @pl.when(r >= 1)
            def _():
                scatter_wait(1 - slot)


def moe_experts(hp, plan, w_gate, b_gate, w_up, b_up, w_down, b_down, layer, *, tm):
    tok, ydst, tile_expert, n_valid = plan
    t = hp.shape[0]
    d = 2 * hp.shape[1]
    n_tiles = tok.shape[0]
    n_e, ff = w_gate.shape[1], w_gate.shape[3]
    bg = b_gate.reshape(b_gate.shape[0], n_e, 1, ff)
    bu = b_up.reshape(b_up.shape[0], n_e, 1, ff)
    bd = b_down.reshape(b_down.shape[0], n_e, 1, d)

    def expert(r, te, nv):
        return te[jnp.minimum(r, nv[0] - 1)]

    smem_blk = functools.partial(pl.BlockSpec, (None, 1, tm), memory_space=pltpu.SMEM)
    grid_spec = pltpu.PrefetchScalarGridSpec(
        num_scalar_prefetch=2,
        grid=(n_tiles,),
        in_specs=[
            smem_blk(lambda r, te, nv: (jnp.minimum(r, nv[0] - 1), 0, 0)),
            smem_blk(lambda r, te, nv: (jnp.minimum(r + 1, nv[0] - 1), 0, 0)),
            smem_blk(lambda r, te, nv: (jnp.minimum(r + 2, nv[0] - 1), 0, 0)),
            smem_blk(lambda r, te, nv: (r, 0, 0)),
            pl.BlockSpec(memory_space=pl.ANY),
            pl.BlockSpec((None, None, d, ff), lambda r, te, nv: (layer, expert(r, te, nv), 0, 0)),
            pl.BlockSpec((None, None, d, ff), lambda r, te, nv: (layer, expert(r, te, nv), 0, 0)),
            pl.BlockSpec((None, None, ff, d), lambda r, te, nv: (layer, expert(r, te, nv), 0, 0)),
            pl.BlockSpec((None, None, 1, ff), lambda r, te, nv: (layer, expert(r, te, nv), 0, 0)),
            pl.BlockSpec((None, None, 1, ff), lambda r, te, nv: (layer, expert(r, te, nv), 0, 0)),
            pl.BlockSpec((None, None, 1, d), lambda r, te, nv: (layer, expert(r, te, nv), 0, 0)),
        ],
        out_specs=pl.BlockSpec(memory_space=pl.ANY),
        scratch_shapes=[
            pltpu.VMEM((3, tm, d // 2), jnp.uint32),
            pltpu.VMEM((2, tm, d // 2), jnp.uint32),
            pltpu.VMEM((d, 2 * ff), BF16),
            pltpu.VMEM((ff, d), BF16),
            pltpu.SemaphoreType.DMA((3,)),
            pltpu.SemaphoreType.DMA((2,)),
        ],
    )
    return pl.pallas_call(
        functools.partial(_moe_ffn_kernel, tm=tm, ff=ff),
        grid_spec=grid_spec,
        out_shape=jax.ShapeDtypeStruct((TOP_K * t + 2 * tm, d // 2), jnp.uint32),
        compiler_params=_params("arbitrary"),
        name="moe_experts",
    )(tile_expert, n_valid, tok, tok, tok, ydst, hp, w_gate, w_up, w_down, bg, bu, bd)


def _moe_combine_kernel(y0_ref, y1_ref, y2_ref, y3_ref, w_ref, h_ref, g_ref, b_ref, of_ref, ob_ref, *, alpha, rows):
    tm = h_ref.shape[0]
    y_refs = (y0_ref, y1_ref, y2_ref, y3_ref)

    def body(rr, carry):
        sl = pl.ds(pl.multiple_of(rr * rows, rows), rows)
        w = w_ref[sl, :]
        hrow = h_ref[sl, :]
        half = hrow.shape[1] // 2
        z_hi = alpha * hrow[:, :half]
        z_lo = alpha * hrow[:, half:]
        for k, y_ref in enumerate(y_refs):
            y_hi, y_lo = _unpack_bf16_pair(y_ref[sl, :])
            z_hi = z_hi + w[:, k:k + 1] * y_hi
            z_lo = z_lo + w[:, k:k + 1] * y_lo
        z = jnp.concatenate([z_hi, z_lo], axis=1)
        mu = jnp.mean(z, axis=-1, keepdims=True)
        zc = z - mu
        var = jnp.mean(zc * zc, axis=-1, keepdims=True)
        y = zc * lax.rsqrt(var + LN_EPS) * g_ref[...] + b_ref[...]
        of_ref[sl, :] = y
        ob_ref[sl, :] = y.astype(BF16)
        return carry

    lax.fori_loop(0, tm // rows, body, 0)


def moe_combine_ln(y, topw, h, g, b, alpha, *, tm=128):
    t, d = h.shape
    nb = t // tm

    def y_spec(k):
        return pl.BlockSpec((tm, d // 2), lambda i: (k * nb + i, 0))

    row_spec = pl.BlockSpec((tm, d), lambda i: (i, 0))
    return pl.pallas_call(
        functools.partial(_moe_combine_kernel, alpha=alpha, rows=32),
        grid=(nb,),
        in_specs=[y_spec(0), y_spec(1), y_spec(2), y_spec(3),
                  pl.BlockSpec((tm, LANES), lambda i: (i, 0)), row_spec,
                  pl.BlockSpec((1, d), lambda i: (0, 0)), pl.BlockSpec((1, d), lambda i: (0, 0))],
        out_specs=[row_spec, row_spec],
        out_shape=[jax.ShapeDtypeStruct((t, d), F32), jax.ShapeDtypeStruct((t, d), BF16)],
        compiler_params=_params("arbitrary"),
        name="moe_combine_ln",
    )(y, y, y, y, topw, h, g, b)


NEG = -1e30
LOG2E = 1.4426950408889634
MIX_CHUNK = 256
BAND_BLOCK = 256
SB_TQ = 512
SB_TK = 256

def _head_norm_rows(x, gain):
    mu = jnp.mean(x, axis=-1, keepdims=True)
    xc = x - mu
    var = jnp.mean(xc * xc, axis=-1, keepdims=True)
    return xc * lax.rsqrt(var + HN_EPS) * gain


def _log_sigmoid(x):
    return jnp.minimum(x, 0.0) - jnp.log(1.0 + jnp.exp(-jnp.abs(x)))


def _split_bf16(x):
    hi = x.astype(BF16)
    lo = (x - hi.astype(F32)).astype(BF16)
    return hi, lo


def _lane_select(x, lane_idx):
    lane = lax.broadcasted_iota(jnp.int32, x.shape, 1)
    col = jnp.sum(jnp.where(lane == lane_idx, x, 0.0), axis=-1, keepdims=True)
    return jnp.broadcast_to(col, x.shape)


def _retention_body(q_ref, k_ref, v_ref, g_ref, cos_ref, sin_ref, lg_ref, gain_ref, o_ref,
                    qs_ref, ks_ref, st_ref, *, scale):
    seq = q_ref.shape[0]
    L = MIX_CHUNK
    half = HEAD_DIM // 2
    cos = cos_ref[...]
    sin = sin_ref[...]
    q = q_ref[...].astype(F32)
    k = k_ref[...].astype(F32)
    qs_ref[...] = (q * cos + pltpu.roll(q, half, 1) * sin).astype(BF16)
    ks_ref[...] = ((k * cos + pltpu.roll(k, half, 1) * sin) * scale).astype(BF16)

    lg = lg_ref[...]
    ri = lax.broadcasted_iota(jnp.int32, (L, L), 0)
    ci = lax.broadcasted_iota(jnp.int32, (L, L), 1)
    intra = jnp.where(ri >= ci, jnp.exp(lg * jnp.maximum(ri - ci, 0).astype(F32)), 0.0)
    rr = lax.broadcasted_iota(jnp.int32, (L, HEAD_DIM), 0).astype(F32)
    lg_d = lg[:, :HEAD_DIM]
    q_dec = jnp.exp(lg_d * (rr + 1.0))
    k_dec = jnp.exp(lg_d * (L - 1.0 - rr))
    c_dec = jnp.exp(lg_d * float(L))
    gain = gain_ref[...]
    st_ref[...] = jnp.zeros_like(st_ref)

    def chunk(c, carry):
        sl = pl.ds(pl.multiple_of(c * L, L), L)
        qc = qs_ref[sl, :]
        kc = ks_ref[sl, :]
        vc = v_ref[sl, :]
        state = st_ref[:, :HEAD_DIM]
        att = lax.dot_general(qc, kc, _NT, preferred_element_type=F32) * intra
        o = (jnp.dot(att.astype(BF16), vc, preferred_element_type=F32)
             + jnp.dot((qc.astype(F32) * q_dec).astype(BF16), state.astype(BF16), preferred_element_type=F32))
        st_ref[:, :HEAD_DIM] = state * c_dec + lax.dot_general(
            (kc.astype(F32) * k_dec).astype(BF16), vc, _TN, preferred_element_type=F32)
        gv = g_ref[sl, :].astype(F32)
        o_ref[sl, :] = (_head_norm_rows(o, gain) * (gv * jax.nn.sigmoid(gv))).astype(o_ref.dtype)
        return carry

    lax.fori_loop(0, seq // L, chunk, 0, unroll=True)


def _mlstm_body(u_ref, v_ref, og_ref, gates_ref, cw_ref, cb_ref, wq_ref, wk_ref, gain_ref, o_ref,
                qs_ref, ks_ref, st_ref, m_ref, *, head, heads, scale):
    seq = u_ref.shape[0]
    L = MIX_CHUNK
    d = HEAD_DIM
    x = u_ref[...].astype(F32)
    row = lax.broadcasted_iota(jnp.int32, x.shape, 0)
    cw = cw_ref[...]
    y = x * cw[CONV_K - 1:CONV_K, :] + cb_ref[...]
    for sh in range(1, CONV_K):
        xs = jnp.where(row >= sh, pltpu.roll(x, sh, 0), 0.0)
        y = y + xs * cw[CONV_K - 1 - sh:CONV_K - sh, :]
    ub = (y * jax.nn.sigmoid(y)).astype(BF16)
    qs_ref[...] = jnp.dot(ub, wq_ref[...].astype(BF16), preferred_element_type=F32).astype(BF16)
    ks_ref[...] = (jnp.dot(ub, wk_ref[...].astype(BF16), preferred_element_type=F32) * scale).astype(BF16)

    ri = lax.broadcasted_iota(jnp.int32, (L, L), 0)
    ci = lax.broadcasted_iota(jnp.int32, (L, L), 1)
    causal = ri >= ci
    tri = jnp.where(causal, 1.0, 0.0).astype(BF16)
    ones_v = jnp.ones((L, d), BF16)
    gain = gain_ref[...]
    st_ref[...] = jnp.zeros_like(st_ref)
    m_ref[...] = jnp.zeros_like(m_ref)

    def chunk(c, carry):
        sl = pl.ds(pl.multiple_of(c * L, L), L)
        qc = qs_ref[sl, :]
        kc = ks_ref[sl, :]
        v_ext = jnp.concatenate([v_ref[sl, :], ones_v], axis=1)
        gts = gates_ref[sl, :]
        ic = _lane_select(gts, head)
        lf = _log_sigmoid(_lane_select(gts, heads + head))
        lf_hi, lf_lo = _split_bf16(lf)
        bcum = (jnp.dot(tri, lf_hi, preferred_element_type=F32)
                + jnp.dot(tri, lf_lo, preferred_element_type=F32))
        m_st = m_ref[...]
        src = jnp.transpose(ic - bcum)[:1, :]
        bcum2 = jnp.concatenate([bcum, bcum], axis=1)
        log_intra = jnp.where(causal, bcum2 + src, NEG)
        m_intra = jnp.max(log_intra, axis=-1, keepdims=True)
        log_cross = bcum + m_st
        m_row = jnp.maximum(log_cross, m_intra)
        m_row2 = jnp.concatenate([m_row, m_row], axis=1)
        w_intra = jnp.exp(log_intra - m_row2)
        w_cross = jnp.exp(log_cross - m_row)
        w_cross2 = jnp.concatenate([w_cross, w_cross], axis=1)
        qk = lax.dot_general(qc, kc, _NT, preferred_element_type=F32) * w_intra
        state = st_ref[...]
        res = (jnp.dot(qk.astype(BF16), v_ext, preferred_element_type=F32)
               + w_cross2 * jnp.dot(qc, state.astype(BF16), preferred_element_type=F32))
        num = res[:, :d]
        den = res[:, d:]
        hh = num / jnp.maximum(jnp.abs(den), jnp.exp(-m_row))
        og = og_ref[sl, :].astype(F32)
        o_ref[sl, :] = _head_norm_rows(hh * jax.nn.sigmoid(og), gain).astype(o_ref.dtype)
        b_last = bcum[L - 1:L, :]
        log_state = b_last - bcum + ic
        m_new = jnp.maximum(b_last + m_st, jnp.max(log_state, axis=0, keepdims=True))
        decay = jnp.exp(b_last + m_st - m_new)
        kw = (kc.astype(F32) * jnp.exp(log_state - m_new)).astype(BF16)
        decay2 = jnp.concatenate([decay, decay], axis=1)
        st_ref[...] = decay2 * state + lax.dot_general(kw, v_ext, _TN, preferred_element_type=F32)
        m_ref[...] = m_new
        return carry

    lax.fori_loop(0, seq // L, chunk, 0, unroll=True)


def _mixer_ab_kernel(a0_ref, a1_ref, a2_ref, a3_ref, gates_ref, cos_ref, sin_ref, lg_ref, cw_ref, cb_ref,
                     wq_ref, wk_ref, gain_ref, o_ref, qs_ref, ks_ref, st_ref, m_ref, *, heads, scale):
    g = pl.program_id(1)

    @pl.when(g < heads)
    def _():
        _retention_body(a0_ref, a1_ref, a2_ref, a3_ref, cos_ref, sin_ref, lg_ref, gain_ref, o_ref,
                        qs_ref, ks_ref, st_ref, scale=scale)

    @pl.when(g >= heads)
    def _():
        _mlstm_body(a0_ref, a1_ref, a2_ref, gates_ref, cw_ref, cb_ref, wq_ref, wk_ref, gain_ref, o_ref,
                    qs_ref, ks_ref, st_ref, m_ref, head=g - heads, heads=heads, scale=scale)


def mixer_ab(z, gates, cos2, sin2, log_g, conv_w, conv_b, wq_m, wk_m, gain, layer, *, bsz, seq, heads):
    t = z.shape[0]
    d = HEAD_DIM
    L = MIX_CHUNK

    def col(base_ret, base_ml):
        def index(b, g):
            is_ml = g // heads
            return (b, (1 - is_ml) * (base_ret * heads + g) + is_ml * (base_ml * heads + g - heads))
        return pl.BlockSpec((seq, d), index)

    def ml_head(g):
        return jnp.maximum(g - heads, 0)

    conv_w4 = conv_w.reshape(conv_w.shape[0], CONV_K, heads, d).transpose(0, 2, 1, 3)
    conv_b4 = conv_b.reshape(conv_b.shape[0], heads, 1, d)
    gain4 = gain.reshape(2 * heads, 1, d)
    return pl.pallas_call(
        functools.partial(_mixer_ab_kernel, heads=heads, scale=d ** -0.5),
        grid=(bsz, 2 * heads),
        in_specs=[
            col(0, 4), col(1, 5), col(2, 6), col(3, 6),
            pl.BlockSpec((seq, LANES), lambda b, g: (b, 0)),
            pl.BlockSpec((seq, d), lambda b, g: (0, 0)),
            pl.BlockSpec((seq, d), lambda b, g: (0, 0)),
            pl.BlockSpec((None, 1, L), lambda b, g: (jnp.minimum(g, heads - 1), 0, 0)),
            pl.BlockSpec((None, None, CONV_K, d), lambda b, g: (layer, ml_head(g), 0, 0)),
            pl.BlockSpec((None, None, 1, d), lambda b, g: (layer, ml_head(g), 0, 0)),
            pl.BlockSpec((None, None, d, d), lambda b, g: (layer, ml_head(g), 0, 0)),
            pl.BlockSpec((None, None, d, d), lambda b, g: (layer, ml_head(g), 0, 0)),
            pl.BlockSpec((None, 1, d), lambda b, g: (g, 0, 0)),
        ],
        out_specs=pl.BlockSpec((seq, d), lambda b, g: (b, g)),
        out_shape=jax.ShapeDtypeStruct((t, 2 * heads * d), BF16),
        scratch_shapes=[
            pltpu.VMEM((seq, d), BF16),
            pltpu.VMEM((seq, d), BF16),
            pltpu.VMEM((d, 2 * d), F32),
            pltpu.VMEM((1, d), F32),
        ],
        compiler_params=_params("arbitrary", "arbitrary"),
        name="mixer_ab",
    )(z, z, z, z, gates, cos2, sin2, log_g, conv_w4, conv_b4, wq_m, wk_m, gain4)


def _band_body(q_ref, k_ref, v_ref, bias_ref, o_ref, *, scale):
    seq = q_ref.shape[0]
    bq = BAND_BLOCK
    n_back = bias_ref.shape[0]

    def block(i, carry):
        sl = pl.ds(pl.multiple_of(i * bq, bq), bq)
        q = q_ref[sl, :]
        scores, vals = [], []
        for dlt in range(n_back):
            ks = pl.ds(pl.multiple_of(jnp.maximum(i - dlt, 0) * bq, bq), bq)
            s = lax.dot_general(q, k_ref[ks, :], _NT, preferred_element_type=F32) * (scale * LOG2E) + bias_ref[dlt]
            scores.append(jnp.where(i - dlt >= 0, s, NEG))
            vals.append(v_ref[ks, :])
        m = jnp.max(scores[0], axis=-1, keepdims=True)
        for s in scores[1:]:
            m = jnp.maximum(m, jnp.max(s, axis=-1, keepdims=True))
        acc = jnp.zeros((bq, HEAD_DIM), F32)
        l = jnp.zeros((bq, 1), F32)
        for s, vv in zip(scores, vals):
            p = jnp.exp2(s - m)
            l = l + jnp.sum(p, axis=-1, keepdims=True)
            acc = acc + jnp.dot(p.astype(BF16), vv, preferred_element_type=F32)
        o_ref[sl, :] = (acc / l).astype(o_ref.dtype)
        return carry

    lax.fori_loop(0, seq // bq, block, 0, unroll=True)


def _stick_breaking_body(q_ref, k_ref, v_ref, o_ref, acc_ref, r_ref, *, scale):
    seq = q_ref.shape[0]
    tq, tk = SB_TQ, SB_TK
    n_sub = tq // tk
    ri = lax.broadcasted_iota(jnp.int32, (tk, tk), 0)
    ci = lax.broadcasted_iota(jnp.int32, (tk, tk), 1)
    suffix = jnp.where(ri > ci, 1.0, 0.0).astype(BF16)
    strict = (lax.broadcasted_iota(jnp.int32, (tq, tq), 1)
              < lax.broadcasted_iota(jnp.int32, (tq, tq), 0))

    def one_group(q, k0, masked):
        kb = k_ref[pl.ds(k0, tq), :]
        vb = v_ref[pl.ds(k0, tq), :]
        z = lax.dot_general(q, kb, _NT, preferred_element_type=F32) * (scale * LOG2E)
        sp = jnp.maximum(z, 0.0) + jnp.log2(1.0 + jnp.exp2(-jnp.abs(z)))
        ls_pos = z - sp
        if masked:
            sp = jnp.where(strict, sp, 0.0)
        sp_b = sp.astype(BF16)
        r = r_ref[...]
        pieces = []
        for s in reversed(range(n_sub)):
            sl = slice(s * tk, (s + 1) * tk)
            between = jnp.dot(sp_b[:, sl], suffix, preferred_element_type=F32)
            pieces.append(ls_pos[:, sl] - between - jnp.concatenate([r] * (tk // LANES), axis=1))
            r = r + jnp.sum(sp[:, sl], axis=-1, keepdims=True)
        p = jnp.exp2(jnp.concatenate(pieces[::-1], axis=1))
        if masked:
            p = jnp.where(strict, p, 0.0)
        acc_ref[...] += jnp.dot(p.astype(BF16), vb, preferred_element_type=F32)
        r_ref[...] = r

    for i in range(seq // tq):
        q0 = i * tq
        q = q_ref[pl.ds(q0, tq), :]
        acc_ref[...] = jnp.zeros_like(acc_ref)
        r_ref[...] = jnp.zeros_like(r_ref)
        one_group(q, q0, True)

        def below(jj, c, q=q, i=i):
            one_group(q, pl.multiple_of((i - 1 - jj) * tq, tq), False)
            return c

        lax.fori_loop(0, i, below, 0, unroll=True)
        o_ref[pl.ds(q0, tq), :] = acc_ref[...].astype(o_ref.dtype)


def _mixer_cd_kernel(q_ref, k_ref, v_ref, bias_ref, o_ref, acc_ref, r_ref, *, heads, scale):
    g = pl.program_id(1)

    @pl.when(g < heads)
    def _():
        _band_body(q_ref, k_ref, v_ref, bias_ref, o_ref, scale=scale)

    @pl.when(g >= heads)
    def _():
        _stick_breaking_body(q_ref, k_ref, v_ref, o_ref, acc_ref, r_ref, scale=scale)


def _band_bias_kernel(rrow_ref, o_ref):
    bq = BAND_BLOCK
    shift = CHUNK.bit_length() - 1
    qo = lax.broadcasted_iota(jnp.int32, (bq, bq), 0)
    ck = jnp.right_shift(lax.broadcasted_iota(jnp.int32, (bq, bq), 1), shift)
    for dlt in range(o_ref.shape[0]):
        x = jnp.broadcast_to(rrow_ref[dlt], (bq, 2 * bq))
        toep = pltpu.roll(x, 0, 1, stride=1, stride_axis=0)[:, :bq]
        cq = jnp.right_shift(qo + bq * dlt, shift)
        allowed = (ck <= cq) & (ck >= cq - PAST_CHUNKS)
        o_ref[dlt] = jnp.where(allowed, toep * LOG2E, NEG)


def band_bias_table(rel_bias):
    bq = BAND_BLOCK
    heads = rel_bias.shape[0]
    n_back = PAST_CHUNKS * CHUNK // bq + 1
    m = jnp.arange(2 * bq)
    key_minus_query = jnp.where(m < bq, m, m - 2 * bq)
    dist = bq * jnp.arange(n_back)[:, None] - key_minus_query[None, :]
    idx = jnp.clip(dist, -(CHUNK - 1), REL_MAX) + (CHUNK - 1)
    rrow = rel_bias.astype(F32)[:, idx].reshape(heads, n_back, 1, 2 * bq)
    return pl.pallas_call(
        _band_bias_kernel,
        grid=(heads,),
        in_specs=[pl.BlockSpec((None, n_back, 1, 2 * bq), lambda hh: (hh, 0, 0, 0))],
        out_specs=pl.BlockSpec((None, n_back, bq, bq), lambda hh: (hh, 0, 0, 0)),
        out_shape=jax.ShapeDtypeStruct((heads, n_back, bq, bq), F32),
        compiler_params=_params("arbitrary"),
        name="band_bias",
    )(rrow)


def mixer_cd(z, bias_tab, *, bsz, seq, heads):
    t = z.shape[0]
    d = HEAD_DIM

    def col(which):
        return pl.BlockSpec((seq, d), lambda b, g: (b, (3 * (g // heads) + which) * heads + g % heads))

    return pl.pallas_call(
        functools.partial(_mixer_cd_kernel, heads=heads, scale=d ** -0.5),
        grid=(bsz, 2 * heads),
        in_specs=[
            col(0), col(1), col(2),
            pl.BlockSpec((None,) + bias_tab.shape[1:], lambda b, g: (jnp.minimum(g, heads - 1), 0, 0, 0)),
        ],
        out_specs=pl.BlockSpec((seq, d), lambda b, g: (b, g)),
        out_shape=jax.ShapeDtypeStruct((t, 2 * heads * d), BF16),
        scratch_shapes=[pltpu.VMEM((SB_TQ, d), F32), pltpu.VMEM((SB_TQ, LANES), F32)],
        compiler_params=_params("arbitrary", "arbitrary"),
        name="mixer_cd",
    )(z, z, z, bias_tab)


def rope_tables(seq_len):
    pos = jnp.arange(seq_len, dtype=F32)
    inv_freq = ROPE_BASE ** (-jnp.arange(0, HEAD_DIM, 2, dtype=F32) / HEAD_DIM)
    ang = pos[:, None] * inv_freq[None, :]
    cos, sin = jnp.cos(ang), jnp.sin(ang)
    return jnp.concatenate([cos, cos], axis=1), jnp.concatenate([-sin, sin], axis=1)


def retention_log_decay(heads):
    lg = jnp.log1p(-jnp.exp2(-(5.0 + jnp.arange(heads, dtype=F32))))
    return jnp.broadcast_to(lg[:, None, None], (heads, 1, MIX_CHUNK))


def _pad_lanes(a, value=0.0):
    return jnp.pad(a, ((0, 0), (0, LANES - a.shape[1])), constant_values=value)


def kernel(x, mem, ab_w_in, ab_gate_b, ab_conv_w, ab_conv_b, ab_wq, ab_wk, ab_ret_norm_g, ab_mlstm_norm_g, ab_w_out, cd_w_in, cd_rel_bias, cd_w_out, mix_ln_g, mix_ln_b, xa_wq, xa_wkv, xa_wo, xa_ln_g, xa_ln_b, moe_router_w, moe_router_b, moe_w_gate, moe_b_gate, moe_w_up, moe_b_up, moe_w_down, moe_b_down, moe_ln_g, moe_ln_b):
    bsz, seq, d = x.shape
    depth = mix_ln_g.shape[0]
    heads = GROUP_HEADS
    alpha = (2.0 * depth) ** 0.25
    t = bsz * seq
    cos2, sin2 = rope_tables(seq)
    log_g = retention_log_decay(heads)
    h = x.reshape(t, d)
    hb = h.astype(BF16)
    memb = mem.reshape(-1, d).astype(BF16)
    n_ab = 7 * GROUP_WIDTH
    n_cd = 6 * GROUP_WIDTH
    ab_w_in_t = jnp.swapaxes(ab_w_in, 1, 2)
    ab_w_out_b = ab_w_out.astype(BF16)
    cd_w_out_b = cd_w_out.astype(BF16)
    for layer in range(depth):
        i = layer // 2
        if layer % 2 == 0:
            z = matmul_nt_stacked(hb, ab_w_in_t, i, n_ab, tm=1024, tn=512, out_dtype=BF16)
            gates = gates_nt(hb, ab_w_in_t, i, n_ab, 2 * heads, _pad_lanes(ab_gate_b[i][None, :]), tm=1024)
            gain = jnp.concatenate([ab_ret_norm_g[i], ab_mlstm_norm_g[i]])
            yb = mixer_ab(z, gates, cos2, sin2, log_g, ab_conv_w, ab_conv_b, ab_wq, ab_wk, gain, i,
                          bsz=bsz, seq=seq, heads=heads)
            w_out = ab_w_out_b
        else:
            z = matmul_stacked(hb, cd_w_in, i, n_cd, tm=1024, tn=512, out_dtype=BF16)
            yb = mixer_cd(z, band_bias_table(cd_rel_bias[i]), bsz=bsz, seq=seq, heads=heads)
            w_out = cd_w_out_b
        h, hb = matmul_ln(yb, w_out, i, h, mix_ln_g[layer][None, :], mix_ln_b[layer][None, :], alpha)

        kvb = matmul_stacked(memb, xa_wkv, layer, xa_wkv.shape[2], tm=memb.shape[0], tn=512, out_dtype=BF16)
        h, hp, topw, topi, sel = xattn_ln(
            h, xa_wq[layer].astype(BF16), kvb, xa_wo[layer].astype(BF16),
            xa_ln_g[layer][None, :], xa_ln_b[layer][None, :],
            _pad_lanes(moe_router_w[layer]), _pad_lanes(moe_router_b[layer][None, :]), alpha, seq=seq)
        plan = moe_route_plan(topi, sel, tm=MOE_TM)
        y = moe_experts(hp, plan, moe_w_gate, moe_b_gate, moe_w_up, moe_b_up, moe_w_down, moe_b_down, layer, tm=MOE_TM)
        h, hb = moe_combine_ln(y, topw, h, moe_ln_g[layer][None, :], moe_ln_b[layer][None, :], alpha)
    return h.reshape(bsz, seq, d)
```

```python
import functools

import jax
import jax.numpy as jnp
from jax import lax
from jax.experimental import pallas as pl
from jax.experimental.pallas import tpu as pltpu

F32 = jnp.float32
BF16 = jnp.bfloat16

CHUNK = 64
HEAD_DIM = 128
GROUP_HEADS = 16
GROUP_WIDTH = GROUP_HEADS * HEAD_DIM
CONV_K = 4
PAST_CHUNKS = 8
REL_MAX = 2 * CHUNK
ROPE_BASE = 10000.0
XA_HEADS = 4
N_EXPERTS = 32
TOP_K = 4
SWIGLU_LIMIT = 7.0
SWIGLU_ALPHA = 1.702
LN_EPS = 1e-5
HN_EPS = 1e-6

LANES = 128
VMEM_LIMIT_BYTES = 58 * 1024 * 1024


def _params(*sem):
    return pltpu.CompilerParams(dimension_semantics=sem, vmem_limit_bytes=VMEM_LIMIT_BYTES)


def _mm_kernel(x_ref, w_ref, o_ref, wb_ref):
    @pl.when(pl.program_id(1) == 0)
    def _():
        wb_ref[...] = w_ref[...].astype(BF16)

    o_ref[...] = jnp.dot(x_ref[...], wb_ref[...], preferred_element_type=F32).astype(o_ref.dtype)


def matmul_stacked(x, w, layer, n_cols, *, tm, tn, out_dtype):
    m, k = x.shape
    return pl.pallas_call(
        _mm_kernel,
        grid=(n_cols // tn, m // tm),
        in_specs=[
            pl.BlockSpec((tm, k), lambda j, i: (i, 0)),
            pl.BlockSpec((None, k, tn), lambda j, i: (layer, 0, j)),
        ],
        out_specs=pl.BlockSpec((tm, tn), lambda j, i: (i, j)),
        out_shape=jax.ShapeDtypeStruct((m, n_cols), out_dtype),
        scratch_shapes=[pltpu.VMEM((k, tn), BF16)],
        compiler_params=_params("arbitrary", "arbitrary"),
        name="mm_in",
    )(x, w)


_NT = (((1,), (1,)), ((), ()))
_TN = (((0,), (0,)), ((), ()))


def _mm_nt_kernel(x_ref, w_ref, o_ref, wb_ref):
    @pl.when(pl.program_id(1) == 0)
    def _():
        wb_ref[...] = w_ref[...].astype(BF16)

    o_ref[...] = lax.dot_general(x_ref[...], wb_ref[...], _NT, preferred_element_type=F32).astype(o_ref.dtype)


def matmul_nt_stacked(x, wt, layer, n_cols, *, tm, tn, out_dtype):
    m, k = x.shape
    return pl.pallas_call(
        _mm_nt_kernel,
        grid=(n_cols // tn, m // tm),
        in_specs=[
            pl.BlockSpec((tm, k), lambda j, i: (i, 0)),
            pl.BlockSpec((None, tn, k), lambda j, i: (layer, j, 0)),
        ],
        out_specs=pl.BlockSpec((tm, tn), lambda j, i: (i, j)),
        out_shape=jax.ShapeDtypeStruct((m, n_cols), out_dtype),
        scratch_shapes=[pltpu.VMEM((tn, k), BF16)],
        compiler_params=_params("arbitrary", "arbitrary"),
        name="mm_in_nt",
    )(x, wt)


def _gates_kernel(x_ref, w_ref, b_ref, o_ref, wb_ref):
    @pl.when(pl.program_id(0) == 0)
    def _():
        wb_ref[...] = jnp.zeros_like(wb_ref)
        wb_ref[:w_ref.shape[0], :] = w_ref[...].astype(BF16)

    o_ref[...] = lax.dot_general(x_ref[...], wb_ref[...], _NT, preferred_element_type=F32) + b_ref[...]


def gates_nt(x, wt, layer, row0, n_rows, b_pad, *, tm):
    m, k = x.shape
    return pl.pallas_call(
        _gates_kernel,
        grid=(m // tm,),
        in_specs=[
            pl.BlockSpec((tm, k), lambda i: (i, 0)),
            pl.BlockSpec((None, n_rows, k), lambda i: (layer, row0 // n_rows, 0)),
            pl.BlockSpec((1, LANES), lambda i: (0, 0)),
        ],
        out_specs=pl.BlockSpec((tm, LANES), lambda i: (i, 0)),
        out_shape=jax.ShapeDtypeStruct((m, LANES), F32),
        scratch_shapes=[pltpu.VMEM((LANES, k), BF16)],
        compiler_params=_params("arbitrary"),
        name="gates",
    )(x, wt, b_pad)


def _pack_bf16_pair(y):
    half = y.shape[1] // 2
    hi = pltpu.bitcast(y[:, :half].astype(BF16).astype(F32), jnp.uint32)
    lo = pltpu.bitcast(y[:, half:].astype(BF16).astype(F32), jnp.uint32)
    return hi | (lo >> 16)


def _unpack_bf16_pair(w):
    return (pltpu.bitcast(w & jnp.uint32(0xFFFF0000), F32), pltpu.bitcast(w << 16, F32))


def _ln_rows(z_ref, g_ref, b_ref, of_ref, ob_ref, rows, packed=False):
    tm = z_ref.shape[0]

    def body(r, carry):
        sl = pl.ds(pl.multiple_of(r * rows, rows), rows)
        z = z_ref[sl, :]
        mu = jnp.mean(z, axis=-1, keepdims=True)
        zc = z - mu
        var = jnp.mean(zc * zc, axis=-1, keepdims=True)
        y = zc * lax.rsqrt(var + LN_EPS) * g_ref[...] + b_ref[...]
        of_ref[sl, :] = y
        ob_ref[sl, :] = _pack_bf16_pair(y) if packed else y.astype(BF16)
        return carry

    lax.fori_loop(0, tm // rows, body, 0, unroll=2)


def _mm_ln_kernel(x_ref, w_ref, h_ref, g_ref, b_ref, of_ref, ob_ref, *, nk, nj, tn, alpha):
    k = pl.program_id(1)
    j = pl.program_id(2)
    part = jnp.dot(x_ref[...], w_ref[...].astype(BF16), preferred_element_type=F32)
    for jj in range(nj):
        sl = slice(jj * tn, (jj + 1) * tn)

        @pl.when((j == jj) & (k == 0))
        def _():
            of_ref[:, sl] = alpha * h_ref[...] + part

        @pl.when((j == jj) & (k > 0))
        def _():
            of_ref[:, sl] += part

    @pl.when((k == nk - 1) & (j == nj - 1))
    def _():
        _ln_rows(of_ref, g_ref, b_ref, of_ref, ob_ref, 64)


def matmul_ln(x, w, layer, h, g, b, alpha, *, tm=512, tn=1024, tk=4096):
    m, kdim = x.shape
    n = h.shape[1]
    tk = min(tk, kdim)
    nk, nj = kdim // tk, n // tn
    in_specs = [
        pl.BlockSpec((tm, tk), lambda i, k, j: (i, k)),
        pl.BlockSpec((None, tk, tn), lambda i, k, j: (layer, k, j)),
        pl.BlockSpec((tm, tn), lambda i, k, j: (i, j)),
        pl.BlockSpec((1, n), lambda i, k, j: (0, 0)),
        pl.BlockSpec((1, n), lambda i, k, j: (0, 0)),
    ]
    args = [x, w, h, g, b]
    return pl.pallas_call(
        functools.partial(_mm_ln_kernel, nk=nk, nj=nj, tn=tn, alpha=alpha),
        grid=(m // tm, nk, nj),
        in_specs=in_specs,
        out_specs=[
            pl.BlockSpec((tm, n), lambda i, k, j: (i, 0)),
            pl.BlockSpec((tm, n), lambda i, k, j: (i, 0)),
        ],
        out_shape=[jax.ShapeDtypeStruct((m, n), F32), jax.ShapeDtypeStruct((m, n), BF16)],
        compiler_params=_params("arbitrary", "arbitrary", "arbitrary"),
        name="mm_ln",
    )(*args)


def _xattn_kernel(h_ref, wq_ref, kv_ref, wo_ref, g_ref, b_ref, wrh_ref, wrl_ref, br_ref,
                  of_ref, ob_ref, topw_ref, topi_ref, sel_ref, *, alpha, heads, hd):
    q = jnp.dot(h_ref[...].astype(BF16), wq_ref[...], preferred_element_type=F32)
    scale = hd ** -0.5
    outs = []
    for hh in range(heads):
        qh = (q[:, hh * hd:(hh + 1) * hd] * scale).astype(BF16)
        kh = kv_ref[:, hh * hd:(hh + 1) * hd]
        vh = kv_ref[:, (heads + hh) * hd:(heads + hh + 1) * hd]
        s = lax.dot_general(qh, kh, (((1,), (1,)), ((), ())), preferred_element_type=F32)
        s = s - jnp.max(s, axis=-1, keepdims=True)
        p = jnp.exp(s)
        l = jnp.sum(p, axis=-1, keepdims=True)
        o = jnp.dot(p.astype(BF16), vh, preferred_element_type=F32) / l
        outs.append(o.astype(BF16))
    o_all = jnp.concatenate(outs, axis=-1)
    of_ref[...] = alpha * h_ref[...] + jnp.dot(o_all, wo_ref[...], preferred_element_type=F32)
    _ln_rows(of_ref, g_ref, b_ref, of_ref, ob_ref, 64, packed=True)
    topw_ref[...], topi_ref[...], sel_ref[...] = _route_top4(of_ref[...], wrh_ref[...], wrl_ref[...], br_ref[...])


def xattn_ln(h, wq_b, kv_b, wo_b, g, b, w_router, b_router, alpha, *, seq, tm=512):
    m, d = h.shape
    lane_spec = pl.BlockSpec((tm, LANES), lambda i: (i, 0))
    xw = wq_b.shape[1]
    n_mem = kv_b.shape[0] // (m // seq)
    per_b = seq // tm

    def const_spec(shape):
        return pl.BlockSpec(shape, lambda i: (0, 0), pipeline_mode=pl.Buffered(1))

    return pl.pallas_call(
        functools.partial(_xattn_kernel, alpha=alpha, heads=XA_HEADS, hd=xw // XA_HEADS),
        grid=(m // tm,),
        in_specs=[
            pl.BlockSpec((tm, d), lambda i: (i, 0)),
            const_spec((d, xw)),
            pl.BlockSpec((n_mem, 2 * xw), lambda i: (i // per_b, 0)),
            const_spec((xw, d)),
            const_spec((1, d)),
            const_spec((1, d)),
            const_spec((d, LANES)),
            const_spec((d, LANES)),
            const_spec((1, LANES)),
        ],
        out_specs=[pl.BlockSpec((tm, d), lambda i: (i, 0)), pl.BlockSpec((tm, d // 2), lambda i: (i, 0)),
                   lane_spec, lane_spec, lane_spec],
        out_shape=[jax.ShapeDtypeStruct((m, d), F32), jax.ShapeDtypeStruct((m, d // 2), jnp.uint32),
                   jax.ShapeDtypeStruct((m, LANES), F32), jax.ShapeDtypeStruct((m, LANES), jnp.int32),
                   jax.ShapeDtypeStruct((m, LANES), F32)],
        compiler_params=_params("arbitrary"),
        name="xattn_ln",
    )(h, wq_b, kv_b, wo_b, g, b, *_split_bf16(w_router), b_router)


def _route_top4(h, w_hi, w_lo, b):
    h_hi, h_lo = _split_bf16(h)
    both = jnp.dot(h_hi, jnp.concatenate([w_hi, w_lo], axis=1), preferred_element_type=F32)
    logits = both[:, :LANES] + both[:, LANES:] + jnp.dot(h_lo, w_hi, preferred_element_type=F32) + b
    lane = lax.broadcasted_iota(jnp.int32, logits.shape, 1)
    neg = jnp.float32(-jnp.inf)
    masked = jnp.where(lane < N_EXPERTS, logits, neg)
    top_vals, top_idx = [], []
    sel = jnp.zeros_like(logits)
    for _ in range(TOP_K):
        mval = jnp.max(masked, axis=-1, keepdims=True)
        idx = jnp.min(jnp.where(masked == mval, lane, LANES), axis=-1, keepdims=True)
        hot = lane == idx
        top_vals.append(mval)
        top_idx.append(idx)
        sel = jnp.where(hot, 1.0, sel)
        masked = jnp.where(hot, neg, masked)
    exps = [jnp.exp(v - top_vals[0]) for v in top_vals]
    denom = exps[0]
    for e in exps[1:]:
        denom = denom + e
    topw = jnp.zeros_like(logits)
    topi = jnp.zeros(logits.shape, jnp.int32)
    for k in range(TOP_K):
        topw = jnp.where(lane == k, exps[k] / denom, topw)
        topi = jnp.where(lane == k, top_idx[k], topi)
    return topw, topi, sel


MOE_TM = 256


def moe_route_plan(topi, sel, *, tm):
    t = topi.shape[0]
    n_tiles = t * TOP_K // tm + N_EXPERTS
    order = jnp.argsort(topi[:, :TOP_K].reshape(-1), stable=True).astype(jnp.int32)
    counts = jnp.sum(sel[:, :N_EXPERTS], axis=0).astype(jnp.int32)
    first_pair = jnp.cumsum(counts) - counts
    tiles_e = (counts + tm - 1) // tm
    tile_end = jnp.cumsum(tiles_e)
    tile_start = tile_end - tiles_e
    tile_ids = jnp.arange(n_tiles, dtype=jnp.int32)
    tile_expert = jnp.minimum(jnp.sum(tile_end[None, :] <= tile_ids[:, None], axis=1), N_EXPERTS - 1).astype(jnp.int32)
    per_tile = jnp.stack([tile_start, counts, first_pair], axis=1)[tile_expert]
    local = (tile_ids - per_tile[:, 0])[:, None] * tm + jnp.arange(tm, dtype=jnp.int32)[None, :]
    valid = (tile_ids < tile_end[-1])[:, None] & (local < per_tile[:, 1:2])
    pair = order[jnp.clip(per_tile[:, 2:3] + local, 0, TOP_K * t - 1)]
    pad_row = TOP_K * t + (tile_ids % 2)[:, None] * tm + jnp.arange(tm, dtype=jnp.int32)[None, :]
    ydst = jnp.where(valid, (pair % TOP_K) * t + pair // TOP_K, pad_row)
    tok = ydst % t
    n_valid = tile_end[-1:].astype(jnp.int32)
    return (tok.reshape(n_tiles, 1, tm), ydst.reshape(n_tiles, 1, tm), tile_expert, n_valid)


def _moe_ffn_kernel(te_ref, nv_ref, tok0_ref, tok1_ref, tok2_ref, ydst_ref, h_hbm, wg_ref, wu_ref, wd_ref,
                    bg_ref, bu_ref, bd_ref, y_hbm, xg, og, wgu_b, wd_b, gsem, ssem, *, tm, ff):
    r = pl.program_id(0)
    nv = nv_ref[0]
    slot = lax.rem(r, 2)
    gslot = lax.rem(r, 3)

    def row_gather(idx_ref, s):
        for i in range(tm):
            pltpu.make_async_copy(h_hbm.at[pl.ds(idx_ref[0, i], 1), :], xg.at[s, pl.ds(i, 1), :], gsem.at[s]).start()

    def gather_wait(s):
        pltpu.make_async_copy(h_hbm.at[pl.ds(0, tm), :], xg.at[s], gsem.at[s]).wait()

    def scatter_wait(s):
        pltpu.make_async_copy(og.at[s], y_hbm.at[pl.ds(0, tm), :], ssem.at[s]).wait()

    @pl.when(r == 0)
    def _():
        row_gather(tok0_ref, 0)
        row_gather(tok1_ref, 1)
        og[1] = jnp.zeros(og.shape[1:], og.dtype)
        base = y_hbm.shape[0] - 2 * tm
        for part in range(2):
            fill = pltpu.make_async_copy(og.at[1], y_hbm.at[pl.ds(base + part * tm, tm), :], ssem.at[1])
            fill.start()
            fill.wait()

    @pl.when(r < nv)
    def _():
        gather_wait(gslot)

        @pl.when(r >= 2)
        def _():
            scatter_wait(slot)

        @pl.when((r == 0) | (te_ref[r] != te_ref[jnp.maximum(r - 1, 0)]))
        def _():
            wgu_b[:, :ff] = wg_ref[...].astype(BF16)
            wgu_b[:, ff:] = wu_ref[...].astype(BF16)
            wd_b[...] = wd_ref[...].astype(BF16)

        x_hi, x_lo = _unpack_bf16_pair(xg[gslot])
        half = x_hi.shape[1]
        gu = (jnp.dot(x_hi.astype(BF16), wgu_b[:half, :], preferred_element_type=F32)
              + jnp.dot(x_lo.astype(BF16), wgu_b[half:, :], preferred_element_type=F32))
        g = jnp.minimum(gu[:, :ff] + bg_ref[...], SWIGLU_LIMIT)
        u = jnp.clip(gu[:, ff:] + bu_ref[...], -SWIGLU_LIMIT, SWIGLU_LIMIT)
        act = (g * jax.nn.sigmoid(SWIGLU_ALPHA * g) * (u + 1.0)).astype(BF16)
        res = _pack_bf16_pair(jnp.dot(act, wd_b[...], preferred_element_type=F32) + bd_ref[...])
        row_gather(tok2_ref, lax.rem(r + 2, 3))
        og[slot] = res
        for i in range(tm):
            pltpu.make_async_copy(og.at[slot, pl.ds(i, 1), :], y_hbm.at[pl.ds(ydst_ref[0, i], 1), :],
                                  ssem.at[slot]).start()

        @pl.when(r == nv - 1)
        def _():
            gather_wait(lax.rem(r + 1, 3))
            gather_wait(lax.rem(r + 2, 3))
            scatter_wait(slot)

            @pl.when(r >= 1)
            def _():
                scatter_wait(1 - slot)


def moe_experts(hp, plan, w_gate, b_gate, w_up, b_up, w_down, b_down, layer, *, tm):
    tok, ydst, tile_expert, n_valid = plan
    t = hp.shape[0]
    d = 2 * hp.shape[1]
    n_tiles = tok.shape[0]
    n_e, ff = w_gate.shape[1], w_gate.shape[3]
    bg = b_gate.reshape(b_gate.shape[0], n_e, 1, ff)
    bu = b_up.reshape(b_up.shape[0], n_e, 1, ff)
    bd = b_down.reshape(b_down.shape[0], n_e, 1, d)

    def expert(r, te, nv):
        return te[jnp.minimum(r, nv[0] - 1)]

    smem_blk = functools.partial(pl.BlockSpec, (None, 1, tm), memory_space=pltpu.SMEM)
    grid_spec = pltpu.PrefetchScalarGridSpec(
        num_scalar_prefetch=2,
        grid=(n_tiles,),
        in_specs=[
            smem_blk(lambda r, te, nv: (jnp.minimum(r, nv[0] - 1), 0, 0)),
            smem_blk(lambda r, te, nv: (jnp.minimum(r + 1, nv[0] - 1), 0, 0)),
            smem_blk(lambda r, te, nv: (jnp.minimum(r + 2, nv[0] - 1), 0, 0)),
            smem_blk(lambda r, te, nv: (r, 0, 0)),
            pl.BlockSpec(memory_space=pl.ANY),
            pl.BlockSpec((None, None, d, ff), lambda r, te, nv: (layer, expert(r, te, nv), 0, 0)),
            pl.BlockSpec((None, None, d, ff), lambda r, te, nv: (layer, expert(r, te, nv), 0, 0)),
            pl.BlockSpec((None, None, ff, d), lambda r, te, nv: (layer, expert(r, te, nv), 0, 0)),
            pl.BlockSpec((None, None, 1, ff), lambda r, te, nv: (layer, expert(r, te, nv), 0, 0)),
            pl.BlockSpec((None, None, 1, ff), lambda r, te, nv: (layer, expert(r, te, nv), 0, 0)),
            pl.BlockSpec((None, None, 1, d), lambda r, te, nv: (layer, expert(r, te, nv), 0, 0)),
        ],
        out_specs=pl.BlockSpec(memory_space=pl.ANY),
        scratch_shapes=[
            pltpu.VMEM((3, tm, d // 2), jnp.uint32),
            pltpu.VMEM((2, tm, d // 2), jnp.uint32),
            pltpu.VMEM((d, 2 * ff), BF16),
            pltpu.VMEM((ff, d), BF16),
            pltpu.SemaphoreType.DMA((3,)),
            pltpu.SemaphoreType.DMA((2,)),
        ],
    )
    return pl.pallas_call(
        functools.partial(_moe_ffn_kernel, tm=tm, ff=ff),
        grid_spec=grid_spec,
        out_shape=jax.ShapeDtypeStruct((TOP_K * t + 2 * tm, d // 2), jnp.uint32),
        compiler_params=_params("arbitrary"),
        name="moe_experts",
    )(tile_expert, n_valid, tok, tok, tok, ydst, hp, w_gate, w_up, w_down, bg, bu, bd)


def _moe_combine_kernel(y0_ref, y1_ref, y2_ref, y3_ref, w_ref, h_ref, g_ref, b_ref, of_ref, ob_ref, *, alpha, rows):
    tm = h_ref.shape[0]
    y_refs = (y0_ref, y1_ref, y2_ref, y3_ref)

    def body(rr, carry):
        sl = pl.ds(pl.multiple_of(rr * rows, rows), rows)
        w = w_ref[sl, :]
        hrow = h_ref[sl, :]
        half = hrow.shape[1] // 2
        z_hi = alpha * hrow[:, :half]
        z_lo = alpha * hrow[:, half:]
        for k, y_ref in enumerate(y_refs):
            y_hi, y_lo = _unpack_bf16_pair(y_ref[sl, :])
            z_hi = z_hi + w[:, k:k + 1] * y_hi
            z_lo = z_lo + w[:, k:k + 1] * y_lo
        z = jnp.concatenate([z_hi, z_lo], axis=1)
        mu = jnp.mean(z, axis=-1, keepdims=True)
        zc = z - mu
        var = jnp.mean(zc * zc, axis=-1, keepdims=True)
        y = zc * lax.rsqrt(var + LN_EPS) * g_ref[...] + b_ref[...]
        of_ref[sl, :] = y
        ob_ref[sl, :] = y.astype(BF16)
        return carry

    lax.fori_loop(0, tm // rows, body, 0)


def moe_combine_ln(y, topw, h, g, b, alpha, *, tm=128):
    t, d = h.shape
    nb = t // tm

    def y_spec(k):
        return pl.BlockSpec((tm, d // 2), lambda i: (k * nb + i, 0))

    row_spec = pl.BlockSpec((tm, d), lambda i: (i, 0))
    return pl.pallas_call(
        functools.partial(_moe_combine_kernel, alpha=alpha, rows=32),
        grid=(nb,),
        in_specs=[y_spec(0), y_spec(1), y_spec(2), y_spec(3),
                  pl.BlockSpec((tm, LANES), lambda i: (i, 0)), row_spec,
                  pl.BlockSpec((1, d), lambda i: (0, 0)), pl.BlockSpec((1, d), lambda i: (0, 0))],
        out_specs=[row_spec, row_spec],
        out_shape=[jax.ShapeDtypeStruct((t, d), F32), jax.ShapeDtypeStruct((t, d), BF16)],
        compiler_params=_params("arbitrary"),
        name="moe_combine_ln",
    )(y, y, y, y, topw, h, g, b)


NEG = -1e30
LOG2E = 1.4426950408889634
MIX_CHUNK = 256
BAND_BLOCK = 256
SB_TQ = 512
SB_TK = 256

def _head_norm_rows(x, gain):
    mu = jnp.mean(x, axis=-1, keepdims=True)
    xc = x - mu
    var = jnp.mean(xc * xc, axis=-1, keepdims=True)
    return xc * lax.rsqrt(var + HN_EPS) * gain


def _log_sigmoid(x):
    return jnp.minimum(x, 0.0) - jnp.log(1.0 + jnp.exp(-jnp.abs(x)))


def _split_bf16(x):
    hi = x.astype(BF16)
    lo = (x - hi.astype(F32)).astype(BF16)
    return hi, lo


def _lane_select(x, lane_idx):
    lane = lax.broadcasted_iota(jnp.int32, x.shape, 1)
    col = jnp.sum(jnp.where(lane == lane_idx, x, 0.0), axis=-1, keepdims=True)
    return jnp.broadcast_to(col, x.shape)


def _retention_body(q_ref, k_ref, v_ref, g_ref, cos_ref, sin_ref, lg_ref, gain_ref, o_ref,
                    qs_ref, ks_ref, st_ref, *, scale):
    seq = q_ref.shape[0]
    L = MIX_CHUNK
    half = HEAD_DIM // 2
    cos = cos_ref[...]
    sin = sin_ref[...]
    q = q_ref[...].astype(F32)
    k = k_ref[...].astype(F32)
    qs_ref[...] = (q * cos + pltpu.roll(q, half, 1) * sin).astype(BF16)
    ks_ref[...] = ((k * cos + pltpu.roll(k, half, 1) * sin) * scale).astype(BF16)

    lg = lg_ref[...]
    ri = lax.broadcasted_iota(jnp.int32, (L, L), 0)
    ci = lax.broadcasted_iota(jnp.int32, (L, L), 1)
    intra = jnp.where(ri >= ci, jnp.exp(lg * jnp.maximum(ri - ci, 0).astype(F32)), 0.0)
    rr = lax.broadcasted_iota(jnp.int32, (L, HEAD_DIM), 0).astype(F32)
    lg_d = lg[:, :HEAD_DIM]
    q_dec = jnp.exp(lg_d * (rr + 1.0))
    k_dec = jnp.exp(lg_d * (L - 1.0 - rr))
    c_dec = jnp.exp(lg_d * float(L))
    gain = gain_ref[...]
    st_ref[...] = jnp.zeros_like(st_ref)

    def chunk(c, carry):
        sl = pl.ds(pl.multiple_of(c * L, L), L)
        qc = qs_ref[sl, :]
        kc = ks_ref[sl, :]
        vc = v_ref[sl, :]
        state = st_ref[:, :HEAD_DIM]
        att = lax.dot_general(qc, kc, _NT, preferred_element_type=F32) * intra
        o = (jnp.dot(att.astype(BF16), vc, preferred_element_type=F32)
             + jnp.dot((qc.astype(F32) * q_dec).astype(BF16), state.astype(BF16), preferred_element_type=F32))
        st_ref[:, :HEAD_DIM] = state * c_dec + lax.dot_general(
            (kc.astype(F32) * k_dec).astype(BF16), vc, _TN, preferred_element_type=F32)
        gv = g_ref[sl, :].astype(F32)
        o_ref[sl, :] = (_head_norm_rows(o, gain) * (gv * jax.nn.sigmoid(gv))).astype(o_ref.dtype)
        return carry

    lax.fori_loop(0, seq // L, chunk, 0, unroll=True)


def _mlstm_body(u_ref, v_ref, og_ref, gates_ref, cw_ref, cb_ref, wq_ref, wk_ref, gain_ref, o_ref,
                qs_ref, ks_ref, st_ref, m_ref, *, head, heads, scale):
    seq = u_ref.shape[0]
    L = MIX_CHUNK
    d = HEAD_DIM
    x = u_ref[...].astype(F32)
    row = lax.broadcasted_iota(jnp.int32, x.shape, 0)
    cw = cw_ref[...]
    y = x * cw[CONV_K - 1:CONV_K, :] + cb_ref[...]
    for sh in range(1, CONV_K):
        xs = jnp.where(row >= sh, pltpu.roll(x, sh, 0), 0.0)
        y = y + xs * cw[CONV_K - 1 - sh:CONV_K - sh, :]
    ub = (y * jax.nn.sigmoid(y)).astype(BF16)
    qs_ref[...] = jnp.dot(ub, wq_ref[...].astype(BF16), preferred_element_type=F32).astype(BF16)
    ks_ref[...] = (jnp.dot(ub, wk_ref[...].astype(BF16), preferred_element_type=F32) * scale).astype(BF16)

    ri = lax.broadcasted_iota(jnp.int32, (L, L), 0)
    ci = lax.broadcasted_iota(jnp.int32, (L, L), 1)
    causal = ri >= ci
    tri = jnp.where(causal, 1.0, 0.0).astype(BF16)
    ones_v = jnp.ones((L, d), BF16)
    gain = gain_ref[...]
    st_ref[...] = jnp.zeros_like(st_ref)
    m_ref[...] = jnp.zeros_like(m_ref)

    def chunk(c, carry):
        sl = pl.ds(pl.multiple_of(c * L, L), L)
        qc = qs_ref[sl, :]
        kc = ks_ref[sl, :]
        v_ext = jnp.concatenate([v_ref[sl, :], ones_v], axis=1)
        gts = gates_ref[sl, :]
        ic = _lane_select(gts, head)
        lf = _log_sigmoid(_lane_select(gts, heads + head))
        lf_hi, lf_lo = _split_bf16(lf)
        bcum = (jnp.dot(tri, lf_hi, preferred_element_type=F32)
                + jnp.dot(tri, lf_lo, preferred_element_type=F32))
        m_st = m_ref[...]
        src = jnp.transpose(ic - bcum)[:1, :]
        bcum2 = jnp.concatenate([bcum, bcum], axis=1)
        log_intra = jnp.where(causal, bcum2 + src, NEG)
        m_intra = jnp.max(log_intra, axis=-1, keepdims=True)
        log_cross = bcum + m_st
        m_row = jnp.maximum(log_cross, m_intra)
        m_row2 = jnp.concatenate([m_row, m_row], axis=1)
        w_intra = jnp.exp(log_intra - m_row2)
        w_cross = jnp.exp(log_cross - m_row)
        w_cross2 = jnp.concatenate([w_cross, w_cross], axis=1)
        qk = lax.dot_general(qc, kc, _NT, preferred_element_type=F32) * w_intra
        state = st_ref[...]
        res = (jnp.dot(qk.astype(BF16), v_ext, preferred_element_type=F32)
               + w_cross2 * jnp.dot(qc, state.astype(BF16), preferred_element_type=F32))
        num = res[:, :d]
        den = res[:, d:]
        hh = num / jnp.maximum(jnp.abs(den), jnp.exp(-m_row))
        og = og_ref[sl, :].astype(F32)
        o_ref[sl, :] = _head_norm_rows(hh * jax.nn.sigmoid(og), gain).astype(o_ref.dtype)
        b_last = bcum[L - 1:L, :]
        log_state = b_last - bcum + ic
        m_new = jnp.maximum(b_last + m_st, jnp.max(log_state, axis=0, keepdims=True))
        decay = jnp.exp(b_last + m_st - m_new)
        kw = (kc.astype(F32) * jnp.exp(log_state - m_new)).astype(BF16)
        decay2 = jnp.concatenate([decay, decay], axis=1)
        st_ref[...] = decay2 * state + lax.dot_general(kw, v_ext, _TN, preferred_element_type=F32)
        m_ref[...] = m_new
        return carry

    lax.fori_loop(0, seq // L, chunk, 0, unroll=True)


def _mixer_ab_kernel(a0_ref, a1_ref, a2_ref, a3_ref, gates_ref, cos_ref, sin_ref, lg_ref, cw_ref, cb_ref,
                     wq_ref, wk_ref, gain_ref, o_ref, qs_ref, ks_ref, st_ref, m_ref, *, heads, scale):
    g = pl.program_id(1)

    @pl.when(g < heads)
    def _():
        _retention_body(a0_ref, a1_ref, a2_ref, a3_ref, cos_ref, sin_ref, lg_ref, gain_ref, o_ref,
                        qs_ref, ks_ref, st_ref, scale=scale)

    @pl.when(g >= heads)
    def _():
        _mlstm_body(a0_ref, a1_ref, a2_ref, gates_ref, cw_ref, cb_ref, wq_ref, wk_ref, gain_ref, o_ref,
                    qs_ref, ks_ref, st_ref, m_ref, head=g - heads, heads=heads, scale=scale)


def mixer_ab(z, gates, cos2, sin2, log_g, conv_w, conv_b, wq_m, wk_m, gain, layer, *, bsz, seq, heads):
    t = z.shape[0]
    d = HEAD_DIM
    L = MIX_CHUNK

    def col(base_ret, base_ml):
        def index(b, g):
            is_ml = g // heads
            return (b, (1 - is_ml) * (base_ret * heads + g) + is_ml * (base_ml * heads + g - heads))
        return pl.BlockSpec((seq, d), index)

    def ml_head(g):
        return jnp.maximum(g - heads, 0)

    conv_w4 = conv_w.reshape(conv_w.shape[0], CONV_K, heads, d).transpose(0, 2, 1, 3)
    conv_b4 = conv_b.reshape(conv_b.shape[0], heads, 1, d)
    gain4 = gain.reshape(2 * heads, 1, d)
    return pl.pallas_call(
        functools.partial(_mixer_ab_kernel, heads=heads, scale=d ** -0.5),
        grid=(bsz, 2 * heads),
        in_specs=[
            col(0, 4), col(1, 5), col(2, 6), col(3, 6),
            pl.BlockSpec((seq, LANES), lambda b, g: (b, 0)),
            pl.BlockSpec((seq, d), lambda b, g: (0, 0)),
            pl.BlockSpec((seq, d), lambda b, g: (0, 0)),
            pl.BlockSpec((None, 1, L), lambda b, g: (jnp.minimum(g, heads - 1), 0, 0)),
            pl.BlockSpec((None, None, CONV_K, d), lambda b, g: (layer, ml_head(g), 0, 0)),
            pl.BlockSpec((None, None, 1, d), lambda b, g: (layer, ml_head(g), 0, 0)),
            pl.BlockSpec((None, None, d, d), lambda b, g: (layer, ml_head(g), 0, 0)),
            pl.BlockSpec((None, None, d, d), lambda b, g: (layer, ml_head(g), 0, 0)),
            pl.BlockSpec((None, 1, d), lambda b, g: (g, 0, 0)),
        ],
        out_specs=pl.BlockSpec((seq, d), lambda b, g: (b, g)),
        out_shape=jax.ShapeDtypeStruct((t, 2 * heads * d), BF16),
        scratch_shapes=[
            pltpu.VMEM((seq, d), BF16),
            pltpu.VMEM((seq, d), BF16),
            pltpu.VMEM((d, 2 * d), F32),
            pltpu.VMEM((1, d), F32),
        ],
        compiler_params=_params("arbitrary", "arbitrary"),
        name="mixer_ab",
    )(z, z, z, z, gates, cos2, sin2, log_g, conv_w4, conv_b4, wq_m, wk_m, gain4)


def _band_body(q_ref, k_ref, v_ref, bias_ref, o_ref, *, scale):
    seq = q_ref.shape[0]
    bq = BAND_BLOCK
    n_back = bias_ref.shape[0]

    def block(i, carry):
        sl = pl.ds(pl.multiple_of(i * bq, bq), bq)
        q = q_ref[sl, :]
        scores, vals = [], []
        for dlt in range(n_back):
            ks = pl.ds(pl.multiple_of(jnp.maximum(i - dlt, 0) * bq, bq), bq)
            s = lax.dot_general(q, k_ref[ks, :], _NT, preferred_element_type=F32) * (scale * LOG2E) + bias_ref[dlt]
            scores.append(jnp.where(i - dlt >= 0, s, NEG))
            vals.append(v_ref[ks, :])
        m = jnp.max(scores[0], axis=-1, keepdims=True)
        for s in scores[1:]:
            m = jnp.maximum(m, jnp.max(s, axis=-1, keepdims=True))
        acc = jnp.zeros((bq, HEAD_DIM), F32)
        l = jnp.zeros((bq, 1), F32)
        for s, vv in zip(scores, vals):
            p = jnp.exp2(s - m)
            l = l + jnp.sum(p, axis=-1, keepdims=True)
            acc = acc + jnp.dot(p.astype(BF16), vv, preferred_element_type=F32)
        o_ref[sl, :] = (acc / l).astype(o_ref.dtype)
        return carry

    lax.fori_loop(0, seq // bq, block, 0, unroll=True)


def _stick_breaking_body(q_ref, k_ref, v_ref, o_ref, acc_ref, r_ref, *, scale):
    seq = q_ref.shape[0]
    tq, tk = SB_TQ, SB_TK
    n_sub = tq // tk
    ri = lax.broadcasted_iota(jnp.int32, (tk, tk), 0)
    ci = lax.broadcasted_iota(jnp.int32, (tk, tk), 1)
    suffix = jnp.where(ri > ci, 1.0, 0.0).astype(BF16)
    strict = (lax.broadcasted_iota(jnp.int32, (tq, tq), 1)
              < lax.broadcasted_iota(jnp.int32, (tq, tq), 0))

    def one_group(q, k0, masked):
        kb = k_ref[pl.ds(k0, tq), :]
        vb = v_ref[pl.ds(k0, tq), :]
        z = lax.dot_general(q, kb, _NT, preferred_element_type=F32) * (scale * LOG2E)
        sp = jnp.maximum(z, 0.0) + jnp.log2(1.0 + jnp.exp2(-jnp.abs(z)))
        ls_pos = z - sp
        if masked:
            sp = jnp.where(strict, sp, 0.0)
        sp_b = sp.astype(BF16)
        r = r_ref[...]
        pieces = []
        for s in reversed(range(n_sub)):
            sl = slice(s * tk, (s + 1) * tk)
            between = jnp.dot(sp_b[:, sl], suffix, preferred_element_type=F32)
            pieces.append(ls_pos[:, sl] - between - jnp.concatenate([r] * (tk // LANES), axis=1))
            r = r + jnp.sum(sp[:, sl], axis=-1, keepdims=True)
        p = jnp.exp2(jnp.concatenate(pieces[::-1], axis=1))
        if masked:
            p = jnp.where(strict, p, 0.0)
        acc_ref[...] += jnp.dot(p.astype(BF16), vb, preferred_element_type=F32)
        r_ref[...] = r

    for i in range(seq // tq):
        q0 = i * tq
        q = q_ref[pl.ds(q0, tq), :]
        acc_ref[...] = jnp.zeros_like(acc_ref)
        r_ref[...] = jnp.zeros_like(r_ref)
        one_group(q, q0, True)

        def below(jj, c, q=q, i=i):
            one_group(q, pl.multiple_of((i - 1 - jj) * tq, tq), False)
            return c

        lax.fori_loop(0, i, below, 0, unroll=True)
        o_ref[pl.ds(q0, tq), :] = acc_ref[...].astype(o_ref.dtype)


def _mixer_cd_kernel(q_ref, k_ref, v_ref, bias_ref, o_ref, acc_ref, r_ref, *, heads, scale):
    g = pl.program_id(1)

    @pl.when(g < heads)
    def _():
        _band_body(q_ref, k_ref, v_ref, bias_ref, o_ref, scale=scale)

    @pl.when(g >= heads)
    def _():
        _stick_breaking_body(q_ref, k_ref, v_ref, o_ref, acc_ref, r_ref, scale=scale)


def _band_bias_kernel(rrow_ref, o_ref):
    bq = BAND_BLOCK
    shift = CHUNK.bit_length() - 1
    qo = lax.broadcasted_iota(jnp.int32, (bq, bq), 0)
    ck = jnp.right_shift(lax.broadcasted_iota(jnp.int32, (bq, bq), 1), shift)
    for dlt in range(o_ref.shape[0]):
        x = jnp.broadcast_to(rrow_ref[dlt], (bq, 2 * bq))
        toep = pltpu.roll(x, 0, 1, stride=1, stride_axis=0)[:, :bq]
        cq = jnp.right_shift(qo + bq * dlt, shift)
        allowed = (ck <= cq) & (ck >= cq - PAST_CHUNKS)
        o_ref[dlt] = jnp.where(allowed, toep * LOG2E, NEG)


def band_bias_table(rel_bias):
    bq = BAND_BLOCK
    heads = rel_bias.shape[0]
    n_back = PAST_CHUNKS * CHUNK // bq + 1
    m = jnp.arange(2 * bq)
    key_minus_query = jnp.where(m < bq, m, m - 2 * bq)
    dist = bq * jnp.arange(n_back)[:, None] - key_minus_query[None, :]
    idx = jnp.clip(dist, -(CHUNK - 1), REL_MAX) + (CHUNK - 1)
    rrow = rel_bias.astype(F32)[:, idx].reshape(heads, n_back, 1, 2 * bq)
    return pl.pallas_call(
        _band_bias_kernel,
        grid=(heads,),
        in_specs=[pl.BlockSpec((None, n_back, 1, 2 * bq), lambda hh: (hh, 0, 0, 0))],
        out_specs=pl.BlockSpec((None, n_back, bq, bq), lambda hh: (hh, 0, 0, 0)),
        out_shape=jax.ShapeDtypeStruct((heads, n_back, bq, bq), F32),
        compiler_params=_params("arbitrary"),
        name="band_bias",
    )(rrow)


def mixer_cd(z, bias_tab, *, bsz, seq, heads):
    t = z.shape[0]
    d = HEAD_DIM

    def col(which):
        return pl.BlockSpec((seq, d), lambda b, g: (b, (3 * (g // heads) + which) * heads + g % heads))

    return pl.pallas_call(
        functools.partial(_mixer_cd_kernel, heads=heads, scale=d ** -0.5),
        grid=(bsz, 2 * heads),
        in_specs=[
            col(0), col(1), col(2),
            pl.BlockSpec((None,) + bias_tab.shape[1:], lambda b, g: (jnp.minimum(g, heads - 1), 0, 0, 0)),
        ],
        out_specs=pl.BlockSpec((seq, d), lambda b, g: (b, g)),
        out_shape=jax.ShapeDtypeStruct((t, 2 * heads * d), BF16),
        scratch_shapes=[pltpu.VMEM((SB_TQ, d), F32), pltpu.VMEM((SB_TQ, LANES), F32)],
        compiler_params=_params("arbitrary", "arbitrary"),
        name="mixer_cd",
    )(z, z, z, bias_tab)


def rope_tables(seq_len):
    pos = jnp.arange(seq_len, dtype=F32)
    inv_freq = ROPE_BASE ** (-jnp.arange(0, HEAD_DIM, 2, dtype=F32) / HEAD_DIM)
    ang = pos[:, None] * inv_freq[None, :]
    cos, sin = jnp.cos(ang), jnp.sin(ang)
    return jnp.concatenate([cos, cos], axis=1), jnp.concatenate([-sin, sin], axis=1)


def retention_log_decay(heads):
    lg = jnp.log1p(-jnp.exp2(-(5.0 + jnp.arange(heads, dtype=F32))))
    return jnp.broadcast_to(lg[:, None, None], (heads, 1, MIX_CHUNK))


def _pad_lanes(a, value=0.0):
    return jnp.pad(a, ((0, 0), (0, LANES - a.shape[1])), constant_values=value)


def kernel(x, mem, ab_w_in, ab_gate_b, ab_conv_w, ab_conv_b, ab_wq, ab_wk, ab_ret_norm_g, ab_mlstm_norm_g, ab_w_out, cd_w_in, cd_rel_bias, cd_w_out, mix_ln_g, mix_ln_b, xa_wq, xa_wkv, xa_wo, xa_ln_g, xa_ln_b, moe_router_w, moe_router_b, moe_w_gate, moe_b_gate, moe_w_up, moe_b_up, moe_w_down, moe_b_down, moe_ln_g, moe_ln_b):
    bsz, seq, d = x.shape
    depth = mix_ln_g.shape[0]
    heads = GROUP_HEADS
    alpha = (2.0 * depth) ** 0.25
    t = bsz * seq
    cos2, sin2 = rope_tables(seq)
    log_g = retention_log_decay(heads)
    h = x.reshape(t, d)
    hb = h.astype(BF16)
    memb = mem.reshape(-1, d).astype(BF16)
    n_ab = 7 * GROUP_WIDTH
    n_cd = 6 * GROUP_WIDTH
    ab_w_in_t = jnp.swapaxes(ab_w_in, 1, 2)
    ab_w_out_b = ab_w_out.astype(BF16)
    cd_w_out_b = cd_w_out.astype(BF16)
    for layer in range(depth):
        i = layer // 2
        if layer % 2 == 0:
            z = matmul_nt_stacked(hb, ab_w_in_t, i, n_ab, tm=1024, tn=512, out_dtype=BF16)
            gates = gates_nt(hb, ab_w_in_t, i, n_ab, 2 * heads, _pad_lanes(ab_gate_b[i][None, :]), tm=1024)
            gain = jnp.concatenate([ab_ret_norm_g[i], ab_mlstm_norm_g[i]])
            yb = mixer_ab(z, gates, cos2, sin2, log_g, ab_conv_w, ab_conv_b, ab_wq, ab_wk, gain, i,
                          bsz=bsz, seq=seq, heads=heads)
            w_out = ab_w_out_b
        else:
            z = matmul_stacked(hb, cd_w_in, i, n_cd, tm=1024, tn=512, out_dtype=BF16)
            yb = mixer_cd(z, band_bias_table(cd_rel_bias[i]), bsz=bsz, seq=seq, heads=heads)
            w_out = cd_w_out_b
        h, hb = matmul_ln(yb, w_out, i, h, mix_ln_g[layer][None, :], mix_ln_b[layer][None, :], alpha)

        kvb = matmul_stacked(memb, xa_wkv, layer, xa_wkv.shape[2], tm=memb.shape[0], tn=512, out_dtype=BF16)
        h, hp, topw, topi, sel = xattn_ln(
            h, xa_wq[layer].astype(BF16), kvb, xa_wo[layer].astype(BF16),
            xa_ln_g[layer][None, :], xa_ln_b[layer][None, :],
            _pad_lanes(moe_router_w[layer]), _pad_lanes(moe_router_b[layer][None, :]), alpha, seq=seq)
        plan = moe_route_plan(topi, sel, tm=MOE_TM)
        y = moe_experts(hp, plan, moe_w_gate, moe_b_gate, moe_w_up, moe_b_up, moe_w_down, moe_b_down, layer, tm=MOE_TM)
        h, hb = moe_combine_ln(y, topw, h, moe_ln_g[layer][None, :], moe_ln_b[layer][None, :], alpha)
    return h.reshape(bsz, seq, d)
```

```python
import functools

import jax
import jax.numpy as jnp
from jax import lax
from jax.experimental import pallas as pl
from jax.experimental.pallas import tpu as pltpu

F32 = jnp.float32
BF16 = jnp.bfloat16

CHUNK = 64
HEAD_DIM = 128
GROUP_HEADS = 16
GROUP_WIDTH = GROUP_HEADS * HEAD_DIM
CONV_K = 4
PAST_CHUNKS = 8
REL_MAX = 2 * CHUNK
ROPE_BASE = 10000.0
XA_HEADS = 4
N_EXPERTS = 32
TOP_K = 4
SWIGLU_LIMIT = 7.0
SWIGLU_ALPHA = 1.702
LN_EPS = 1e-5
HN_EPS = 1e-6

LANES = 128
VMEM_LIMIT_BYTES = 58 * 1024 * 1024


def _params(*sem):
    return pltpu.CompilerParams(dimension_semantics=sem, vmem_limit_bytes=VMEM_LIMIT_BYTES)


def _mm_kernel(x_ref, w_ref, o_ref, wb_ref):
    @pl.when(pl.program_id(1) == 0)
    def _():
        wb_ref[...] = w_ref[...].astype(BF16)

    o_ref[...] = jnp.dot(x_ref[...], wb_ref[...], preferred_element_type=F32).astype(o_ref.dtype)


def matmul_stacked(x, w, layer, n_cols, *, tm, tn, out_dtype):
    m, k = x.shape
    return pl.pallas_call(
        _mm_kernel,
        grid=(n_cols // tn, m // tm),
        in_specs=[
            pl.BlockSpec((tm, k), lambda j, i: (i, 0)),
            pl.BlockSpec((None, k, tn), lambda j, i: (layer, 0, j)),
        ],
        out_specs=pl.BlockSpec((tm, tn), lambda j, i: (i, j)),
        out_shape=jax.ShapeDtypeStruct((m, n_cols), out_dtype),
        scratch_shapes=[pltpu.VMEM((k, tn), BF16)],
        compiler_params=_params("arbitrary", "arbitrary"),
        name="mm_in",
    )(x, w)


_NT = (((1,), (1,)), ((), ()))
_TN = (((0,), (0,)), ((), ()))


def _mm_nt_kernel(x_ref, w_ref, o_ref, wb_ref):
    @pl.when(pl.program_id(1) == 0)
    def _():
        wb_ref[...] = w_ref[...].astype(BF16)

    o_ref[...] = lax.dot_general(x_ref[...], wb_ref[...], _NT, preferred_element_type=F32).astype(o_ref.dtype)


def matmul_nt_stacked(x, wt, layer, n_cols, *, tm, tn, out_dtype):
    m, k = x.shape
    return pl.pallas_call(
        _mm_nt_kernel,
        grid=(n_cols // tn, m // tm),
        in_specs=[
            pl.BlockSpec((tm, k), lambda j, i: (i, 0)),
            pl.BlockSpec((None, tn, k), lambda j, i: (layer, j, 0)),
        ],
        out_specs=pl.BlockSpec((tm, tn), lambda j, i: (i, j)),
        out_shape=jax.ShapeDtypeStruct((m, n_cols), out_dtype),
        scratch_shapes=[pltpu.VMEM((tn, k), BF16)],
        compiler_params=_params("arbitrary", "arbitrary"),
        name="mm_in_nt",
    )(x, wt)


def _gates_kernel(x_ref, w_ref, b_ref, o_ref, wb_ref):
    @pl.when(pl.program_id(0) == 0)
    def _():
        wb_ref[...] = jnp.zeros_like(wb_ref)
        wb_ref[:w_ref.shape[0], :] = w_ref[...].astype(BF16)

    o_ref[...] = lax.dot_general(x_ref[...], wb_ref[...], _NT, preferred_element_type=F32) + b_ref[...]


def gates_nt(x, wt, layer, row0, n_rows, b_pad, *, tm):
    m, k = x.shape
    return pl.pallas_call(
        _gates_kernel,
        grid=(m // tm,),
        in_specs=[
            pl.BlockSpec((tm, k), lambda i: (i, 0)),
            pl.BlockSpec((None, n_rows, k), lambda i: (layer, row0 // n_rows, 0)),
            pl.BlockSpec((1, LANES), lambda i: (0, 0)),
        ],
        out_specs=pl.BlockSpec((tm, LANES), lambda i: (i, 0)),
        out_shape=jax.ShapeDtypeStruct((m, LANES), F32),
        scratch_shapes=[pltpu.VMEM((LANES, k), BF16)],
        compiler_params=_params("arbitrary"),
        name="gates",
    )(x, wt, b_pad)


def _pack_bf16_pair(y):
    half = y.shape[1] // 2
    hi = pltpu.bitcast(y[:, :half].astype(BF16).astype(F32), jnp.uint32)
    lo = pltpu.bitcast(y[:, half:].astype(BF16).astype(F32), jnp.uint32)
    return hi | (lo >> 16)


def _unpack_bf16_pair(w):
    return (pltpu.bitcast(w & jnp.uint32(0xFFFF0000), F32), pltpu.bitcast(w << 16, F32))


def _ln_rows(z_ref, g_ref, b_ref, of_ref, ob_ref, rows, packed=False):
    tm = z_ref.shape[0]

    def body(r, carry):
        sl = pl.ds(pl.multiple_of(r * rows, rows), rows)
        z = z_ref[sl, :]
        mu = jnp.mean(z, axis=-1, keepdims=True)
        zc = z - mu
        var = jnp.mean(zc * zc, axis=-1, keepdims=True)
        y = zc * lax.rsqrt(var + LN_EPS) * g_ref[...] + b_ref[...]
        of_ref[sl, :] = y
        ob_ref[sl, :] = _pack_bf16_pair(y) if packed else y.astype(BF16)
        return carry

    lax.fori_loop(0, tm // rows, body, 0, unroll=2)


def _mm_ln_kernel(x_ref, w_ref, h_ref, g_ref, b_ref, of_ref, ob_ref, *, nk, nj, tn, alpha):
    k = pl.program_id(1)
    j = pl.program_id(2)
    part = jnp.dot(x_ref[...], w_ref[...].astype(BF16), preferred_element_type=F32)
    for jj in range(nj):
        sl = slice(jj * tn, (jj + 1) * tn)

        @pl.when((j == jj) & (k == 0))
        def _():
            of_ref[:, sl] = alpha * h_ref[...] + part

        @pl.when((j == jj) & (k > 0))
        def _():
            of_ref[:, sl] += part

    @pl.when((k == nk - 1) & (j == nj - 1))
    def _():
        _ln_rows(of_ref, g_ref, b_ref, of_ref, ob_ref, 64)


def matmul_ln(x, w, layer, h, g, b, alpha, *, tm=512, tn=1024, tk=4096):
    m, kdim = x.shape
    n = h.shape[1]
    tk = min(tk, kdim)
    nk, nj = kdim // tk, n // tn
    in_specs = [
        pl.BlockSpec((tm, tk), lambda i, k, j: (i, k)),
        pl.BlockSpec((None, tk, tn), lambda i, k, j: (layer, k, j)),
        pl.BlockSpec((tm, tn), lambda i, k, j: (i, j)),
        pl.BlockSpec((1, n), lambda i, k, j: (0, 0)),
        pl.BlockSpec((1, n), lambda i, k, j: (0, 0)),
    ]
    args = [x, w, h, g, b]
    return pl.pallas_call(
        functools.partial(_mm_ln_kernel, nk=nk, nj=nj, tn=tn, alpha=alpha),
        grid=(m // tm, nk, nj),
        in_specs=in_specs,
        out_specs=[
            pl.BlockSpec((tm, n), lambda i, k, j: (i, 0)),
            pl.BlockSpec((tm, n), lambda i, k, j: (i, 0)),
        ],
        out_shape=[jax.ShapeDtypeStruct((m, n), F32), jax.ShapeDtypeStruct((m, n), BF16)],
        compiler_params=_params("arbitrary", "arbitrary", "arbitrary"),
        name="mm_ln",
    )(*args)


def _xattn_kernel(h_ref, wq_ref, kv_ref, wo_ref, g_ref, b_ref, wrh_ref, wrl_ref, br_ref,
                  of_ref, ob_ref, topw_ref, topi_ref, sel_ref, *, alpha, heads, hd):
    q = jnp.dot(h_ref[...].astype(BF16), wq_ref[...], preferred_element_type=F32)
    scale = hd ** -0.5
    outs = []
    for hh in range(heads):
        qh = (q[:, hh * hd:(hh + 1) * hd] * scale).astype(BF16)
        kh = kv_ref[:, hh * hd:(hh + 1) * hd]
        vh = kv_ref[:, (heads + hh) * hd:(heads + hh + 1) * hd]
        s = lax.dot_general(qh, kh, (((1,), (1,)), ((), ())), preferred_element_type=F32)
        s = s - jnp.max(s, axis=-1, keepdims=True)
        p = jnp.exp(s)
        l = jnp.sum(p, axis=-1, keepdims=True)
        o = jnp.dot(p.astype(BF16), vh, preferred_element_type=F32) / l
        outs.append(o.astype(BF16))
    o_all = jnp.concatenate(outs, axis=-1)
    of_ref[...] = alpha * h_ref[...] + jnp.dot(o_all, wo_ref[...], preferred_element_type=F32)
    _ln_rows(of_ref, g_ref, b_ref, of_ref, ob_ref, 64, packed=True)
    topw_ref[...], topi_ref[...], sel_ref[...] = _route_top4(of_ref[...], wrh_ref[...], wrl_ref[...], br_ref[...])


def xattn_ln(h, wq_b, kv_b, wo_b, g, b, w_router, b_router, alpha, *, seq, tm=512):
    m, d = h.shape
    lane_spec = pl.BlockSpec((tm, LANES), lambda i: (i, 0))
    xw = wq_b.shape[1]
    n_mem = kv_b.shape[0] // (m // seq)
    per_b = seq // tm

    def const_spec(shape):
        return pl.BlockSpec(shape, lambda i: (0, 0), pipeline_mode=pl.Buffered(1))

    return pl.pallas_call(
        functools.partial(_xattn_kernel, alpha=alpha, heads=XA_HEADS, hd=xw // XA_HEADS),
        grid=(m // tm,),
        in_specs=[
            pl.BlockSpec((tm, d), lambda i: (i, 0)),
            const_spec((d, xw)),
            pl.BlockSpec((n_mem, 2 * xw), lambda i: (i // per_b, 0)),
            const_spec((xw, d)),
            const_spec((1, d)),
            const_spec((1, d)),
            const_spec((d, LANES)),
            const_spec((d, LANES)),
            const_spec((1, LANES)),
        ],
        out_specs=[pl.BlockSpec((tm, d), lambda i: (i, 0)), pl.BlockSpec((tm, d // 2), lambda i: (i, 0)),
                   lane_spec, lane_spec, lane_spec],
        out_shape=[jax.ShapeDtypeStruct((m, d), F32), jax.ShapeDtypeStruct((m, d // 2), jnp.uint32),
                   jax.ShapeDtypeStruct((m, LANES), F32), jax.ShapeDtypeStruct((m, LANES), jnp.int32),
                   jax.ShapeDtypeStruct((m, LANES), F32)],
        compiler_params=_params("arbitrary"),
        name="xattn_ln",
    )(h, wq_b, kv_b, wo_b, g, b, *_split_bf16(w_router), b_router)


def _route_top4(h, w_hi, w_lo, b):
    h_hi, h_lo = _split_bf16(h)
    both = jnp.dot(h_hi, jnp.concatenate([w_hi, w_lo], axis=1), preferred_element_type=F32)
    logits = both[:, :LANES] + both[:, LANES:] + jnp.dot(h_lo, w_hi, preferred_element_type=F32) + b
    lane = lax.broadcasted_iota(jnp.int32, logits.shape, 1)
    neg = jnp.float32(-jnp.inf)
    masked = jnp.where(lane < N_EXPERTS, logits, neg)
    top_vals, top_idx = [], []
    sel = jnp.zeros_like(logits)
    for _ in range(TOP_K):
        mval = jnp.max(masked, axis=-1, keepdims=True)
        idx = jnp.min(jnp.where(masked == mval, lane, LANES), axis=-1, keepdims=True)
        hot = lane == idx
        top_vals.append(mval)
        top_idx.append(idx)
        sel = jnp.where(hot, 1.0, sel)
        masked = jnp.where(hot, neg, masked)
    exps = [jnp.exp(v - top_vals[0]) for v in top_vals]
    denom = exps[0]
    for e in exps[1:]:
        denom = denom + e
    topw = jnp.zeros_like(logits)
    topi = jnp.zeros(logits.shape, jnp.int32)
    for k in range(TOP_K):
        topw = jnp.where(lane == k, exps[k] / denom, topw)
        topi = jnp.where(lane == k, top_idx[k], topi)
    return topw, topi, sel


MOE_TM = 256


def moe_route_plan(topi, sel, *, tm):
    t = topi.shape[0]
    n_tiles = t * TOP_K // tm + N_EXPERTS
    order = jnp.argsort(topi[:, :TOP_K].reshape(-1), stable=True).astype(jnp.int32)
    counts = jnp.sum(sel[:, :N_EXPERTS], axis=0).astype(jnp.int32)
    first_pair = jnp.cumsum(counts) - counts
    tiles_e = (counts + tm - 1) // tm
    tile_end = jnp.cumsum(tiles_e)
    tile_start = tile_end - tiles_e
    tile_ids = jnp.arange(n_tiles, dtype=jnp.int32)
    tile_expert = jnp.minimum(jnp.sum(tile_end[None, :] <= tile_ids[:, None], axis=1), N_EXPERTS - 1).astype(jnp.int32)
    per_tile = jnp.stack([tile_start, counts, first_pair], axis=1)[tile_expert]
    local = (tile_ids - per_tile[:, 0])[:, None] * tm + jnp.arange(tm, dtype=jnp.int32)[None, :]
    valid = (tile_ids < tile_end[-1])[:, None] & (local < per_tile[:, 1:2])
    pair = order[jnp.clip(per_tile[:, 2:3] + local, 0, TOP_K * t - 1)]
    pad_row = TOP_K * t + (tile_ids % 2)[:, None] * tm + jnp.arange(tm, dtype=jnp.int32)[None, :]
    ydst = jnp.where(valid, (pair % TOP_K) * t + pair // TOP_K, pad_row)
    tok = ydst % t
    n_valid = tile_end[-1:].astype(jnp.int32)
    return (tok.reshape(n_tiles, 1, tm), ydst.reshape(n_tiles, 1, tm), tile_expert, n_valid)


def _moe_ffn_kernel(te_ref, nv_ref, tok0_ref, tok1_ref, tok2_ref, ydst_ref, h_hbm, wg_ref, wu_ref, wd_ref,
                    bg_ref, bu_ref, bd_ref, y_hbm, xg, og, wgu_b, wd_b, gsem, ssem, *, tm, ff):
    r = pl.program_id(0)
    nv = nv_ref[0]
    slot = lax.rem(r, 2)
    gslot = lax.rem(r, 3)

    def row_gather(idx_ref, s):
        for i in range(tm):
            pltpu.make_async_copy(h_hbm.at[pl.ds(idx_ref[0, i], 1), :], xg.at[s, pl.ds(i, 1), :], gsem.at[s]).start()

    def gather_wait(s):
        pltpu.make_async_copy(h_hbm.at[pl.ds(0, tm), :], xg.at[s], gsem.at[s]).wait()

    def scatter_wait(s):
        pltpu.make_async_copy(og.at[s], y_hbm.at[pl.ds(0, tm), :], ssem.at[s]).wait()

    @pl.when(r == 0)
    def _():
        row_gather(tok0_ref, 0)
        row_gather(tok1_ref, 1)
        og[1] = jnp.zeros(og.shape[1:], og.dtype)
        base = y_hbm.shape[0] - 2 * tm
        for part in range(2):
            fill = pltpu.make_async_copy(og.at[1], y_hbm.at[pl.ds(base + part * tm, tm), :], ssem.at[1])
            fill.start()
            fill.wait()

    @pl.when(r < nv)
    def _():
        gather_wait(gslot)

        @pl.when(r >= 2)
        def _():
            scatter_wait(slot)

        @pl.when((r == 0) | (te_ref[r] != te_ref[jnp.maximum(r - 1, 0)]))
        def _():
            wgu_b[:, :ff] = wg_ref[...].astype(BF16)
            wgu_b[:, ff:] = wu_ref[...].astype(BF16)
            wd_b[...] = wd_ref[...].astype(BF16)

        x_hi, x_lo = _unpack_bf16_pair(xg[gslot])
        x = jnp.concatenate([x_hi.astype(BF16), x_lo.astype(BF16)], axis=1)
        gu = jnp.dot(x, wgu_b[...], preferred_element_type=F32)
        g = jnp.minimum(gu[:, :ff] + bg_ref[...], SWIGLU_LIMIT)
        u = jnp.clip(gu[:, ff:] + bu_ref[...], -SWIGLU_LIMIT, SWIGLU_LIMIT)
        act = (g * jax.nn.sigmoid(SWIGLU_ALPHA * g) * (u + 1.0)).astype(BF16)
        res = _pack_bf16_pair(jnp.dot(act, wd_b[...], preferred_element_type=F32) + bd_ref[...])
        row_gather(tok2_ref, lax.rem(r + 2, 3))
        og[slot] = res
        for i in range(tm):
            pltpu.make_async_copy(og.at[slot, pl.ds(i, 1), :], y_hbm.at[pl.ds(ydst_ref[0, i], 1), :],
                                  ssem.at[slot]).start()

        @pl.when(r == nv - 1)
        def _():
            gather_wait(lax.rem(r + 1, 3))
            gather_wait(lax.rem(r + 2, 3))
            scatter_wait(slot)

            @pl.when(r >= 1)
            def _():
                scatter_wait(1 - slot)


def moe_experts(hp, plan, w_gate, b_gate, w_up, b_up, w_down, b_down, layer, *, tm):
    tok, ydst, tile_expert, n_valid = plan
    t = hp.shape[0]
    d = 2 * hp.shape[1]
    n_tiles = tok.shape[0]
    n_e, ff = w_gate.shape[1], w_gate.shape[3]
    bg = b_gate.reshape(b_gate.shape[0], n_e, 1, ff)
    bu = b_up.reshape(b_up.shape[0], n_e, 1, ff)
    bd = b_down.reshape(b_down.shape[0], n_e, 1, d)

    def expert(r, te, nv):
        return te[jnp.minimum(r, nv[0] - 1)]

    smem_blk = functools.partial(pl.BlockSpec, (None, 1, tm), memory_space=pltpu.SMEM)
    grid_spec = pltpu.PrefetchScalarGridSpec(
        num_scalar_prefetch=2,
        grid=(n_tiles,),
        in_specs=[
            smem_blk(lambda r, te, nv: (jnp.minimum(r, nv[0] - 1), 0, 0)),
            smem_blk(lambda r, te, nv: (jnp.minimum(r + 1, nv[0] - 1), 0, 0)),
            smem_blk(lambda r, te, nv: (jnp.minimum(r + 2, nv[0] - 1), 0, 0)),
            smem_blk(lambda r, te, nv: (r, 0, 0)),
            pl.BlockSpec(memory_space=pl.ANY),
            pl.BlockSpec((None, None, d, ff), lambda r, te, nv: (layer, expert(r, te, nv), 0, 0)),
            pl.BlockSpec((None, None, d, ff), lambda r, te, nv: (layer, expert(r, te, nv), 0, 0)),
            pl.BlockSpec((None, None, ff, d), lambda r, te, nv: (layer, expert(r, te, nv), 0, 0)),
            pl.BlockSpec((None, None, 1, ff), lambda r, te, nv: (layer, expert(r, te, nv), 0, 0)),
            pl.BlockSpec((None, None, 1, ff), lambda r, te, nv: (layer, expert(r, te, nv), 0, 0)),
            pl.BlockSpec((None, None, 1, d), lambda r, te, nv: (layer, expert(r, te, nv), 0, 0)),
        ],
        out_specs=pl.BlockSpec(memory_space=pl.ANY),
        scratch_shapes=[
            pltpu.VMEM((3, tm, d // 2), jnp.uint32),
            pltpu.VMEM((2, tm, d // 2), jnp.uint32),
            pltpu.VMEM((d, 2 * ff), BF16),
            pltpu.VMEM((ff, d), BF16),
            pltpu.SemaphoreType.DMA((3,)),
            pltpu.SemaphoreType.DMA((2,)),
        ],
    )
    return pl.pallas_call(
        functools.partial(_moe_ffn_kernel, tm=tm, ff=ff),
        grid_spec=grid_spec,
        out_shape=jax.ShapeDtypeStruct((TOP_K * t + 2 * tm, d // 2), jnp.uint32),
        compiler_params=_params("arbitrary"),
        name="moe_experts",
    )(tile_expert, n_valid, tok, tok, tok, ydst, hp, w_gate, w_up, w_down, bg, bu, bd)


def _moe_combine_kernel(y0_ref, y1_ref, y2_ref, y3_ref, w_ref, h_ref, g_ref, b_ref, of_ref, ob_ref, *, alpha, rows):
    tm = h_ref.shape[0]
    y_refs = (y0_ref, y1_ref, y2_ref, y3_ref)

    def body(rr, carry):
        sl = pl.ds(pl.multiple_of(rr * rows, rows), rows)
        w = w_ref[sl, :]
        hrow = h_ref[sl, :]
        half = hrow.shape[1] // 2
        z_hi = alpha * hrow[:, :half]
        z_lo = alpha * hrow[:, half:]
        for k, y_ref in enumerate(y_refs):
            y_hi, y_lo = _unpack_bf16_pair(y_ref[sl, :])
            z_hi = z_hi + w[:, k:k + 1] * y_hi
            z_lo = z_lo + w[:, k:k + 1] * y_lo
        z = jnp.concatenate([z_hi, z_lo], axis=1)
        mu = jnp.mean(z, axis=-1, keepdims=True)
        zc = z - mu
        var = jnp.mean(zc * zc, axis=-1, keepdims=True)
        y = zc * lax.rsqrt(var + LN_EPS) * g_ref[...] + b_ref[...]
        of_ref[sl, :] = y
        ob_ref[sl, :] = y.astype(BF16)
        return carry

    lax.fori_loop(0, tm // rows, body, 0)


def moe_combine_ln(y, topw, h, g, b, alpha, *, tm=256):
    t, d = h.shape
    nb = t // tm

    def y_spec(k):
        return pl.BlockSpec((tm, d // 2), lambda i: (k * nb + i, 0))

    row_spec = pl.BlockSpec((tm, d), lambda i: (i, 0))
    return pl.pallas_call(
        functools.partial(_moe_combine_kernel, alpha=alpha, rows=32),
        grid=(nb,),
        in_specs=[y_spec(0), y_spec(1), y_spec(2), y_spec(3),
                  pl.BlockSpec((tm, LANES), lambda i: (i, 0)), row_spec,
                  pl.BlockSpec((1, d), lambda i: (0, 0)), pl.BlockSpec((1, d), lambda i: (0, 0))],
        out_specs=[row_spec, row_spec],
        out_shape=[jax.ShapeDtypeStruct((t, d), F32), jax.ShapeDtypeStruct((t, d), BF16)],
        compiler_params=_params("arbitrary"),
        name="moe_combine_ln",
    )(y, y, y, y, topw, h, g, b)


NEG = -1e30
LOG2E = 1.4426950408889634
MIX_CHUNK = 256
BAND_BLOCK = 256
SB_TQ = 512
SB_TK = 256

def _head_norm_rows(x, gain):
    mu = jnp.mean(x, axis=-1, keepdims=True)
    xc = x - mu
    var = jnp.mean(xc * xc, axis=-1, keepdims=True)
    return xc * lax.rsqrt(var + HN_EPS) * gain


def _log_sigmoid(x):
    return jnp.minimum(x, 0.0) - jnp.log(1.0 + jnp.exp(-jnp.abs(x)))


def _split_bf16(x):
    hi = x.astype(BF16)
    lo = (x - hi.astype(F32)).astype(BF16)
    return hi, lo


def _lane_select(x, lane_idx):
    lane = lax.broadcasted_iota(jnp.int32, x.shape, 1)
    col = jnp.sum(jnp.where(lane == lane_idx, x, 0.0), axis=-1, keepdims=True)
    return jnp.broadcast_to(col, x.shape)


def _retention_body(q_ref, k_ref, v_ref, g_ref, cos_ref, sin_ref, lg_ref, gain_ref, o_ref,
                    qs_ref, ks_ref, st_ref, *, scale):
    seq = q_ref.shape[0]
    L = MIX_CHUNK
    half = HEAD_DIM // 2
    cos = cos_ref[...]
    sin = sin_ref[...]
    q = q_ref[...].astype(F32)
    k = k_ref[...].astype(F32)
    qs_ref[...] = (q * cos + pltpu.roll(q, half, 1) * sin).astype(BF16)
    ks_ref[...] = ((k * cos + pltpu.roll(k, half, 1) * sin) * scale).astype(BF16)

    lg = lg_ref[...]
    ri = lax.broadcasted_iota(jnp.int32, (L, L), 0)
    ci = lax.broadcasted_iota(jnp.int32, (L, L), 1)
    intra = jnp.where(ri >= ci, jnp.exp(lg * jnp.maximum(ri - ci, 0).astype(F32)), 0.0)
    rr = lax.broadcasted_iota(jnp.int32, (L, HEAD_DIM), 0).astype(F32)
    lg_d = lg[:, :HEAD_DIM]
    q_dec = jnp.exp(lg_d * (rr + 1.0))
    k_dec = jnp.exp(lg_d * (L - 1.0 - rr))
    c_dec = jnp.exp(lg_d * float(L))
    gain = gain_ref[...]
    st_ref[...] = jnp.zeros_like(st_ref)

    def chunk(c, carry):
        sl = pl.ds(pl.multiple_of(c * L, L), L)
        qc = qs_ref[sl, :]
        kc = ks_ref[sl, :]
        vc = v_ref[sl, :]
        state = st_ref[:, :HEAD_DIM]
        att = lax.dot_general(qc, kc, _NT, preferred_element_type=F32) * intra
        o = (jnp.dot(att.astype(BF16), vc, preferred_element_type=F32)
             + jnp.dot((qc.astype(F32) * q_dec).astype(BF16), state.astype(BF16), preferred_element_type=F32))
        st_ref[:, :HEAD_DIM] = state * c_dec + lax.dot_general(
            (kc.astype(F32) * k_dec).astype(BF16), vc, _TN, preferred_element_type=F32)
        gv = g_ref[sl, :].astype(F32)
        o_ref[sl, :] = (_head_norm_rows(o, gain) * (gv * jax.nn.sigmoid(gv))).astype(o_ref.dtype)
        return carry

    lax.fori_loop(0, seq // L, chunk, 0, unroll=True)


def _mlstm_body(u_ref, v_ref, og_ref, gates_ref, cw_ref, cb_ref, wq_ref, wk_ref, gain_ref, o_ref,
                qs_ref, ks_ref, st_ref, m_ref, *, head, heads, scale):
    seq = u_ref.shape[0]
    L = MIX_CHUNK
    d = HEAD_DIM
    x = u_ref[...].astype(F32)
    row = lax.broadcasted_iota(jnp.int32, x.shape, 0)
    cw = cw_ref[...]
    y = x * cw[CONV_K - 1:CONV_K, :] + cb_ref[...]
    for sh in range(1, CONV_K):
        xs = jnp.where(row >= sh, pltpu.roll(x, sh, 0), 0.0)
        y = y + xs * cw[CONV_K - 1 - sh:CONV_K - sh, :]
    ub = (y * jax.nn.sigmoid(y)).astype(BF16)
    qs_ref[...] = jnp.dot(ub, wq_ref[...].astype(BF16), preferred_element_type=F32).astype(BF16)
    ks_ref[...] = (jnp.dot(ub, wk_ref[...].astype(BF16), preferred_element_type=F32) * scale).astype(BF16)

    ri = lax.broadcasted_iota(jnp.int32, (L, L), 0)
    ci = lax.broadcasted_iota(jnp.int32, (L, L), 1)
    causal = ri >= ci
    tri = jnp.where(causal, 1.0, 0.0).astype(BF16)
    ones_v = jnp.ones((L, d), BF16)
    gain = gain_ref[...]
    st_ref[...] = jnp.zeros_like(st_ref)
    m_ref[...] = jnp.zeros_like(m_ref)

    def chunk(c, carry):
        sl = pl.ds(pl.multiple_of(c * L, L), L)
        qc = qs_ref[sl, :]
        kc = ks_ref[sl, :]
        v_ext = jnp.concatenate([v_ref[sl, :], ones_v], axis=1)
        gts = gates_ref[sl, :]
        ic = _lane_select(gts, head)
        lf = _log_sigmoid(_lane_select(gts, heads + head))
        lf_hi, lf_lo = _split_bf16(lf)
        bcum = (jnp.dot(tri, lf_hi, preferred_element_type=F32)
                + jnp.dot(tri, lf_lo, preferred_element_type=F32))
        m_st = m_ref[...]
        src = jnp.transpose(ic - bcum)[:1, :]
        bcum2 = jnp.concatenate([bcum, bcum], axis=1)
        log_intra = jnp.where(causal, bcum2 + src, NEG)
        m_intra = jnp.max(log_intra, axis=-1, keepdims=True)
        log_cross = bcum + m_st
        m_row = jnp.maximum(log_cross, m_intra)
        m_row2 = jnp.concatenate([m_row, m_row], axis=1)
        w_intra = jnp.exp(log_intra - m_row2)
        w_cross = jnp.exp(log_cross - m_row)
        w_cross2 = jnp.concatenate([w_cross, w_cross], axis=1)
        qk = lax.dot_general(qc, kc, _NT, preferred_element_type=F32) * w_intra
        state = st_ref[...]
        res = (jnp.dot(qk.astype(BF16), v_ext, preferred_element_type=F32)
               + w_cross2 * jnp.dot(qc, state.astype(BF16), preferred_element_type=F32))
        num = res[:, :d]
        den = res[:, d:]
        hh = num / jnp.maximum(jnp.abs(den), jnp.exp(-m_row))
        og = og_ref[sl, :].astype(F32)
        o_ref[sl, :] = _head_norm_rows(hh * jax.nn.sigmoid(og), gain).astype(o_ref.dtype)
        b_last = bcum[L - 1:L, :]
        log_state = b_last - bcum + ic
        m_new = jnp.maximum(b_last + m_st, jnp.max(log_state, axis=0, keepdims=True))
        decay = jnp.exp(b_last + m_st - m_new)
        kw = (kc.astype(F32) * jnp.exp(log_state - m_new)).astype(BF16)
        decay2 = jnp.concatenate([decay, decay], axis=1)
        st_ref[...] = decay2 * state + lax.dot_general(kw, v_ext, _TN, preferred_element_type=F32)
        m_ref[...] = m_new
        return carry

    lax.fori_loop(0, seq // L, chunk, 0, unroll=True)


def _mixer_ab_kernel(a0_ref, a1_ref, a2_ref, a3_ref, gates_ref, cos_ref, sin_ref, lg_ref, cw_ref, cb_ref,
                     wq_ref, wk_ref, gain_ref, o_ref, qs_ref, ks_ref, st_ref, m_ref, *, heads, scale):
    g = pl.program_id(1)

    @pl.when(g < heads)
    def _():
        _retention_body(a0_ref, a1_ref, a2_ref, a3_ref, cos_ref, sin_ref, lg_ref, gain_ref, o_ref,
                        qs_ref, ks_ref, st_ref, scale=scale)

    @pl.when(g >= heads)
    def _():
        _mlstm_body(a0_ref, a1_ref, a2_ref, gates_ref, cw_ref, cb_ref, wq_ref, wk_ref, gain_ref, o_ref,
                    qs_ref, ks_ref, st_ref, m_ref, head=g - heads, heads=heads, scale=scale)


def mixer_ab(z, gates, cos2, sin2, log_g, conv_w, conv_b, wq_m, wk_m, gain, layer, *, bsz, seq, heads):
    t = z.shape[0]
    d = HEAD_DIM
    L = MIX_CHUNK

    def col(base_ret, base_ml):
        def index(b, g):
            is_ml = g // heads
            return (b, (1 - is_ml) * (base_ret * heads + g) + is_ml * (base_ml * heads + g - heads))
        return pl.BlockSpec((seq, d), index)

    def ml_head(g):
        return jnp.maximum(g - heads, 0)

    conv_w4 = conv_w.reshape(conv_w.shape[0], CONV_K, heads, d).transpose(0, 2, 1, 3)
    conv_b4 = conv_b.reshape(conv_b.shape[0], heads, 1, d)
    gain4 = gain.reshape(2 * heads, 1, d)
    return pl.pallas_call(
        functools.partial(_mixer_ab_kernel, heads=heads, scale=d ** -0.5),
        grid=(bsz, 2 * heads),
        in_specs=[
            col(0, 4), col(1, 5), col(2, 6), col(3, 6),
            pl.BlockSpec((seq, LANES), lambda b, g: (b, 0)),
            pl.BlockSpec((seq, d), lambda b, g: (0, 0)),
            pl.BlockSpec((seq, d), lambda b, g: (0, 0)),
            pl.BlockSpec((None, 1, L), lambda b, g: (jnp.minimum(g, heads - 1), 0, 0)),
            pl.BlockSpec((None, None, CONV_K, d), lambda b, g: (layer, ml_head(g), 0, 0)),
            pl.BlockSpec((None, None, 1, d), lambda b, g: (layer, ml_head(g), 0, 0)),
            pl.BlockSpec((None, None, d, d), lambda b, g: (layer, ml_head(g), 0, 0)),
            pl.BlockSpec((None, None, d, d), lambda b, g: (layer, ml_head(g), 0, 0)),
            pl.BlockSpec((None, 1, d), lambda b, g: (g, 0, 0)),
        ],
        out_specs=pl.BlockSpec((seq, d), lambda b, g: (b, g)),
        out_shape=jax.ShapeDtypeStruct((t, 2 * heads * d), BF16),
        scratch_shapes=[
            pltpu.VMEM((seq, d), BF16),
            pltpu.VMEM((seq, d), BF16),
            pltpu.VMEM((d, 2 * d), F32),
            pltpu.VMEM((1, d), F32),
        ],
        compiler_params=_params("arbitrary", "arbitrary"),
        name="mixer_ab",
    )(z, z, z, z, gates, cos2, sin2, log_g, conv_w4, conv_b4, wq_m, wk_m, gain4)


def _band_body(q_ref, k_ref, v_ref, bias_ref, o_ref, *, scale):
    seq = q_ref.shape[0]
    bq = BAND_BLOCK
    n_back = bias_ref.shape[0]

    def block(i, carry):
        sl = pl.ds(pl.multiple_of(i * bq, bq), bq)
        q = q_ref[sl, :]
        scores, vals = [], []
        for dlt in range(n_back):
            ks = pl.ds(pl.multiple_of(jnp.maximum(i - dlt, 0) * bq, bq), bq)
            s = lax.dot_general(q, k_ref[ks, :], _NT, preferred_element_type=F32) * (scale * LOG2E) + bias_ref[dlt]
            scores.append(jnp.where(i - dlt >= 0, s, NEG))
            vals.append(v_ref[ks, :])
        m = jnp.max(scores[0], axis=-1, keepdims=True)
        for s in scores[1:]:
            m = jnp.maximum(m, jnp.max(s, axis=-1, keepdims=True))
        acc = jnp.zeros((bq, HEAD_DIM), F32)
        l = jnp.zeros((bq, 1), F32)
        for s, vv in zip(scores, vals):
            p = jnp.exp2(s - m)
            l = l + jnp.sum(p, axis=-1, keepdims=True)
            acc = acc + jnp.dot(p.astype(BF16), vv, preferred_element_type=F32)
        o_ref[sl, :] = (acc / l).astype(o_ref.dtype)
        return carry

    lax.fori_loop(0, seq // bq, block, 0, unroll=True)


def _stick_breaking_body(q_ref, k_ref, v_ref, o_ref, acc_ref, r_ref, *, scale):
    seq = q_ref.shape[0]
    tq, tk = SB_TQ, SB_TK
    n_sub = tq // tk
    ri = lax.broadcasted_iota(jnp.int32, (tk, tk), 0)
    ci = lax.broadcasted_iota(jnp.int32, (tk, tk), 1)
    suffix = jnp.where(ri > ci, 1.0, 0.0).astype(BF16)
    strict = (lax.broadcasted_iota(jnp.int32, (tq, tq), 1)
              < lax.broadcasted_iota(jnp.int32, (tq, tq), 0))

    def one_group(q, k0, masked):
        kb = k_ref[pl.ds(k0, tq), :]
        vb = v_ref[pl.ds(k0, tq), :]
        z = lax.dot_general(q, kb, _NT, preferred_element_type=F32) * (scale * LOG2E)
        sp = jnp.maximum(z, 0.0) + jnp.log2(1.0 + jnp.exp2(-jnp.abs(z)))
        ls_pos = z - sp
        if masked:
            sp = jnp.where(strict, sp, 0.0)
        sp_b = sp.astype(BF16)
        r = r_ref[...]
        pieces = []
        for s in reversed(range(n_sub)):
            sl = slice(s * tk, (s + 1) * tk)
            between = jnp.dot(sp_b[:, sl], suffix, preferred_element_type=F32)
            pieces.append(ls_pos[:, sl] - between - jnp.concatenate([r] * (tk // LANES), axis=1))
            r = r + jnp.sum(sp[:, sl], axis=-1, keepdims=True)
        p = jnp.exp2(jnp.concatenate(pieces[::-1], axis=1))
        if masked:
            p = jnp.where(strict, p, 0.0)
        acc_ref[...] += jnp.dot(p.astype(BF16), vb, preferred_element_type=F32)
        r_ref[...] = r

    for i in range(seq // tq):
        q0 = i * tq
        q = q_ref[pl.ds(q0, tq), :]
        acc_ref[...] = jnp.zeros_like(acc_ref)
        r_ref[...] = jnp.zeros_like(r_ref)
        one_group(q, q0, True)

        def below(jj, c, q=q, i=i):
            one_group(q, pl.multiple_of((i - 1 - jj) * tq, tq), False)
            return c

        lax.fori_loop(0, i, below, 0, unroll=True)
        o_ref[pl.ds(q0, tq), :] = acc_ref[...].astype(o_ref.dtype)


def _mixer_cd_kernel(q_ref, k_ref, v_ref, bias_ref, o_ref, acc_ref, r_ref, *, heads, scale):
    g = pl.program_id(1)

    @pl.when(g < heads)
    def _():
        _band_body(q_ref, k_ref, v_ref, bias_ref, o_ref, scale=scale)

    @pl.when(g >= heads)
    def _():
        _stick_breaking_body(q_ref, k_ref, v_ref, o_ref, acc_ref, r_ref, scale=scale)


def _band_bias_kernel(rrow_ref, o_ref):
    bq = BAND_BLOCK
    shift = CHUNK.bit_length() - 1
    qo = lax.broadcasted_iota(jnp.int32, (bq, bq), 0)
    ck = jnp.right_shift(lax.broadcasted_iota(jnp.int32, (bq, bq), 1), shift)
    for dlt in range(o_ref.shape[0]):
        x = jnp.broadcast_to(rrow_ref[dlt], (bq, 2 * bq))
        toep = pltpu.roll(x, 0, 1, stride=1, stride_axis=0)[:, :bq]
        cq = jnp.right_shift(qo + bq * dlt, shift)
        allowed = (ck <= cq) & (ck >= cq - PAST_CHUNKS)
        o_ref[dlt] = jnp.where(allowed, toep * LOG2E, NEG)


def band_bias_table(rel_bias):
    bq = BAND_BLOCK
    heads = rel_bias.shape[0]
    n_back = PAST_CHUNKS * CHUNK // bq + 1
    m = jnp.arange(2 * bq)
    key_minus_query = jnp.where(m < bq, m, m - 2 * bq)
    dist = bq * jnp.arange(n_back)[:, None] - key_minus_query[None, :]
    idx = jnp.clip(dist, -(CHUNK - 1), REL_MAX) + (CHUNK - 1)
    rrow = rel_bias.astype(F32)[:, idx].reshape(heads, n_back, 1, 2 * bq)
    return pl.pallas_call(
        _band_bias_kernel,
        grid=(heads,),
        in_specs=[pl.BlockSpec((None, n_back, 1, 2 * bq), lambda hh: (hh, 0, 0, 0))],
        out_specs=pl.BlockSpec((None, n_back, bq, bq), lambda hh: (hh, 0, 0, 0)),
        out_shape=jax.ShapeDtypeStruct((heads, n_back, bq, bq), F32),
        compiler_params=_params("arbitrary"),
        name="band_bias",
    )(rrow)


def mixer_cd(z, bias_tab, *, bsz, seq, heads):
    t = z.shape[0]
    d = HEAD_DIM

    def col(which):
        return pl.BlockSpec((seq, d), lambda b, g: (b, (3 * (g // heads) + which) * heads + g % heads))

    return pl.pallas_call(
        functools.partial(_mixer_cd_kernel, heads=heads, scale=d ** -0.5),
        grid=(bsz, 2 * heads),
        in_specs=[
            col(0), col(1), col(2),
            pl.BlockSpec((None,) + bias_tab.shape[1:], lambda b, g: (jnp.minimum(g, heads - 1), 0, 0, 0)),
        ],
        out_specs=pl.BlockSpec((seq, d), lambda b, g: (b, g)),
        out_shape=jax.ShapeDtypeStruct((t, 2 * heads * d), BF16),
        scratch_shapes=[pltpu.VMEM((SB_TQ, d), F32), pltpu.VMEM((SB_TQ, LANES), F32)],
        compiler_params=_params("arbitrary", "arbitrary"),
        name="mixer_cd",
    )(z, z, z, bias_tab)


def rope_tables(seq_len):
    pos = jnp.arange(seq_len, dtype=F32)
    inv_freq = ROPE_BASE ** (-jnp.arange(0, HEAD_DIM, 2, dtype=F32) / HEAD_DIM)
    ang = pos[:, None] * inv_freq[None, :]
    cos, sin = jnp.cos(ang), jnp.sin(ang)
    return jnp.concatenate([cos, cos], axis=1), jnp.concatenate([-sin, sin], axis=1)


def retention_log_decay(heads):
    lg = jnp.log1p(-jnp.exp2(-(5.0 + jnp.arange(heads, dtype=F32))))
    return jnp.broadcast_to(lg[:, None, None], (heads, 1, MIX_CHUNK))


def _pad_lanes(a, value=0.0):
    return jnp.pad(a, ((0, 0), (0, LANES - a.shape[1])), constant_values=value)


def kernel(x, mem, ab_w_in, ab_gate_b, ab_conv_w, ab_conv_b, ab_wq, ab_wk, ab_ret_norm_g, ab_mlstm_norm_g, ab_w_out, cd_w_in, cd_rel_bias, cd_w_out, mix_ln_g, mix_ln_b, xa_wq, xa_wkv, xa_wo, xa_ln_g, xa_ln_b, moe_router_w, moe_router_b, moe_w_gate, moe_b_gate, moe_w_up, moe_b_up, moe_w_down, moe_b_down, moe_ln_g, moe_ln_b):
    bsz, seq, d = x.shape
    depth = mix_ln_g.shape[0]
    heads = GROUP_HEADS
    alpha = (2.0 * depth) ** 0.25
    t = bsz * seq
    cos2, sin2 = rope_tables(seq)
    log_g = retention_log_decay(heads)
    h = x.reshape(t, d)
    hb = h.astype(BF16)
    memb = mem.reshape(-1, d).astype(BF16)
    n_ab = 7 * GROUP_WIDTH
    n_cd = 6 * GROUP_WIDTH
    ab_w_in_t = jnp.swapaxes(ab_w_in, 1, 2)
    ab_w_out_b = ab_w_out.astype(BF16)
    cd_w_out_b = cd_w_out.astype(BF16)
    for layer in range(depth):
        i = layer // 2
        if layer % 2 == 0:
            z = matmul_nt_stacked(hb, ab_w_in_t, i, n_ab, tm=1024, tn=512, out_dtype=BF16)
            gates = gates_nt(hb, ab_w_in_t, i, n_ab, 2 * heads, _pad_lanes(ab_gate_b[i][None, :]), tm=1024)
            gain = jnp.concatenate([ab_ret_norm_g[i], ab_mlstm_norm_g[i]])
            yb = mixer_ab(z, gates, cos2, sin2, log_g, ab_conv_w, ab_conv_b, ab_wq, ab_wk, gain, i,
                          bsz=bsz, seq=seq, heads=heads)
            w_out = ab_w_out_b
        else:
            z = matmul_stacked(hb, cd_w_in, i, n_cd, tm=1024, tn=512, out_dtype=BF16)
            yb = mixer_cd(z, band_bias_table(cd_rel_bias[i]), bsz=bsz, seq=seq, heads=heads)
            w_out = cd_w_out_b
        h, hb = matmul_ln(yb, w_out, i, h, mix_ln_g[layer][None, :], mix_ln_b[layer][None, :], alpha)

        kvb = matmul_stacked(memb, xa_wkv, layer, xa_wkv.shape[2], tm=memb.shape[0], tn=512, out_dtype=BF16)
        h, hp, topw, topi, sel = xattn_ln(
            h, xa_wq[layer].astype(BF16), kvb, xa_wo[layer].astype(BF16),
            xa_ln_g[layer][None, :], xa_ln_b[layer][None, :],
            _pad_lanes(moe_router_w[layer]), _pad_lanes(moe_router_b[layer][None, :]), alpha, seq=seq)
        plan = moe_route_plan(topi, sel, tm=MOE_TM)
        y = moe_experts(hp, plan, moe_w_gate, moe_b_gate, moe_w_up, moe_b_up, moe_w_down, moe_b_down, layer, tm=MOE_TM)
        h, hb = moe_combine_ln(y, topw, h, moe_ln_g[layer][None, :], moe_ln_b[layer][None, :], alpha)
    return h.reshape(bsz, seq, d)
```

```python
import functools

import jax
import jax.numpy as jnp
from jax import lax
from jax.experimental import pallas as pl
from jax.experimental.pallas import tpu as pltpu

F32 = jnp.float32
BF16 = jnp.bfloat16

CHUNK = 64
HEAD_DIM = 128
GROUP_HEADS = 16
GROUP_WIDTH = GROUP_HEADS * HEAD_DIM
CONV_K = 4
PAST_CHUNKS = 8
REL_MAX = 2 * CHUNK
ROPE_BASE = 10000.0
XA_HEADS = 4
N_EXPERTS = 32
TOP_K = 4
SWIGLU_LIMIT = 7.0
SWIGLU_ALPHA = 1.702
LN_EPS = 1e-5
HN_EPS = 1e-6

LANES = 128
VMEM_LIMIT_BYTES = 58 * 1024 * 1024


def _params(*sem):
    return pltpu.CompilerParams(dimension_semantics=sem, vmem_limit_bytes=VMEM_LIMIT_BYTES)


def _mm_kernel(x_ref, w_ref, o_ref, wb_ref):
    @pl.when(pl.program_id(1) == 0)
    def _():
        wb_ref[...] = w_ref[...].astype(BF16)

    o_ref[...] = jnp.dot(x_ref[...], wb_ref[...], preferred_element_type=F32).astype(o_ref.dtype)


def matmul_stacked(x, w, layer, n_cols, *, tm, tn, out_dtype):
    m, k = x.shape
    return pl.pallas_call(
        _mm_kernel,
        grid=(n_cols // tn, m // tm),
        in_specs=[
            pl.BlockSpec((tm, k), lambda j, i: (i, 0)),
            pl.BlockSpec((None, k, tn), lambda j, i: (layer, 0, j)),
        ],
        out_specs=pl.BlockSpec((tm, tn), lambda j, i: (i, j)),
        out_shape=jax.ShapeDtypeStruct((m, n_cols), out_dtype),
        scratch_shapes=[pltpu.VMEM((k, tn), BF16)],
        compiler_params=_params("arbitrary", "arbitrary"),
        name="mm_in",
    )(x, w)


_NT = (((1,), (1,)), ((), ()))
_TN = (((0,), (0,)), ((), ()))


def _mm_nt_kernel(x_ref, w_ref, o_ref, wb_ref):
    @pl.when(pl.program_id(1) == 0)
    def _():
        wb_ref[...] = w_ref[...].astype(BF16)

    o_ref[...] = lax.dot_general(x_ref[...], wb_ref[...], _NT, preferred_element_type=F32).astype(o_ref.dtype)


def matmul_all_layers(x, w, *, tm, tn, out_dtype):
    m, k = x.shape
    n_layers, _, n = w.shape
    per = n // tn
    return pl.pallas_call(
        _mm_kernel,
        grid=(n_layers * per, m // tm),
        in_specs=[
            pl.BlockSpec((tm, k), lambda j, i: (i, 0)),
            pl.BlockSpec((None, k, tn), lambda j, i: (j // per, 0, j % per)),
        ],
        out_specs=pl.BlockSpec((tm, tn), lambda j, i: (i, j)),
        out_shape=jax.ShapeDtypeStruct((m, n_layers * n), out_dtype),
        scratch_shapes=[pltpu.VMEM((k, tn), BF16)],
        compiler_params=_params("arbitrary", "arbitrary"),
        name="mm_kv",
    )(x, w)


def matmul_nt_stacked(x, wt, layer, n_cols, *, tm, tn, out_dtype):
    m, k = x.shape
    return pl.pallas_call(
        _mm_nt_kernel,
        grid=(n_cols // tn, m // tm),
        in_specs=[
            pl.BlockSpec((tm, k), lambda j, i: (i, 0)),
            pl.BlockSpec((None, tn, k), lambda j, i: (layer, j, 0)),
        ],
        out_specs=pl.BlockSpec((tm, tn), lambda j, i: (i, j)),
        out_shape=jax.ShapeDtypeStruct((m, n_cols), out_dtype),
        scratch_shapes=[pltpu.VMEM((tn, k), BF16)],
        compiler_params=_params("arbitrary", "arbitrary"),
        name="mm_in_nt",
    )(x, wt)


def _gates_kernel(x_ref, w_ref, b_ref, o_ref, wb_ref):
    @pl.when(pl.program_id(0) == 0)
    def _():
        wb_ref[...] = jnp.zeros_like(wb_ref)
        wb_ref[:w_ref.shape[0], :] = w_ref[...].astype(BF16)

    o_ref[...] = lax.dot_general(x_ref[...], wb_ref[...], _NT, preferred_element_type=F32) + b_ref[...]


def gates_nt(x, wt, layer, row0, n_rows, b_pad, *, tm):
    m, k = x.shape
    return pl.pallas_call(
        _gates_kernel,
        grid=(m // tm,),
        in_specs=[
            pl.BlockSpec((tm, k), lambda i: (i, 0)),
            pl.BlockSpec((None, n_rows, k), lambda i: (layer, row0 // n_rows, 0)),
            pl.BlockSpec((1, LANES), lambda i: (0, 0)),
        ],
        out_specs=pl.BlockSpec((tm, LANES), lambda i: (i, 0)),
        out_shape=jax.ShapeDtypeStruct((m, LANES), F32),
        scratch_shapes=[pltpu.VMEM((LANES, k), BF16)],
        compiler_params=_params("arbitrary"),
        name="gates",
    )(x, wt, b_pad)


def _pack_bf16_pair(y):
    half = y.shape[1] // 2
    hi = pltpu.bitcast(y[:, :half].astype(BF16).astype(F32), jnp.uint32)
    lo = pltpu.bitcast(y[:, half:].astype(BF16).astype(F32), jnp.uint32)
    return hi | (lo >> 16)


def _unpack_bf16_pair(w):
    return (pltpu.bitcast(w & jnp.uint32(0xFFFF0000), F32), pltpu.bitcast(w << 16, F32))


def _ln_rows(z_ref, g_ref, b_ref, of_ref, ob_ref, rows, packed=False):
    tm = z_ref.shape[0]

    def body(r, carry):
        sl = pl.ds(pl.multiple_of(r * rows, rows), rows)
        z = z_ref[sl, :]
        mu = jnp.mean(z, axis=-1, keepdims=True)
        zc = z - mu
        var = jnp.mean(zc * zc, axis=-1, keepdims=True)
        y = zc * lax.rsqrt(var + LN_EPS) * g_ref[...] + b_ref[...]
        of_ref[sl, :] = y
        ob_ref[sl, :] = _pack_bf16_pair(y) if packed else y.astype(BF16)
        return carry

    lax.fori_loop(0, tm // rows, body, 0, unroll=2)


def _mm_ln_kernel(x_ref, w_ref, h_ref, g_ref, b_ref, of_ref, ob_ref, *, nk, nj, tn, alpha):
    k = pl.program_id(1)
    j = pl.program_id(2)
    part = jnp.dot(x_ref[...], w_ref[...].astype(BF16), preferred_element_type=F32)
    for jj in range(nj):
        sl = slice(jj * tn, (jj + 1) * tn)

        @pl.when((j == jj) & (k == 0))
        def _():
            of_ref[:, sl] = alpha * h_ref[...] + part

        @pl.when((j == jj) & (k > 0))
        def _():
            of_ref[:, sl] += part

    @pl.when((k == nk - 1) & (j == nj - 1))
    def _():
        _ln_rows(of_ref, g_ref, b_ref, of_ref, ob_ref, 64)


def matmul_ln(x, w, layer, h, g, b, alpha, *, tm=512, tn=1024, tk=4096):
    m, kdim = x.shape
    n = h.shape[1]
    tk = min(tk, kdim)
    nk, nj = kdim // tk, n // tn
    in_specs = [
        pl.BlockSpec((tm, tk), lambda i, k, j: (i, k)),
        pl.BlockSpec((None, tk, tn), lambda i, k, j: (layer, k, j)),
        pl.BlockSpec((tm, tn), lambda i, k, j: (i, j)),
        pl.BlockSpec((1, n), lambda i, k, j: (0, 0)),
        pl.BlockSpec((1, n), lambda i, k, j: (0, 0)),
    ]
    args = [x, w, h, g, b]
    return pl.pallas_call(
        functools.partial(_mm_ln_kernel, nk=nk, nj=nj, tn=tn, alpha=alpha),
        grid=(m // tm, nk, nj),
        in_specs=in_specs,
        out_specs=[
            pl.BlockSpec((tm, n), lambda i, k, j: (i, 0)),
            pl.BlockSpec((tm, n), lambda i, k, j: (i, 0)),
        ],
        out_shape=[jax.ShapeDtypeStruct((m, n), F32), jax.ShapeDtypeStruct((m, n), BF16)],
        compiler_params=_params("arbitrary", "arbitrary", "arbitrary"),
        name="mm_ln",
    )(*args)


def _xattn_kernel(h_ref, wq_ref, kv_ref, wo_ref, g_ref, b_ref, wrh_ref, wrl_ref, br_ref,
                  of_ref, ob_ref, topw_ref, topi_ref, sel_ref, *, alpha, heads, hd):
    q = jnp.dot(h_ref[...].astype(BF16), wq_ref[...], preferred_element_type=F32)
    scale = hd ** -0.5
    outs = []
    for hh in range(heads):
        qh = (q[:, hh * hd:(hh + 1) * hd] * scale).astype(BF16)
        kh = kv_ref[:, hh * hd:(hh + 1) * hd]
        vh = kv_ref[:, (heads + hh) * hd:(heads + hh + 1) * hd]
        s = lax.dot_general(qh, kh, (((1,), (1,)), ((), ())), preferred_element_type=F32)
        s = s - jnp.max(s, axis=-1, keepdims=True)
        p = jnp.exp(s)
        l = jnp.sum(p, axis=-1, keepdims=True)
        o = jnp.dot(p.astype(BF16), vh, preferred_element_type=F32) / l
        outs.append(o.astype(BF16))
    o_all = jnp.concatenate(outs, axis=-1)
    of_ref[...] = alpha * h_ref[...] + jnp.dot(o_all, wo_ref[...], preferred_element_type=F32)
    _ln_rows(of_ref, g_ref, b_ref, of_ref, ob_ref, 64, packed=True)
    topw_ref[...], topi_ref[...], sel_ref[...] = _route_top4(of_ref[...], wrh_ref[...], wrl_ref[...], br_ref[...])


def xattn_ln(h, wq_b, kv_b, kv_layer, wo_b, g, b, w_router, b_router, alpha, *, seq, tm=512):
    m, d = h.shape
    lane_spec = pl.BlockSpec((tm, LANES), lambda i: (i, 0))
    xw = wq_b.shape[1]
    n_mem = kv_b.shape[0] // (m // seq)
    per_b = seq // tm

    def const_spec(shape):
        return pl.BlockSpec(shape, lambda i: (0, 0), pipeline_mode=pl.Buffered(1))

    return pl.pallas_call(
        functools.partial(_xattn_kernel, alpha=alpha, heads=XA_HEADS, hd=xw // XA_HEADS),
        grid=(m // tm,),
        in_specs=[
            pl.BlockSpec((tm, d), lambda i: (i, 0)),
            const_spec((d, xw)),
            pl.BlockSpec((n_mem, 2 * xw), lambda i: (i // per_b, kv_layer)),
            const_spec((xw, d)),
            const_spec((1, d)),
            const_spec((1, d)),
            const_spec((d, LANES)),
            const_spec((d, LANES)),
            const_spec((1, LANES)),
        ],
        out_specs=[pl.BlockSpec((tm, d), lambda i: (i, 0)), pl.BlockSpec((tm, d // 2), lambda i: (i, 0)),
                   lane_spec, lane_spec, lane_spec],
        out_shape=[jax.ShapeDtypeStruct((m, d), F32), jax.ShapeDtypeStruct((m, d // 2), jnp.uint32),
                   jax.ShapeDtypeStruct((m, LANES), F32), jax.ShapeDtypeStruct((m, LANES), jnp.int32),
                   jax.ShapeDtypeStruct((m, LANES), F32)],
        compiler_params=_params("arbitrary"),
        name="xattn_ln",
    )(h, wq_b, kv_b, wo_b, g, b, *_split_bf16(w_router), b_router)


def _route_top4(h, w_hi, w_lo, b):
    h_hi, h_lo = _split_bf16(h)
    both = jnp.dot(h_hi, jnp.concatenate([w_hi, w_lo], axis=1), preferred_element_type=F32)
    logits = both[:, :LANES] + both[:, LANES:] + jnp.dot(h_lo, w_hi, preferred_element_type=F32) + b
    lane = lax.broadcasted_iota(jnp.int32, logits.shape, 1)
    neg = jnp.float32(-jnp.inf)
    masked = jnp.where(lane < N_EXPERTS, logits, neg)
    top_vals, top_idx = [], []
    sel = jnp.zeros_like(logits)
    for _ in range(TOP_K):
        mval = jnp.max(masked, axis=-1, keepdims=True)
        idx = jnp.min(jnp.where(masked == mval, lane, LANES), axis=-1, keepdims=True)
        hot = lane == idx
        top_vals.append(mval)
        top_idx.append(idx)
        sel = jnp.where(hot, 1.0, sel)
        masked = jnp.where(hot, neg, masked)
    exps = [jnp.exp(v - top_vals[0]) for v in top_vals]
    denom = exps[0]
    for e in exps[1:]:
        denom = denom + e
    topw = jnp.zeros_like(logits)
    topi = jnp.zeros(logits.shape, jnp.int32)
    for k in range(TOP_K):
        topw = jnp.where(lane == k, exps[k] / denom, topw)
        topi = jnp.where(lane == k, top_idx[k], topi)
    return topw, topi, sel


MOE_TM = 256


def moe_route_plan(topi, sel, *, tm):
    t = topi.shape[0]
    n_tiles = t * TOP_K // tm + N_EXPERTS
    order = jnp.argsort(topi[:, :TOP_K].reshape(-1), stable=True).astype(jnp.int32)
    counts = jnp.sum(sel[:, :N_EXPERTS], axis=0).astype(jnp.int32)
    first_pair = jnp.cumsum(counts) - counts
    tiles_e = (counts + tm - 1) // tm
    tile_end = jnp.cumsum(tiles_e)
    tile_start = tile_end - tiles_e
    tile_ids = jnp.arange(n_tiles, dtype=jnp.int32)
    tile_expert = jnp.minimum(jnp.sum(tile_end[None, :] <= tile_ids[:, None], axis=1), N_EXPERTS - 1).astype(jnp.int32)
    per_tile = jnp.stack([tile_start, counts, first_pair], axis=1)[tile_expert]
    local = (tile_ids - per_tile[:, 0])[:, None] * tm + jnp.arange(tm, dtype=jnp.int32)[None, :]
    valid = (tile_ids < tile_end[-1])[:, None] & (local < per_tile[:, 1:2])
    pair = order[jnp.clip(per_tile[:, 2:3] + local, 0, TOP_K * t - 1)]
    pad_row = TOP_K * t + (tile_ids % 2)[:, None] * tm + jnp.arange(tm, dtype=jnp.int32)[None, :]
    ydst = jnp.where(valid, (pair % TOP_K) * t + pair // TOP_K, pad_row)
    tok = ydst % t
    n_valid = tile_end[-1:].astype(jnp.int32)
    return (tok.reshape(n_tiles, 1, tm), ydst.reshape(n_tiles, 1, tm), tile_expert, n_valid)


def _moe_ffn_kernel(te_ref, nv_ref, tok0_ref, tok1_ref, tok2_ref, ydst_ref, h_hbm, wg_ref, wu_ref, wd_ref,
                    bg_ref, bu_ref, bd_ref, y_hbm, xg, og, wgu_b, wd_b, gsem, ssem, *, tm, ff):
    r = pl.program_id(0)
    nv = nv_ref[0]
    slot = lax.rem(r, 2)
    gslot = lax.rem(r, 3)

    def row_gather(idx_ref, s):
        for i in range(tm):
            pltpu.make_async_copy(h_hbm.at[pl.ds(idx_ref[0, i], 1), :], xg.at[s, pl.ds(i, 1), :], gsem.at[s]).start()

    def gather_wait(s):
        pltpu.make_async_copy(h_hbm.at[pl.ds(0, tm), :], xg.at[s], gsem.at[s]).wait()

    def scatter_wait(s):
        pltpu.make_async_copy(og.at[s], y_hbm.at[pl.ds(0, tm), :], ssem.at[s]).wait()

    @pl.when(r == 0)
    def _():
        row_gather(tok0_ref, 0)
        row_gather(tok1_ref, 1)
        og[1] = jnp.zeros(og.shape[1:], og.dtype)
        base = y_hbm.shape[0] - 2 * tm
        for part in range(2):
            fill = pltpu.make_async_copy(og.at[1], y_hbm.at[pl.ds(base + part * tm, tm), :], ssem.at[1])
            fill.start()
            fill.wait()

    @pl.when(r < nv)
    def _():
        gather_wait(gslot)

        @pl.when(r >= 2)
        def _():
            scatter_wait(slot)

        @pl.when((r == 0) | (te_ref[r] != te_ref[jnp.maximum(r - 1, 0)]))
        def _():
            wgu_b[:, :ff] = wg_ref[...].astype(BF16)
            wgu_b[:, ff:] = wu_ref[...].astype(BF16)
            wd_b[...] = wd_ref[...].astype(BF16)

        x_hi, x_lo = _unpack_bf16_pair(xg[gslot])
        half = x_hi.shape[1]
        gu = (jnp.dot(x_hi.astype(BF16), wgu_b[:half, :], preferred_element_type=F32)
              + jnp.dot(x_lo.astype(BF16), wgu_b[half:, :], preferred_element_type=F32))
        g = jnp.minimum(gu[:, :ff] + bg_ref[...], SWIGLU_LIMIT)
        u = jnp.clip(gu[:, ff:] + bu_ref[...], -SWIGLU_LIMIT, SWIGLU_LIMIT)
        act = (g * jax.nn.sigmoid(SWIGLU_ALPHA * g) * (u + 1.0)).astype(BF16)
        res = _pack_bf16_pair(jnp.dot(act, wd_b[...], preferred_element_type=F32) + bd_ref[...])
        row_gather(tok2_ref, lax.rem(r + 2, 3))
        og[slot] = res
        for i in range(tm):
            pltpu.make_async_copy(og.at[slot, pl.ds(i, 1), :], y_hbm.at[pl.ds(ydst_ref[0, i], 1), :],
                                  ssem.at[slot]).start()

        @pl.when(r == nv - 1)
        def _():
            gather_wait(lax.rem(r + 1, 3))
            gather_wait(lax.rem(r + 2, 3))
            scatter_wait(slot)

            @pl.when(r >= 1)
            def _():
                scatter_wait(1 - slot)


def moe_experts(hp, plan, w_gate, b_gate, w_up, b_up, w_down, b_down, layer, *, tm):
    tok, ydst, tile_expert, n_valid = plan
    t = hp.shape[0]
    d = 2 * hp.shape[1]
    n_tiles = tok.shape[0]
    n_e, ff = w_gate.shape[1], w_gate.shape[3]
    bg = b_gate.reshape(b_gate.shape[0], n_e, 1, ff)
    bu = b_up.reshape(b_up.shape[0], n_e, 1, ff)
    bd = b_down.reshape(b_down.shape[0], n_e, 1, d)

    def expert(r, te, nv):
        return te[jnp.minimum(r, nv[0] - 1)]

    smem_blk = functools.partial(pl.BlockSpec, (None, 1, tm), memory_space=pltpu.SMEM)
    grid_spec = pltpu.PrefetchScalarGridSpec(
        num_scalar_prefetch=2,
        grid=(n_tiles,),
        in_specs=[
            smem_blk(lambda r, te, nv: (jnp.minimum(r, nv[0] - 1), 0, 0)),
            smem_blk(lambda r, te, nv: (jnp.minimum(r + 1, nv[0] - 1), 0, 0)),
            smem_blk(lambda r, te, nv: (jnp.minimum(r + 2, nv[0] - 1), 0, 0)),
            smem_blk(lambda r, te, nv: (r, 0, 0)),
            pl.BlockSpec(memory_space=pl.ANY),
            pl.BlockSpec((None, None, d, ff), lambda r, te, nv: (layer, expert(r, te, nv), 0, 0)),
            pl.BlockSpec((None, None, d, ff), lambda r, te, nv: (layer, expert(r, te, nv), 0, 0)),
            pl.BlockSpec((None, None, ff, d), lambda r, te, nv: (layer, expert(r, te, nv), 0, 0)),
            pl.BlockSpec((None, None, 1, ff), lambda r, te, nv: (layer, expert(r, te, nv), 0, 0)),
            pl.BlockSpec((None, None, 1, ff), lambda r, te, nv: (layer, expert(r, te, nv), 0, 0)),
            pl.BlockSpec((None, None, 1, d), lambda r, te, nv: (layer, expert(r, te, nv), 0, 0)),
        ],
        out_specs=pl.BlockSpec(memory_space=pl.ANY),
        scratch_shapes=[
            pltpu.VMEM((3, tm, d // 2), jnp.uint32),
            pltpu.VMEM((2, tm, d // 2), jnp.uint32),
            pltpu.VMEM((d, 2 * ff), BF16),
            pltpu.VMEM((ff, d), BF16),
            pltpu.SemaphoreType.DMA((3,)),
            pltpu.SemaphoreType.DMA((2,)),
        ],
    )
    return pl.pallas_call(
        functools.partial(_moe_ffn_kernel, tm=tm, ff=ff),
        grid_spec=grid_spec,
        out_shape=jax.ShapeDtypeStruct((TOP_K * t + 2 * tm, d // 2), jnp.uint32),
        compiler_params=_params("arbitrary"),
        name="moe_experts",
    )(tile_expert, n_valid, tok, tok, tok, ydst, hp, w_gate, w_up, w_down, bg, bu, bd)


def _moe_combine_kernel(y0_ref, y1_ref, y2_ref, y3_ref, w_ref, h_ref, g_ref, b_ref, of_ref, ob_ref, *, alpha, rows):
    tm = h_ref.shape[0]
    y_refs = (y0_ref, y1_ref, y2_ref, y3_ref)

    def body(rr, carry):
        sl = pl.ds(pl.multiple_of(rr * rows, rows), rows)
        w = w_ref[sl, :]
        hrow = h_ref[sl, :]
        half = hrow.shape[1] // 2
        z_hi = alpha * hrow[:, :half]
        z_lo = alpha * hrow[:, half:]
        for k, y_ref in enumerate(y_refs):
            y_hi, y_lo = _unpack_bf16_pair(y_ref[sl, :])
            z_hi = z_hi + w[:, k:k + 1] * y_hi
            z_lo = z_lo + w[:, k:k + 1] * y_lo
        z = jnp.concatenate([z_hi, z_lo], axis=1)
        mu = jnp.mean(z, axis=-1, keepdims=True)
        zc = z - mu
        var = jnp.mean(zc * zc, axis=-1, keepdims=True)
        y = zc * lax.rsqrt(var + LN_EPS) * g_ref[...] + b_ref[...]
        of_ref[sl, :] = y
        ob_ref[sl, :] = y.astype(BF16)
        return carry

    lax.fori_loop(0, tm // rows, body, 0)


def moe_combine_ln(y, topw, h, g, b, alpha, *, tm=128):
    t, d = h.shape
    nb = t // tm

    def y_spec(k):
        return pl.BlockSpec((tm, d // 2), lambda i: (k * nb + i, 0))

    row_spec = pl.BlockSpec((tm, d), lambda i: (i, 0))
    return pl.pallas_call(
        functools.partial(_moe_combine_kernel, alpha=alpha, rows=32),
        grid=(nb,),
        in_specs=[y_spec(0), y_spec(1), y_spec(2), y_spec(3),
                  pl.BlockSpec((tm, LANES), lambda i: (i, 0)), row_spec,
                  pl.BlockSpec((1, d), lambda i: (0, 0)), pl.BlockSpec((1, d), lambda i: (0, 0))],
        out_specs=[row_spec, row_spec],
        out_shape=[jax.ShapeDtypeStruct((t, d), F32), jax.ShapeDtypeStruct((t, d), BF16)],
        compiler_params=_params("arbitrary"),
        name="moe_combine_ln",
    )(y, y, y, y, topw, h, g, b)


NEG = -1e30
LOG2E = 1.4426950408889634
MIX_CHUNK = 256
BAND_BLOCK = 256
SB_TQ = 512
SB_TK = 256

def _head_norm_rows(x, gain):
    mu = jnp.mean(x, axis=-1, keepdims=True)
    xc = x - mu
    var = jnp.mean(xc * xc, axis=-1, keepdims=True)
    return xc * lax.rsqrt(var + HN_EPS) * gain


def _log_sigmoid(x):
    return jnp.minimum(x, 0.0) - jnp.log(1.0 + jnp.exp(-jnp.abs(x)))


def _split_bf16(x):
    hi = x.astype(BF16)
    lo = (x - hi.astype(F32)).astype(BF16)
    return hi, lo


def _lane_select(x, lane_idx):
    lane = lax.broadcasted_iota(jnp.int32, x.shape, 1)
    col = jnp.sum(jnp.where(lane == lane_idx, x, 0.0), axis=-1, keepdims=True)
    return jnp.broadcast_to(col, x.shape)


def _retention_body(q_ref, k_ref, v_ref, g_ref, cos_ref, sin_ref, lg_ref, gain_ref, o_ref,
                    qs_ref, ks_ref, st_ref, *, scale):
    seq = q_ref.shape[0]
    L = MIX_CHUNK
    half = HEAD_DIM // 2
    cos = cos_ref[...]
    sin = sin_ref[...]
    q = q_ref[...].astype(F32)
    k = k_ref[...].astype(F32)
    qs_ref[...] = (q * cos + pltpu.roll(q, half, 1) * sin).astype(BF16)
    ks_ref[...] = ((k * cos + pltpu.roll(k, half, 1) * sin) * scale).astype(BF16)

    lg = lg_ref[...]
    ri = lax.broadcasted_iota(jnp.int32, (L, L), 0)
    ci = lax.broadcasted_iota(jnp.int32, (L, L), 1)
    intra = jnp.where(ri >= ci, jnp.exp(lg * jnp.maximum(ri - ci, 0).astype(F32)), 0.0)
    rr = lax.broadcasted_iota(jnp.int32, (L, HEAD_DIM), 0).astype(F32)
    lg_d = lg[:, :HEAD_DIM]
    q_dec = jnp.exp(lg_d * (rr + 1.0))
    k_dec = jnp.exp(lg_d * (L - 1.0 - rr))
    c_dec = jnp.exp(lg_d * float(L))
    gain = gain_ref[...]
    st_ref[...] = jnp.zeros_like(st_ref)

    def chunk(c, carry):
        sl = pl.ds(pl.multiple_of(c * L, L), L)
        qc = qs_ref[sl, :]
        kc = ks_ref[sl, :]
        vc = v_ref[sl, :]
        state = st_ref[:, :HEAD_DIM]
        att = lax.dot_general(qc, kc, _NT, preferred_element_type=F32) * intra
        o = (jnp.dot(att.astype(BF16), vc, preferred_element_type=F32)
             + jnp.dot((qc.astype(F32) * q_dec).astype(BF16), state.astype(BF16), preferred_element_type=F32))
        st_ref[:, :HEAD_DIM] = state * c_dec + lax.dot_general(
            (kc.astype(F32) * k_dec).astype(BF16), vc, _TN, preferred_element_type=F32)
        gv = g_ref[sl, :].astype(F32)
        o_ref[sl, :] = (_head_norm_rows(o, gain) * (gv * jax.nn.sigmoid(gv))).astype(o_ref.dtype)
        return carry

    lax.fori_loop(0, seq // L, chunk, 0, unroll=True)


def _mlstm_body(u_ref, v_ref, og_ref, gates_ref, cw_ref, cb_ref, wq_ref, wk_ref, gain_ref, o_ref,
                qs_ref, ks_ref, st_ref, m_ref, *, head, heads, scale):
    seq = u_ref.shape[0]
    L = MIX_CHUNK
    d = HEAD_DIM
    x = u_ref[...].astype(F32)
    row = lax.broadcasted_iota(jnp.int32, x.shape, 0)
    cw = cw_ref[...]
    y = x * cw[CONV_K - 1:CONV_K, :] + cb_ref[...]
    for sh in range(1, CONV_K):
        xs = jnp.where(row >= sh, pltpu.roll(x, sh, 0), 0.0)
        y = y + xs * cw[CONV_K - 1 - sh:CONV_K - sh, :]
    ub = (y * jax.nn.sigmoid(y)).astype(BF16)
    qs_ref[...] = jnp.dot(ub, wq_ref[...].astype(BF16), preferred_element_type=F32).astype(BF16)
    ks_ref[...] = (jnp.dot(ub, wk_ref[...].astype(BF16), preferred_element_type=F32) * scale).astype(BF16)

    ri = lax.broadcasted_iota(jnp.int32, (L, L), 0)
    ci = lax.broadcasted_iota(jnp.int32, (L, L), 1)
    causal = ri >= ci
    tri = jnp.where(causal, 1.0, 0.0).astype(BF16)
    ones_v = jnp.ones((L, d), BF16)
    gain = gain_ref[...]
    st_ref[...] = jnp.zeros_like(st_ref)
    m_ref[...] = jnp.zeros_like(m_ref)

    def chunk(c, carry):
        sl = pl.ds(pl.multiple_of(c * L, L), L)
        qc = qs_ref[sl, :]
        kc = ks_ref[sl, :]
        v_ext = jnp.concatenate([v_ref[sl, :], ones_v], axis=1)
        gts = gates_ref[sl, :]
        ic = _lane_select(gts, head)
        lf = _log_sigmoid(_lane_select(gts, heads + head))
        lf_hi, lf_lo = _split_bf16(lf)
        bcum = (jnp.dot(tri, lf_hi, preferred_element_type=F32)
                + jnp.dot(tri, lf_lo, preferred_element_type=F32))
        m_st = m_ref[...]
        src = jnp.transpose(ic - bcum)[:1, :]
        bcum2 = jnp.concatenate([bcum, bcum], axis=1)
        log_intra = jnp.where(causal, bcum2 + src, NEG)
        m_intra = jnp.max(log_intra, axis=-1, keepdims=True)
        log_cross = bcum + m_st
        m_row = jnp.maximum(log_cross, m_intra)
        m_row2 = jnp.concatenate([m_row, m_row], axis=1)
        w_intra = jnp.exp(log_intra - m_row2)
        w_cross = jnp.exp(log_cross - m_row)
        w_cross2 = jnp.concatenate([w_cross, w_cross], axis=1)
        qk = lax.dot_general(qc, kc, _NT, preferred_element_type=F32) * w_intra
        state = st_ref[...]
        res = (jnp.dot(qk.astype(BF16), v_ext, preferred_element_type=F32)
               + w_cross2 * jnp.dot(qc, state.astype(BF16), preferred_element_type=F32))
        num = res[:, :d]
        den = res[:, d:]
        hh = num / jnp.maximum(jnp.abs(den), jnp.exp(-m_row))
        og = og_ref[sl, :].astype(F32)
        o_ref[sl, :] = _head_norm_rows(hh * jax.nn.sigmoid(og), gain).astype(o_ref.dtype)
        b_last = bcum[L - 1:L, :]
        log_state = b_last - bcum + ic
        m_new = jnp.maximum(b_last + m_st, jnp.max(log_state, axis=0, keepdims=True))
        decay = jnp.exp(b_last + m_st - m_new)
        kw = (kc.astype(F32) * jnp.exp(log_state - m_new)).astype(BF16)
        decay2 = jnp.concatenate([decay, decay], axis=1)
        st_ref[...] = decay2 * state + lax.dot_general(kw, v_ext, _TN, preferred_element_type=F32)
        m_ref[...] = m_new
        return carry

    lax.fori_loop(0, seq // L, chunk, 0, unroll=True)


def _mixer_ab_kernel(a0_ref, a1_ref, a2_ref, a3_ref, gates_ref, cos_ref, sin_ref, lg_ref, cw_ref, cb_ref,
                     wq_ref, wk_ref, gain_ref, o_ref, qs_ref, ks_ref, st_ref, m_ref, *, heads, scale):
    g = pl.program_id(1)

    @pl.when(g < heads)
    def _():
        _retention_body(a0_ref, a1_ref, a2_ref, a3_ref, cos_ref, sin_ref, lg_ref, gain_ref, o_ref,
                        qs_ref, ks_ref, st_ref, scale=scale)

    @pl.when(g >= heads)
    def _():
        _mlstm_body(a0_ref, a1_ref, a2_ref, gates_ref, cw_ref, cb_ref, wq_ref, wk_ref, gain_ref, o_ref,
                    qs_ref, ks_ref, st_ref, m_ref, head=g - heads, heads=heads, scale=scale)


def mixer_ab(z, gates, cos2, sin2, log_g, conv_w, conv_b, wq_m, wk_m, gain, layer, *, bsz, seq, heads):
    t = z.shape[0]
    d = HEAD_DIM
    L = MIX_CHUNK

    def col(base_ret, base_ml):
        def index(b, g):
            is_ml = g // heads
            return (b, (1 - is_ml) * (base_ret * heads + g) + is_ml * (base_ml * heads + g - heads))
        return pl.BlockSpec((seq, d), index)

    def ml_head(g):
        return jnp.maximum(g - heads, 0)

    conv_w4 = conv_w.reshape(conv_w.shape[0], CONV_K, heads, d).transpose(0, 2, 1, 3)
    conv_b4 = conv_b.reshape(conv_b.shape[0], heads, 1, d)
    gain4 = gain.reshape(2 * heads, 1, d)
    return pl.pallas_call(
        functools.partial(_mixer_ab_kernel, heads=heads, scale=d ** -0.5),
        grid=(bsz, 2 * heads),
        in_specs=[
            col(0, 4), col(1, 5), col(2, 6), col(3, 6),
            pl.BlockSpec((seq, LANES), lambda b, g: (b, 0)),
            pl.BlockSpec((seq, d), lambda b, g: (0, 0)),
            pl.BlockSpec((seq, d), lambda b, g: (0, 0)),
            pl.BlockSpec((None, 1, L), lambda b, g: (jnp.minimum(g, heads - 1), 0, 0)),
            pl.BlockSpec((None, None, CONV_K, d), lambda b, g: (layer, ml_head(g), 0, 0)),
            pl.BlockSpec((None, None, 1, d), lambda b, g: (layer, ml_head(g), 0, 0)),
            pl.BlockSpec((None, None, d, d), lambda b, g: (layer, ml_head(g), 0, 0)),
            pl.BlockSpec((None, None, d, d), lambda b, g: (layer, ml_head(g), 0, 0)),
            pl.BlockSpec((None, 1, d), lambda b, g: (g, 0, 0)),
        ],
        out_specs=pl.BlockSpec((seq, d), lambda b, g: (b, g)),
        out_shape=jax.ShapeDtypeStruct((t, 2 * heads * d), BF16),
        scratch_shapes=[
            pltpu.VMEM((seq, d), BF16),
            pltpu.VMEM((seq, d), BF16),
            pltpu.VMEM((d, 2 * d), F32),
            pltpu.VMEM((1, d), F32),
        ],
        compiler_params=_params("arbitrary", "arbitrary"),
        name="mixer_ab",
    )(z, z, z, z, gates, cos2, sin2, log_g, conv_w4, conv_b4, wq_m, wk_m, gain4)


def _band_body(q_ref, k_ref, v_ref, bias_ref, o_ref, *, scale):
    seq = q_ref.shape[0]
    bq = BAND_BLOCK
    n_back = bias_ref.shape[0]

    def block(i, carry):
        sl = pl.ds(pl.multiple_of(i * bq, bq), bq)
        q = q_ref[sl, :]
        scores, vals = [], []
        for dlt in range(n_back):
            ks = pl.ds(pl.multiple_of(jnp.maximum(i - dlt, 0) * bq, bq), bq)
            s = lax.dot_general(q, k_ref[ks, :], _NT, preferred_element_type=F32) * (scale * LOG2E) + bias_ref[dlt]
            scores.append(jnp.where(i - dlt >= 0, s, NEG))
            vals.append(v_ref[ks, :])
        m = jnp.max(scores[0], axis=-1, keepdims=True)
        for s in scores[1:]:
            m = jnp.maximum(m, jnp.max(s, axis=-1, keepdims=True))
        acc = jnp.zeros((bq, HEAD_DIM), F32)
        l = jnp.zeros((bq, 1), F32)
        for s, vv in zip(scores, vals):
            p = jnp.exp2(s - m)
            l = l + jnp.sum(p, axis=-1, keepdims=True)
            acc = acc + jnp.dot(p.astype(BF16), vv, preferred_element_type=F32)
        o_ref[sl, :] = (acc / l).astype(o_ref.dtype)
        return carry

    lax.fori_loop(0, seq // bq, block, 0, unroll=True)


def _stick_breaking_body(q_ref, k_ref, v_ref, o_ref, acc_ref, r_ref, *, scale):
    seq = q_ref.shape[0]
    tq, tk = SB_TQ, SB_TK
    n_sub = tq // tk
    ri = lax.broadcasted_iota(jnp.int32, (tk, tk), 0)
    ci = lax.broadcasted_iota(jnp.int32, (tk, tk), 1)
    suffix = jnp.where(ri > ci, 1.0, 0.0).astype(BF16)
    strict = (lax.broadcasted_iota(jnp.int32, (tq, tq), 1)
              < lax.broadcasted_iota(jnp.int32, (tq, tq), 0))

    def one_group(q, k0, masked):
        kb = k_ref[pl.ds(k0, tq), :]
        vb = v_ref[pl.ds(k0, tq), :]
        z = lax.dot_general(q, kb, _NT, preferred_element_type=F32) * (scale * LOG2E)
        sp = jnp.maximum(z, 0.0) + jnp.log2(1.0 + jnp.exp2(-jnp.abs(z)))
        ls_pos = z - sp
        if masked:
            sp = jnp.where(strict, sp, 0.0)
        sp_b = sp.astype(BF16)
        r = r_ref[...]
        pieces = []
        for s in reversed(range(n_sub)):
            sl = slice(s * tk, (s + 1) * tk)
            between = jnp.dot(sp_b[:, sl], suffix, preferred_element_type=F32)
            pieces.append(ls_pos[:, sl] - between - jnp.concatenate([r] * (tk // LANES), axis=1))
            r = r + jnp.sum(sp[:, sl], axis=-1, keepdims=True)
        p = jnp.exp2(jnp.concatenate(pieces[::-1], axis=1))
        if masked:
            p = jnp.where(strict, p, 0.0)
        acc_ref[...] += jnp.dot(p.astype(BF16), vb, preferred_element_type=F32)
        r_ref[...] = r

    for i in range(seq // tq):
        q0 = i * tq
        q = q_ref[pl.ds(q0, tq), :]
        acc_ref[...] = jnp.zeros_like(acc_ref)
        r_ref[...] = jnp.zeros_like(r_ref)
        one_group(q, q0, True)

        def below(jj, c, q=q, i=i):
            one_group(q, pl.multiple_of((i - 1 - jj) * tq, tq), False)
            return c

        lax.fori_loop(0, i, below, 0, unroll=True)
        o_ref[pl.ds(q0, tq), :] = acc_ref[...].astype(o_ref.dtype)


def _mixer_cd_kernel(q_ref, k_ref, v_ref, bias_ref, o_ref, acc_ref, r_ref, *, heads, scale):
    g = pl.program_id(1)

    @pl.when(g < heads)
    def _():
        _band_body(q_ref, k_ref, v_ref, bias_ref, o_ref, scale=scale)

    @pl.when(g >= heads)
    def _():
        _stick_breaking_body(q_ref, k_ref, v_ref, o_ref, acc_ref, r_ref, scale=scale)


def _band_bias_kernel(rrow_ref, o_ref):
    bq = BAND_BLOCK
    shift = CHUNK.bit_length() - 1
    qo = lax.broadcasted_iota(jnp.int32, (bq, bq), 0)
    ck = jnp.right_shift(lax.broadcasted_iota(jnp.int32, (bq, bq), 1), shift)
    for dlt in range(o_ref.shape[0]):
        x = jnp.broadcast_to(rrow_ref[dlt], (bq, 2 * bq))
        toep = pltpu.roll(x, 0, 1, stride=1, stride_axis=0)[:, :bq]
        cq = jnp.right_shift(qo + bq * dlt, shift)
        allowed = (ck <= cq) & (ck >= cq - PAST_CHUNKS)
        o_ref[dlt] = jnp.where(allowed, toep * LOG2E, NEG)


def band_bias_table(rel_bias):
    bq = BAND_BLOCK
    heads = rel_bias.shape[0]
    n_back = PAST_CHUNKS * CHUNK // bq + 1
    m = jnp.arange(2 * bq)
    key_minus_query = jnp.where(m < bq, m, m - 2 * bq)
    dist = bq * jnp.arange(n_back)[:, None] - key_minus_query[None, :]
    idx = jnp.clip(dist, -(CHUNK - 1), REL_MAX) + (CHUNK - 1)
    rrow = rel_bias.astype(F32)[:, idx].reshape(heads, n_back, 1, 2 * bq)
    return pl.pallas_call(
        _band_bias_kernel,
        grid=(heads,),
        in_specs=[pl.BlockSpec((None, n_back, 1, 2 * bq), lambda hh: (hh, 0, 0, 0))],
        out_specs=pl.BlockSpec((None, n_back, bq, bq), lambda hh: (hh, 0, 0, 0)),
        out_shape=jax.ShapeDtypeStruct((heads, n_back, bq, bq), F32),
        compiler_params=_params("arbitrary"),
        name="band_bias",
    )(rrow)


def mixer_cd(z, bias_tab, *, bsz, seq, heads):
    t = z.shape[0]
    d = HEAD_DIM

    def col(which):
        return pl.BlockSpec((seq, d), lambda b, g: (b, (3 * (g // heads) + which) * heads + g % heads))

    return pl.pallas_call(
        functools.partial(_mixer_cd_kernel, heads=heads, scale=d ** -0.5),
        grid=(bsz, 2 * heads),
        in_specs=[
            col(0), col(1), col(2),
            pl.BlockSpec((None,) + bias_tab.shape[1:], lambda b, g: (jnp.minimum(g, heads - 1), 0, 0, 0)),
        ],
        out_specs=pl.BlockSpec((seq, d), lambda b, g: (b, g)),
        out_shape=jax.ShapeDtypeStruct((t, 2 * heads * d), BF16),
        scratch_shapes=[pltpu.VMEM((SB_TQ, d), F32), pltpu.VMEM((SB_TQ, LANES), F32)],
        compiler_params=_params("arbitrary", "arbitrary"),
        name="mixer_cd",
    )(z, z, z, bias_tab)


def rope_tables(seq_len):
    pos = jnp.arange(seq_len, dtype=F32)
    inv_freq = ROPE_BASE ** (-jnp.arange(0, HEAD_DIM, 2, dtype=F32) / HEAD_DIM)
    ang = pos[:, None] * inv_freq[None, :]
    cos, sin = jnp.cos(ang), jnp.sin(ang)
    return jnp.concatenate([cos, cos], axis=1), jnp.concatenate([-sin, sin], axis=1)


def retention_log_decay(heads):
    lg = jnp.log1p(-jnp.exp2(-(5.0 + jnp.arange(heads, dtype=F32))))
    return jnp.broadcast_to(lg[:, None, None], (heads, 1, MIX_CHUNK))


def _pad_lanes(a, value=0.0):
    return jnp.pad(a, ((0, 0), (0, LANES - a.shape[1])), constant_values=value)


def kernel(x, mem, ab_w_in, ab_gate_b, ab_conv_w, ab_conv_b, ab_wq, ab_wk, ab_ret_norm_g, ab_mlstm_norm_g, ab_w_out, cd_w_in, cd_rel_bias, cd_w_out, mix_ln_g, mix_ln_b, xa_wq, xa_wkv, xa_wo, xa_ln_g, xa_ln_b, moe_router_w, moe_router_b, moe_w_gate, moe_b_gate, moe_w_up, moe_b_up, moe_w_down, moe_b_down, moe_ln_g, moe_ln_b):
    bsz, seq, d = x.shape
    depth = mix_ln_g.shape[0]
    heads = GROUP_HEADS
    alpha = (2.0 * depth) ** 0.25
    t = bsz * seq
    cos2, sin2 = rope_tables(seq)
    log_g = retention_log_decay(heads)
    h = x.reshape(t, d)
    hb = h.astype(BF16)
    memb = mem.reshape(-1, d).astype(BF16)
    n_ab = 7 * GROUP_WIDTH
    n_cd = 6 * GROUP_WIDTH
    ab_w_in_t = jnp.swapaxes(ab_w_in, 1, 2)
    kv_all = matmul_all_layers(memb, xa_wkv, tm=memb.shape[0], tn=512, out_dtype=BF16)
    ab_w_out_b = ab_w_out.astype(BF16)
    cd_w_out_b = cd_w_out.astype(BF16)
    for layer in range(depth):
        i = layer // 2
        if layer % 2 == 0:
            z = matmul_nt_stacked(hb, ab_w_in_t, i, n_ab, tm=1024, tn=512, out_dtype=BF16)
            gates = gates_nt(hb, ab_w_in_t, i, n_ab, 2 * heads, _pad_lanes(ab_gate_b[i][None, :]), tm=1024)
            gain = jnp.concatenate([ab_ret_norm_g[i], ab_mlstm_norm_g[i]])
            yb = mixer_ab(z, gates, cos2, sin2, log_g, ab_conv_w, ab_conv_b, ab_wq, ab_wk, gain, i,
                          bsz=bsz, seq=seq, heads=heads)
            w_out = ab_w_out_b
        else:
            z = matmul_stacked(hb, cd_w_in, i, n_cd, tm=1024, tn=512, out_dtype=BF16)
            yb = mixer_cd(z, band_bias_table(cd_rel_bias[i]), bsz=bsz, seq=seq, heads=heads)
            w_out = cd_w_out_b
        h, hb = matmul_ln(yb, w_out, i, h, mix_ln_g[layer][None, :], mix_ln_b[layer][None, :], alpha)

        h, hp, topw, topi, sel = xattn_ln(
            h, xa_wq[layer].astype(BF16), kv_all, layer, xa_wo[layer].astype(BF16),
            xa_ln_g[layer][None, :], xa_ln_b[layer][None, :],
            _pad_lanes(moe_router_w[layer]), _pad_lanes(moe_router_b[layer][None, :]), alpha, seq=seq)
        plan = moe_route_plan(topi, sel, tm=MOE_TM)
        y = moe_experts(hp, plan, moe_w_gate, moe_b_gate, moe_w_up, moe_b_up, moe_w_down, moe_b_down, layer, tm=MOE_TM)
        h, hb = moe_combine_ln(y, topw, h, moe_ln_g[layer][None, :], moe_ln_b[layer][None, :], alpha)
    return h.reshape(bsz, seq, d)
```
